```python
import math
import jax, jax.numpy as jnp
from jax import lax
import numpy as np

D_MODEL = 1024
BATCH = 8
SEQ = 2048
DEPTH = 4

HEAD_DIM = 64
N_HEADS = 4
MIX_W = N_HEADS * HEAD_DIM
N_BRANCH = 4
Q_BLOCK = 128
EPS = 1e-6
NEG = -1e30
FORCE = 1e9
MLA_Q_LORA = 384
MLA_KV_LORA = 128
MLA_NOPE = 64
MLA_ROPE = 32
MLA_V = 64
MLA_QK = MLA_NOPE + MLA_ROPE
ROPE_THETA = 10000.0
NSA_CMP_LEN = 32
NSA_CMP_STRIDE = 16
NSA_SEL_LEN = 64
NSA_TOP_N = 16
NSA_WINDOW = 512
REL_BUCKETS = 32
REL_MAX_DIST = 128
D_FF = 4 * D_MODEL
PLE_DIM = 256
IN_WIDTHS = ((MIX_W,) * 3
             + (MLA_Q_LORA, MLA_KV_LORA, MLA_ROPE)
             + (MIX_W,) + (HEAD_DIM,) * 6 + (3 * N_HEADS,)
             + (MIX_W,) * 3 + (N_HEADS,)
             + (N_BRANCH * D_MODEL,))
N_IN = sum(IN_WIDTHS)

kernel_name = 'hybrid_sb_mla_nsa_fox_block'


def rmsnorm(x, g):
    x32 = x.astype(jnp.float32)
    y = x32 * lax.rsqrt(jnp.mean(x32 * x32, axis=-1, keepdims=True) + EPS)
    return (y * g.astype(jnp.float32)).astype(x.dtype)


def to_heads(t):
    B, S, _ = t.shape
    return t.reshape(B, S, N_HEADS, -1).transpose(0, 2, 1, 3)


def from_heads(t):
    B, H, S, Dh = t.shape
    return t.transpose(0, 2, 1, 3).reshape(B, S, H * Dh)


def unblock(o):
    nb, B, H, Qb, Dv = o.shape
    return o.transpose(1, 2, 0, 3, 4).reshape(B, H, nb * Qb, Dv)


def rel_bucket(dist):
    max_exact = REL_BUCKETS // 2
    d = jnp.maximum(dist, 0)
    large = max_exact + (jnp.log(jnp.maximum(d, 1).astype(jnp.float32) / max_exact)
                         / math.log(REL_MAX_DIST / max_exact)
                         * (REL_BUCKETS - max_exact)).astype(jnp.int32)
    large = jnp.minimum(large, REL_BUCKETS - 1)
    return jnp.where(d < max_exact, d, large)


def rope_tables(S):
    half = MLA_ROPE // 2
    inv = jnp.exp(-math.log(ROPE_THETA) * jnp.arange(half, dtype=jnp.float32) / half)
    ang = jnp.arange(S, dtype=jnp.float32)[:, None] * inv[None, :]
    return jnp.cos(ang), jnp.sin(ang)


def apply_rope(t, cos, sin):
    half = t.shape[-1] // 2
    t1, t2 = t[..., :half], t[..., half:]
    c = cos[None, :, None, :].astype(t.dtype)
    s = sin[None, :, None, :].astype(t.dtype)
    return jnp.concatenate([t1 * c - t2 * s, t2 * c + t1 * s], axis=-1)


def causal_softmax_attention(q, k, v, logit_bias):
    B, H, S, Dk = q.shape
    scale = Dk ** -0.5
    kpos = jnp.arange(S)

    def block(i):
        start = i * Q_BLOCK
        qb = lax.dynamic_slice_in_dim(q, start, Q_BLOCK, axis=2)
        s = jnp.einsum('bhqd,bhkd->bhqk', qb, k).astype(jnp.float32) * scale
        if logit_bias is not None:
            s = s + logit_bias(start)
        qpos = start + jnp.arange(Q_BLOCK)
        s = jnp.where(kpos[None, :] <= qpos[:, None], s, -jnp.inf)
        return jnp.einsum('bhqk,bhkd->bhqd', jax.nn.softmax(s, axis=-1).astype(v.dtype), v)

    return unblock(lax.map(block, jnp.arange(S // Q_BLOCK)))


def stick_breaking_attention(q, k, v):
    B, H, S, Dh = q.shape
    scale = Dh ** -0.5
    kpos = jnp.arange(S)

    def block(i):
        start = i * Q_BLOCK
        qb = lax.dynamic_slice_in_dim(q, start, Q_BLOCK, axis=2)
        z = jnp.einsum('bhqd,bhkd->bhqk', qb, k).astype(jnp.float32) * scale
        qpos = start + jnp.arange(Q_BLOCK)
        past = kpos[None, :] < qpos[:, None]
        log_1m = jnp.where(past, jax.nn.log_sigmoid(-z), 0.0)
        between = lax.cumsum(log_1m, axis=3, reverse=True) - log_1m
        w = jnp.where(past, jnp.exp(jax.nn.log_sigmoid(z) + between), 0.0)
        return jnp.einsum('bhqk,bhkd->bhqd', w.astype(v.dtype), v)

    return unblock(lax.map(block, jnp.arange(S // Q_BLOCK)))


def mla_mixer(c_q, c_kv, k_rope, cq_g, ckv_g, w_uq, w_ukv, qn_g, kn_g):
    B, S, _ = c_q.shape
    q = (rmsnorm(c_q, cq_g) @ w_uq).reshape(B, S, N_HEADS, MLA_QK)
    kv = (rmsnorm(c_kv, ckv_g) @ w_ukv).reshape(B, S, N_HEADS, MLA_NOPE + MLA_V)
    k_nope, v = kv[..., :MLA_NOPE], kv[..., MLA_NOPE:]
    k = jnp.concatenate([k_nope, jnp.broadcast_to(k_rope[:, :, None, :], (B, S, N_HEADS, MLA_ROPE))], axis=-1)
    q = rmsnorm(q, qn_g)
    k = rmsnorm(k, kn_g)
    cos, sin = rope_tables(S)
    q = jnp.concatenate([q[..., :MLA_NOPE], apply_rope(q[..., MLA_NOPE:], cos, sin)], axis=-1)
    k = jnp.concatenate([k[..., :MLA_NOPE], apply_rope(k[..., MLA_NOPE:], cos, sin)], axis=-1)
    o = causal_softmax_attention(q.transpose(0, 2, 1, 3), k.transpose(0, 2, 1, 3),
                                 v.transpose(0, 2, 1, 3), None)
    return from_heads(o)


def nsa_mixer(q, k_cmp, v_cmp, k_slc, v_slc, k_win, v_win, g_logit,
              pe_k, pe_v, w1_k, w2_k, w1_v, w2_v, qn_g, kn_g, rel_bias):
    B, S, _ = q.shape
    Dh = HEAD_DIM
    scale = Dh ** -0.5
    q = rmsnorm(to_heads(q), qn_g)
    tpos = jnp.arange(S)
    table = rel_bias.astype(jnp.float32)

    n_cmp = (S - NSA_CMP_LEN) // NSA_CMP_STRIDE + 1
    starts = jnp.arange(n_cmp) * NSA_CMP_STRIDE
    gidx = starts[:, None] + jnp.arange(NSA_CMP_LEN)[None, :]

    def compress(t, pe, w1, w2):
        blocks = (t[:, gidx, :] + pe).reshape(B, n_cmp, NSA_CMP_LEN * Dh)
        return jax.nn.silu(blocks @ w1) @ w2

    kc = rmsnorm(compress(k_cmp, pe_k, w1_k, w2_k), kn_g[0])
    vc = compress(v_cmp, pe_v, w1_v, w2_v)
    cmp_dist = tpos[:, None] - (starts + NSA_CMP_LEN - 1)[None, :]
    cmp_valid = cmp_dist >= 0
    s_c = (jnp.einsum('bhsd,bcd->bhsc', q, kc).astype(jnp.float32) * scale
           + table[rel_bucket(cmp_dist)].transpose(2, 0, 1)[None])
    p_c = jax.nn.softmax(jnp.where(cmp_valid, s_c, NEG), axis=-1)
    p_c = jnp.where(cmp_valid, p_c, 0.0)
    o_cmp = jnp.einsum('bhsc,bcd->bhsd', p_c.astype(vc.dtype), vc)

    n_sel = S // NSA_SEL_LEN
    c0 = starts[:, None]
    j0 = (jnp.arange(n_sel) * NSA_SEL_LEN)[None, :]
    overlap = ((c0 < j0 + NSA_SEL_LEN) & (c0 + NSA_CMP_LEN > j0)).astype(jnp.float32)
    imp = jnp.einsum('bhsc,cj->bsj', p_c, overlap)
    blk = jnp.arange(n_sel)[None, :]
    cur = (tpos // NSA_SEL_LEN)[:, None]
    forced = (blk == 0) | (blk == cur) | (blk == cur - 1)
    imp = jnp.where(forced, FORCE, imp)
    imp = jnp.where(blk <= cur, imp, -FORCE)
    top_n = min(NSA_TOP_N, n_sel)
    _, sel = lax.top_k(imp, top_n)

    ks = rmsnorm(k_slc, kn_g[1])
    kw = rmsnorm(k_win, kn_g[2])
    kw_pad = jnp.pad(kw, ((0, 0), (NSA_WINDOW, 0), (0, 0)))
    vw_pad = jnp.pad(v_win, ((0, 0), (NSA_WINDOW, 0), (0, 0)))
    bidx = jnp.arange(B)[:, None, None]
    offs = jnp.arange(NSA_SEL_LEN)
    n_keys = top_n * NSA_SEL_LEN

    def block(i):
        start = i * Q_BLOCK
        qb = lax.dynamic_slice_in_dim(q, start, Q_BLOCK, axis=2)
        qpos = start + jnp.arange(Q_BLOCK)
        sb = lax.dynamic_slice_in_dim(sel, start, Q_BLOCK, axis=1)
        kpos = (sb[..., None] * NSA_SEL_LEN + offs).reshape(B, Q_BLOCK, n_keys)
        kg = ks[bidx, kpos]
        vg = v_slc[bidx, kpos]
        dist = qpos[None, :, None] - kpos
        s = (jnp.einsum('bhqd,bqkd->bhqk', qb, kg).astype(jnp.float32) * scale
             + table[rel_bucket(dist)].transpose(0, 3, 1, 2))
        s = jnp.where((dist >= 0)[:, None], s, -jnp.inf)
        o_s = jnp.einsum('bhqk,bqkd->bhqd', jax.nn.softmax(s, axis=-1).astype(vg.dtype), vg)
        kb = lax.dynamic_slice_in_dim(kw_pad, start, NSA_WINDOW + Q_BLOCK, axis=1)
        vb = lax.dynamic_slice_in_dim(vw_pad, start, NSA_WINDOW + Q_BLOCK, axis=1)
        wpos = start - NSA_WINDOW + jnp.arange(NSA_WINDOW + Q_BLOCK)
        wdist = qpos[:, None] - wpos[None, :]
        wmask = (wdist >= 0) & (wdist < NSA_WINDOW) & (wpos[None, :] >= 0)
        s_w = (jnp.einsum('bhqd,bkd->bhqk', qb, kb).astype(jnp.float32) * scale
               + table[rel_bucket(wdist)].transpose(2, 0, 1)[None])
        s_w = jnp.where(wmask, s_w, -jnp.inf)
        o_w = jnp.einsum('bhqk,bkd->bhqd', jax.nn.softmax(s_w, axis=-1).astype(vb.dtype), vb)
        return o_s, o_w

    o_slc, o_win = lax.map(block, jnp.arange(S // Q_BLOCK))
    o_slc = unblock(o_slc)
    o_win = unblock(o_win)
    g = jax.nn.sigmoid(g_logit.astype(jnp.float32)).reshape(B, S, 3, N_HEADS).transpose(2, 0, 3, 1)[..., None]
    o = g[0] * o_cmp + g[1] * o_slc + g[2] * o_win
    return from_heads(o.astype(q.dtype))


def fox_mixer(q, k, v, f_logit, f_bias, qn_g, kn_g):
    q = rmsnorm(to_heads(q), qn_g)
    k = rmsnorm(to_heads(k), kn_g)
    v = to_heads(v)
    log_f = jax.nn.log_sigmoid(f_logit.astype(jnp.float32) + f_bias.astype(jnp.float32))
    cum = lax.cumsum(log_f, axis=1).transpose(0, 2, 1)

    def decay_bias(start):
        cq = lax.dynamic_slice_in_dim(cum, start, Q_BLOCK, axis=2)
        return cq[..., :, None] - cum[..., None, :]

    return from_heads(causal_softmax_attention(q, k, v, decay_bias))


def setup_inputs(seed: int = 0) -> dict:
    key = jax.random.key(seed)
    keys = list(jax.random.split(key, 40))
    f32 = jnp.float32

    def nrm(shape, scale):
        return scale * jax.random.normal(keys.pop(), shape, f32)

    def gain(shape):
        return 1.0 + 0.05 * jax.random.normal(keys.pop(), shape, f32)

    L = DEPTH
    cmp_in = NSA_CMP_LEN * HEAD_DIM
    return {
        'x': nrm((BATCH, SEQ, D_MODEL), 1.0),
        'p': nrm((DEPTH, BATCH, SEQ, PLE_DIM), 1.0),
        'rel_bias': nrm((REL_BUCKETS, N_HEADS), 0.5),
        'norm_mix_g': gain((L, D_MODEL)),
        'w_in': nrm((L, D_MODEL, N_IN), D_MODEL ** -0.5),
        'mla_cq_norm_g': gain((L, MLA_Q_LORA)),
        'mla_ckv_norm_g': gain((L, MLA_KV_LORA)),
        'mla_w_uq': nrm((L, MLA_Q_LORA, N_HEADS * MLA_QK), MLA_Q_LORA ** -0.5),
        'mla_w_ukv': nrm((L, MLA_KV_LORA, N_HEADS * (MLA_NOPE + MLA_V)), MLA_KV_LORA ** -0.5),
        'mla_qn_g': gain((L, MLA_QK)),
        'mla_kn_g': gain((L, MLA_QK)),
        'nsa_pe_k': nrm((L, NSA_CMP_LEN, HEAD_DIM), 0.1),
        'nsa_pe_v': nrm((L, NSA_CMP_LEN, HEAD_DIM), 0.1),
        'nsa_w1_k': nrm((L, cmp_in, HEAD_DIM), cmp_in ** -0.5),
        'nsa_w2_k': nrm((L, HEAD_DIM, HEAD_DIM), HEAD_DIM ** -0.5),
        'nsa_w1_v': nrm((L, cmp_in, HEAD_DIM), cmp_in ** -0.5),
        'nsa_w2_v': nrm((L, HEAD_DIM, HEAD_DIM), HEAD_DIM ** -0.5),
        'nsa_qn_g': gain((L, HEAD_DIM)),
        'nsa_kn_g': gain((L, 3, HEAD_DIM)),
        'fox_f_bias': 2.0 + nrm((L, N_HEADS), 0.5),
        'fox_qn_g': gain((L, HEAD_DIM)),
        'fox_kn_g': gain((L, HEAD_DIM)),
        'w_branch': nrm((L, N_BRANCH, MIX_W, D_MODEL), MIX_W ** -0.5),
        'w_o': nrm((L, D_MODEL, D_MODEL), D_MODEL ** -0.5),
        'norm_mlp_g': gain((L, D_MODEL)),
        'w_mlp_up': nrm((L, D_MODEL, D_FF), D_MODEL ** -0.5),
        'w_mlp_down': nrm((L, D_FF, D_MODEL), D_FF ** -0.5),
        'norm_ple_g': gain((L, D_MODEL)),
        'w_ple_gate': nrm((L, D_MODEL, D_MODEL), D_MODEL ** -0.5),
        'w_ple_proj': nrm((L, PLE_DIM, D_MODEL), PLE_DIM ** -0.5),
    }


def reference(x, p, rel_bias, norm_mix_g, w_in, mla_cq_norm_g, mla_ckv_norm_g,
              mla_w_uq, mla_w_ukv, mla_qn_g, mla_kn_g, nsa_pe_k, nsa_pe_v,
              nsa_w1_k, nsa_w2_k, nsa_w1_v, nsa_w2_v, nsa_qn_g, nsa_kn_g,
              fox_f_bias, fox_qn_g, fox_kn_g, w_branch, w_o, norm_mlp_g,
              w_mlp_up, w_mlp_down, norm_ple_g, w_ple_gate, w_ple_proj):
    B, S, _ = x.shape
    offsets = np.cumsum(IN_WIDTHS)[:-1].tolist()
    for i in range(DEPTH):
        h = rmsnorm(x, norm_mix_g[i])
        (sb_q, sb_k, sb_v, mla_cq, mla_ckv, mla_kr,
         nsa_q, nsa_kc, nsa_vc, nsa_ks, nsa_vs, nsa_kw, nsa_vw, nsa_g,
         fox_q, fox_k, fox_v, fox_f, gate_logits) = jnp.split(h @ w_in[i], offsets, axis=-1)

        y_sb = from_heads(stick_breaking_attention(to_heads(sb_q), to_heads(sb_k), to_heads(sb_v)))
        y_mla = mla_mixer(mla_cq, mla_ckv, mla_kr, mla_cq_norm_g[i], mla_ckv_norm_g[i],
                          mla_w_uq[i], mla_w_ukv[i], mla_qn_g[i], mla_kn_g[i])
        y_nsa = nsa_mixer(nsa_q, nsa_kc, nsa_vc, nsa_ks, nsa_vs, nsa_kw, nsa_vw, nsa_g,
                          nsa_pe_k[i], nsa_pe_v[i], nsa_w1_k[i], nsa_w2_k[i],
                          nsa_w1_v[i], nsa_w2_v[i], nsa_qn_g[i], nsa_kn_g[i], rel_bias)
        y_fox = fox_mixer(fox_q, fox_k, fox_v, fox_f, fox_f_bias[i], fox_qn_g[i], fox_kn_g[i])

        branches = jnp.stack([y_sb, y_mla, y_nsa, y_fox], axis=2)
        widened = jnp.einsum('bsnc,ncd->bsnd', branches, w_branch[i])
        gates = jax.nn.sigmoid(gate_logits.reshape(B, S, N_BRANCH, D_MODEL))
        x = x + jnp.einsum('bsnd,de->bse', gates * widened, w_o[i])

        h2 = rmsnorm(x, norm_mlp_g[i])
        x = x + jnp.square(jax.nn.relu(h2 @ w_mlp_up[i])) @ w_mlp_down[i]

        ple_gate = jax.nn.sigmoid(rmsnorm(x, norm_ple_g[i]) @ w_ple_gate[i])
        x = x + ple_gate * (p[i] @ w_ple_proj[i])
    return x
```

```python
import functools
import math

import numpy as np
import jax
import jax.numpy as jnp
from jax import lax
from jax.experimental import pallas as pl
from jax.experimental.pallas import tpu as pltpu

F32 = jnp.float32
BF16 = jnp.bfloat16

D_MODEL = 1024
DEPTH = 4
HEAD_DIM = 64
N_HEADS = 4
MIX_W = N_HEADS * HEAD_DIM
N_BRANCH = 4
EPS = 1e-6
FORCE = 1e9
MLA_Q_LORA = 384
MLA_KV_LORA = 128
MLA_NOPE = 64
MLA_ROPE = 32
MLA_V = 64
MLA_QK = MLA_NOPE + MLA_ROPE
ROPE_THETA = 10000.0
NSA_CMP_LEN = 32
NSA_CMP_STRIDE = 16
NSA_SEL_LEN = 64
NSA_TOP_N = 16
NSA_WINDOW = 512
REL_BUCKETS = 32
REL_MAX_DIST = 128
D_FF = 4 * D_MODEL
PLE_DIM = 256

LANES = 128
MASKED = -1e30
VMEM_LIMIT = 56 * 1024 * 1024

IN_WIDTHS = ((MIX_W,) * 3
             + (MLA_Q_LORA, MLA_KV_LORA, MLA_ROPE)
             + (MIX_W,) + (HEAD_DIM,) * 6 + (3 * N_HEADS,)
             + (MIX_W,) * 3 + (N_HEADS,)
             + (N_BRANCH * D_MODEL,))

COL_CQ = 0
COL_CKV = 384
COL_K2 = 512
COL_V2 = 640
COL_KCVC = 768
COL_MISC = 896
COL_GATE = 1024
COL_NSAQ = 5120
COL_SB = 5376
COL_FOX = 6144
N_PACKED = 6912
MISC_F = 0
MISC_G = 4
MISC_KR = 32


def _cparams(*sem):
    return pltpu.CompilerParams(dimension_semantics=sem, vmem_limit_bytes=VMEM_LIMIT)


def _nt_dot(a, b):
    return lax.dot_general(a, b, (((1,), (1,)), ((), ())), preferred_element_type=F32)


def _dot(a, b):
    return jnp.dot(a, b, preferred_element_type=F32)


def _split3(x):
    hi = x.astype(BF16)
    r1 = x - hi.astype(F32)
    mid = r1.astype(BF16)
    lo = (r1 - mid.astype(F32)).astype(BF16)
    return hi, mid, lo


def _softplus(z):
    return jnp.maximum(z, 0.0) + jnp.log(1.0 + jnp.exp(-jnp.abs(z)))


def _sigmoid(z):
    return 1.0 / (1.0 + jnp.exp(-z))


def _rms_rows(x, g):
    r = lax.rsqrt(jnp.mean(x * x, axis=-1, keepdims=True) + EPS)
    return x * r * g


def _pair_rms(x, g, lo):
    x2 = x * x
    s0 = jnp.sum(jnp.where(lo, x2, 0.0), axis=-1, keepdims=True)
    s1 = jnp.sum(jnp.where(lo, 0.0, x2), axis=-1, keepdims=True)
    r = jnp.where(lo, lax.rsqrt(s0 / HEAD_DIM + EPS), lax.rsqrt(s1 / HEAD_DIM + EPS))
    return x * r * g


def _lane_col(x, lane, idx):
    return jnp.sum(jnp.where(lane == idx, x, 0.0), axis=-1, keepdims=True)


def _online_update(carry, s, mask, v):
    m, l, acc = carry
    s = jnp.where(mask, s, MASKED)
    m_new = jnp.maximum(m, jnp.max(s, axis=-1, keepdims=True))
    p = jnp.where(mask, jnp.exp(s - m_new), 0.0)
    alpha = jnp.exp(m - m_new)
    l = alpha * l + jnp.sum(p, axis=-1, keepdims=True)
    acc = alpha * acc + _dot(p.astype(BF16), v)
    return m_new, l, acc


def _softmax_init(rows, width):
    return (jnp.full((rows, 1), MASKED, F32), jnp.zeros((rows, 1), F32),
            jnp.zeros((rows, width), F32))


def _norm_matmul_kernel(x_ref, g_ref, w_ref, o_ref, h_ref):
    @pl.when(pl.program_id(1) == 0)
    def _():
        h_ref[...] = _rms_rows(x_ref[...], g_ref[...]).astype(BF16)

    o_ref[...] = _dot(h_ref[...], w_ref[...])


def norm_matmul(x, g, w, *, tm=512, tn=768):
    t, d = x.shape
    n = w.shape[1]
    return pl.pallas_call(
        _norm_matmul_kernel,
        grid=(t // tm, n // tn),
        in_specs=[pl.BlockSpec((tm, d), lambda i, j: (i, 0)),
                  pl.BlockSpec((1, d), lambda i, j: (0, 0)),
                  pl.BlockSpec((d, tn), lambda i, j: (0, j))],
        out_specs=pl.BlockSpec((tm, tn), lambda i, j: (i, j)),
        out_shape=jax.ShapeDtypeStruct((t, n), F32),
        scratch_shapes=[pltpu.VMEM((tm, d), BF16)],
        compiler_params=_cparams("parallel", "arbitrary"),
    )(x, g.reshape(1, d), w)


def _sb_kernel(q_ref, k_ref, v_ref, o_ref, kb_ref, vb_ref, *, tq):
    i = pl.program_id(2)

    @pl.when(i == 0)
    def _():
        kb_ref[...] = k_ref[0].astype(BF16)
        vb_ref[...] = v_ref[0].astype(BF16)

    q = q_ref[0]
    lane = lax.broadcasted_iota(jnp.int32, (tq, LANES), 1)
    lo = lane < HEAD_DIM
    rr = lax.broadcasted_iota(jnp.int32, (tq, tq), 0)
    cc = lax.broadcasted_iota(jnp.int32, (tq, tq), 1)
    upper = jnp.where(rr > cc, 1.0, 0.0).astype(BF16)
    scale = HEAD_DIM ** -0.5

    outs = []
    for h in range(2):
        qh = jnp.where(lo if h == 0 else jnp.logical_not(lo), q, 0.0).astype(BF16)

        def body(n, carry, qh=qh):
            tail, acc = carry
            kt = i - n
            off = pl.multiple_of(kt * tq, tq)
            k = kb_ref[pl.ds(off, tq), :]
            v = vb_ref[pl.ds(off, tq), :]
            z = _nt_dot(qh, k) * scale
            past = (kt * tq + cc) < (i * tq + rr)
            sp = _softplus(z)
            log_1m = jnp.where(past, -sp, 0.0)
            hi, mid, _ = _split3(log_1m)
            between = _dot(hi, upper) + _dot(mid, upper) + tail
            w = jnp.where(past, jnp.exp(z - sp + between), 0.0)
            acc = acc + _dot(w.astype(BF16), v)
            tail = tail + jnp.sum(log_1m, axis=-1, keepdims=True)
            return tail, acc

        _, acc = lax.fori_loop(0, i + 1, body,
                               (jnp.zeros((tq, 1), F32), jnp.zeros((tq, LANES), F32)))
        outs.append(acc)
    o_ref[0] = jnp.where(lo, outs[0], outs[1]).astype(o_ref.dtype)


def sb_attention(y3, *, tq=256):
    b, s, _ = y3.shape
    qb, kb, vb = COL_SB // LANES, (COL_SB + MIX_W) // LANES, (COL_SB + 2 * MIX_W) // LANES
    return pl.pallas_call(
        functools.partial(_sb_kernel, tq=tq),
        grid=(b, 2, s // tq),
        in_specs=[pl.BlockSpec((1, tq, LANES), lambda bi, hp, i: (bi, i, qb + hp)),
                  pl.BlockSpec((1, s, LANES), lambda bi, hp, i: (bi, 0, kb + hp)),
                  pl.BlockSpec((1, s, LANES), lambda bi, hp, i: (bi, 0, vb + hp))],
        out_specs=pl.BlockSpec((1, tq, LANES), lambda bi, hp, i: (bi, i, hp)),
        out_shape=jax.ShapeDtypeStruct((b, s, MIX_W), BF16),
        scratch_shapes=[pltpu.VMEM((s, LANES), BF16), pltpu.VMEM((s, LANES), BF16)],
        compiler_params=_cparams("parallel", "parallel", "arbitrary"),
    )(y3, y3, y3)


def _fox_cum_kernel(misc_ref, fb_ref, cumc_ref, cumr_ref, *, s):
    rr = lax.broadcasted_iota(jnp.int32, (LANES, LANES), 0)
    cc = lax.broadcasted_iota(jnp.int32, (LANES, LANES), 1)
    lower = jnp.where(cc <= rr, 1.0, 0.0).astype(BF16)

    def body(n, carry):
        off = pl.multiple_of(n * LANES, LANES)
        x = misc_ref[0, pl.ds(off, LANES), :] + fb_ref[...]
        log_f = jnp.minimum(x, 0.0) - jnp.log(1.0 + jnp.exp(-jnp.abs(x)))
        hi, mid, lo3 = _split3(log_f)
        c = _dot(lower, hi) + _dot(lower, mid) + _dot(lower, lo3) + carry
        cumc_ref[0, pl.ds(off, LANES), :] = c
        ct = c.T
        for h in range(N_HEADS):
            cumr_ref[0, h, :, pl.ds(off, LANES)] = ct[MISC_F + h:MISC_F + h + 1, :]
        return c[LANES - 1:LANES, :]

    lax.fori_loop(0, s // LANES, body, jnp.zeros((1, LANES), F32))


def fox_cum(y3, fbias_row):
    b, s, _ = y3.shape
    return pl.pallas_call(
        functools.partial(_fox_cum_kernel, s=s),
        grid=(b,),
        in_specs=[pl.BlockSpec((1, s, LANES), lambda bi: (bi, 0, COL_MISC // LANES)),
                  pl.BlockSpec((1, LANES), lambda bi: (0, 0))],
        out_specs=[pl.BlockSpec((1, s, LANES), lambda bi: (bi, 0, 0)),
                   pl.BlockSpec((1, N_HEADS, 1, s), lambda bi: (bi, 0, 0, 0))],
        out_shape=[jax.ShapeDtypeStruct((b, s, LANES), F32),
                   jax.ShapeDtypeStruct((b, N_HEADS, 1, s), F32)],
        compiler_params=_cparams("parallel"),
    )(y3, fbias_row)


def _fox_kernel(q_ref, k_ref, v_ref, cumc_ref, cumr_ref, gq_ref, gk_ref, o_ref,
                kb_ref, vb_ref, *, tq):
    hp = pl.program_id(1)
    i = pl.program_id(2)
    lane_s = lax.broadcasted_iota(jnp.int32, (1, LANES), 1)

    @pl.when(i == 0)
    def _():
        kb_ref[...] = _pair_rms(k_ref[0], gk_ref[...], lane_s < HEAD_DIM).astype(BF16)
        vb_ref[...] = v_ref[0].astype(BF16)

    lane = lax.broadcasted_iota(jnp.int32, (tq, LANES), 1)
    lo = lane < HEAD_DIM
    q = _pair_rms(q_ref[0], gq_ref[...], lo) * (HEAD_DIM ** -0.5)
    cumc = cumc_ref[0]
    rr = lax.broadcasted_iota(jnp.int32, (tq, tq), 0)
    cc = lax.broadcasted_iota(jnp.int32, (tq, tq), 1)

    outs = []
    for h in range(2):
        qh = jnp.where(lo if h == 0 else jnp.logical_not(lo), q, 0.0).astype(BF16)
        cq = _lane_col(cumc, lane, MISC_F + 2 * hp + h)

        def body(kt, carry, qh=qh, cq=cq, h=h):
            off = pl.multiple_of(kt * tq, tq)
            k = kb_ref[pl.ds(off, tq), :]
            v = vb_ref[pl.ds(off, tq), :]
            ck = cumr_ref[0, h, :, pl.ds(off, tq)]
            sc = _nt_dot(qh, k) + (cq - ck)
            mask = (kt * tq + cc) <= (i * tq + rr)
            return _online_update(carry, sc, mask, v)

        _, l, acc = lax.fori_loop(0, i + 1, body, _softmax_init(tq, LANES))
        outs.append(acc / l)
    o_ref[0] = jnp.where(lo, outs[0], outs[1]).astype(o_ref.dtype)


def fox_attention(y3, cumc, cumr, gq, gk, *, tq=256):
    b, s, _ = y3.shape
    qb, kb, vb = COL_FOX // LANES, (COL_FOX + MIX_W) // LANES, (COL_FOX + 2 * MIX_W) // LANES
    return pl.pallas_call(
        functools.partial(_fox_kernel, tq=tq),
        grid=(b, 2, s // tq),
        in_specs=[pl.BlockSpec((1, tq, LANES), lambda bi, hp, i: (bi, i, qb + hp)),
                  pl.BlockSpec((1, s, LANES), lambda bi, hp, i: (bi, 0, kb + hp)),
                  pl.BlockSpec((1, s, LANES), lambda bi, hp, i: (bi, 0, vb + hp)),
                  pl.BlockSpec((1, tq, LANES), lambda bi, hp, i: (bi, i, 0)),
                  pl.BlockSpec((1, 2, 1, s), lambda bi, hp, i: (bi, hp, 0, 0)),
                  pl.BlockSpec((1, LANES), lambda bi, hp, i: (0, 0)),
                  pl.BlockSpec((1, LANES), lambda bi, hp, i: (0, 0))],
        out_specs=pl.BlockSpec((1, tq, LANES), lambda bi, hp, i: (bi, i, hp)),
        out_shape=jax.ShapeDtypeStruct((b, s, MIX_W), BF16),
        scratch_shapes=[pltpu.VMEM((s, LANES), BF16), pltpu.VMEM((s, LANES), BF16)],
        compiler_params=_cparams("parallel", "parallel", "arbitrary"),
    )(y3, y3, y3, cumc, cumr, gq, gk)


def _mla_prep_kernel(cq_ref, ckv_ref, misc_ref, gcq_ref, gckv_ref, wuq_ref, wuk_ref, wuv_ref,
                     qg_ref, kg_ref, cos_ref, sa_ref, sb_ref, q_out, k_out, v_out, *, ts):
    hq = _rms_rows(cq_ref[0], gcq_ref[...]).astype(BF16)
    hkv = _rms_rows(ckv_ref[0], gckv_ref[...]).astype(BF16)
    q = _dot(hq, wuq_ref[...])
    kn = _dot(hkv, wuk_ref[...])
    v_out[0] = _dot(hkv, wuv_ref[...]).astype(v_out.dtype)

    lane = lax.broadcasted_iota(jnp.int32, (ts, LANES), 1)
    misc = misc_ref[0]
    kr = pltpu.roll(jnp.where((lane >= MISC_KR) & (lane < MISC_KR + MLA_ROPE), misc, 0.0),
                    MLA_NOPE - MISC_KR, 1)
    cos, sa, sb = cos_ref[...], sa_ref[...], sb_ref[...]
    half = MLA_ROPE // 2

    def norm_rope(t, g):
        t = t * lax.rsqrt(jnp.sum(t * t, axis=-1, keepdims=True) / MLA_QK + EPS) * g
        return t * cos + pltpu.roll(t, LANES - half, 1) * sa + pltpu.roll(t, half, 1) * sb

    for h in range(N_HEADS):
        sl = slice(h * LANES, (h + 1) * LANES)
        q_out[0, :, sl] = norm_rope(q[:, sl], qg_ref[...]).astype(q_out.dtype)
        k_out[0, :, sl] = norm_rope(kn[:, sl] + kr, kg_ref[...]).astype(k_out.dtype)


def mla_prep(y3, gcq, gckv, wuq, wuk, wuv, qg, kg, cos, sa, sb, *, ts=256):
    b, s, _ = y3.shape
    const = lambda shape: pl.BlockSpec(shape, lambda bi, i: (0,) * len(shape))
    return pl.pallas_call(
        functools.partial(_mla_prep_kernel, ts=ts),
        grid=(b, s // ts),
        in_specs=[pl.BlockSpec((1, ts, MLA_Q_LORA), lambda bi, i: (bi, i, COL_CQ // MLA_Q_LORA)),
                  pl.BlockSpec((1, ts, LANES), lambda bi, i: (bi, i, COL_CKV // LANES)),
                  pl.BlockSpec((1, ts, LANES), lambda bi, i: (bi, i, COL_MISC // LANES)),
                  const((1, MLA_Q_LORA)), const((1, MLA_KV_LORA)),
                  const((MLA_Q_LORA, N_HEADS * LANES)), const((MLA_KV_LORA, N_HEADS * LANES)),
                  const((MLA_KV_LORA, MIX_W)), const((1, LANES)), const((1, LANES)),
                  pl.BlockSpec((ts, LANES), lambda bi, i: (i, 0)),
                  pl.BlockSpec((ts, LANES), lambda bi, i: (i, 0)),
                  pl.BlockSpec((ts, LANES), lambda bi, i: (i, 0))],
        out_specs=[pl.BlockSpec((1, ts, N_HEADS * LANES), lambda bi, i: (bi, i, 0)),
                   pl.BlockSpec((1, ts, N_HEADS * LANES), lambda bi, i: (bi, i, 0)),
                   pl.BlockSpec((1, ts, MIX_W), lambda bi, i: (bi, i, 0))],
        out_shape=[jax.ShapeDtypeStruct((b, s, N_HEADS * LANES), BF16),
                   jax.ShapeDtypeStruct((b, s, N_HEADS * LANES), BF16),
                   jax.ShapeDtypeStruct((b, s, MIX_W), BF16)],
        compiler_params=_cparams("parallel", "parallel"),
    )(y3, y3, y3, gcq, gckv, wuq, wuk, wuv, qg, kg, cos, sa, sb)


def _mla_kernel(q_ref, k_ref, v_ref, o_ref, *, tq):
    i = pl.program_id(2)
    lane = lax.broadcasted_iota(jnp.int32, (tq, LANES), 1)
    lo = lane < HEAD_DIM
    rr = lax.broadcasted_iota(jnp.int32, (tq, tq), 0)
    cc = lax.broadcasted_iota(jnp.int32, (tq, tq), 1)
    scale = MLA_QK ** -0.5

    outs = []
    for h in range(2):
        qh = q_ref[0, :, h * LANES:(h + 1) * LANES]

        def body(kt, carry, qh=qh, h=h):
            off = pl.multiple_of(kt * tq, tq)
            k = k_ref[0, pl.ds(off, tq), h * LANES:(h + 1) * LANES]
            v = v_ref[0, pl.ds(off, tq), :]
            sc = _nt_dot(qh, k) * scale
            mask = (kt * tq + cc) <= (i * tq + rr)
            return _online_update(carry, sc, mask, v)

        _, l, acc = lax.fori_loop(0, i + 1, body, _softmax_init(tq, LANES))
        outs.append(acc / l)
    o_ref[0] = jnp.where(lo, outs[0], outs[1]).astype(o_ref.dtype)


def mla_attention(q, k, v, *, tq=256):
    b, s, _ = q.shape
    return pl.pallas_call(
        functools.partial(_mla_kernel, tq=tq),
        grid=(b, 2, s // tq),
        in_specs=[pl.BlockSpec((1, tq, 2 * LANES), lambda bi, hp, i: (bi, i, hp)),
                  pl.BlockSpec((1, s, 2 * LANES), lambda bi, hp, i: (bi, 0, hp)),
                  pl.BlockSpec((1, s, LANES), lambda bi, hp, i: (bi, 0, hp))],
        out_specs=pl.BlockSpec((1, tq, LANES), lambda bi, hp, i: (bi, i, hp)),
        out_shape=jax.ShapeDtypeStruct((b, s, MIX_W), BF16),
        compiler_params=_cparams("parallel", "parallel", "arbitrary"),
    )(q, k, v)


def _rel_bias_tile(dist, tab_ref, h):
    max_exact = REL_BUCKETS // 2
    d = jnp.maximum(dist, 0)
    large = max_exact + (jnp.log(jnp.maximum(d, 1).astype(F32) / max_exact)
                         / math.log(REL_MAX_DIST / max_exact)
                         * (REL_BUCKETS - max_exact)).astype(jnp.int32)
    large = jnp.minimum(large, REL_BUCKETS - 1)
    bucket = jnp.where(d < max_exact, d, large)
    out = jnp.zeros(dist.shape, F32)
    for bkt in range(REL_BUCKETS):
        out = jnp.where(bucket == bkt, tab_ref[bkt, h], out)
    return out


def _band_bias_kernel(tab_ref, o_ref):
    r = pl.program_id(0)
    qi = lax.broadcasted_iota(jnp.int32, (LANES, LANES), 0)
    kj = lax.broadcasted_iota(jnp.int32, (LANES, LANES), 1)
    dist = qi + LANES * r - kj
    for h in range(N_HEADS):
        o_ref[0, h * LANES:(h + 1) * LANES, :] = _rel_bias_tile(dist, tab_ref, h)


def _cmp_bias_kernel(tab_ref, o_ref):
    i = pl.program_id(0)
    s = i * LANES + lax.broadcasted_iota(jnp.int32, (LANES, LANES), 0)
    c = lax.broadcasted_iota(jnp.int32, (LANES, LANES), 1)
    dist = s - (c * NSA_CMP_STRIDE + NSA_CMP_LEN - 1)
    for h in range(N_HEADS):
        o_ref[h] = _rel_bias_tile(dist, tab_ref, h)


def rel_bias_tables(rel_bias, s):
    n_band = NSA_WINDOW // LANES + 1
    smem = pl.BlockSpec(memory_space=pltpu.SMEM)
    band = pl.pallas_call(
        _band_bias_kernel, grid=(n_band,), in_specs=[smem],
        out_specs=pl.BlockSpec((1, N_HEADS * LANES, LANES), lambda r: (r, 0, 0)),
        out_shape=jax.ShapeDtypeStruct((n_band, N_HEADS * LANES, LANES), F32),
    )(rel_bias)
    cmpb = pl.pallas_call(
        _cmp_bias_kernel, grid=(s // LANES,), in_specs=[smem],
        out_specs=pl.BlockSpec((N_HEADS, LANES, LANES), lambda i: (0, i, 0)),
        out_shape=jax.ShapeDtypeStruct((N_HEADS, s, LANES), F32),
    )(rel_bias)
    return band, cmpb


def _nsa_compress_kernel(kin_ref, vin_ref, pek_ref, pev_ref, w1k_ref, w1v_ref, w2k_ref, w2v_ref,
                         gk_ref, o_ref):
    half = NSA_CMP_STRIDE * HEAD_DIM

    def compress(x, pe_ref, w1_ref, w2_ref):
        a = _dot((x + pe_ref[:, :half]).astype(BF16), w1_ref[:half, :])
        bh = _dot((x + pe_ref[:, half:]).astype(BF16), w1_ref[half:, :])
        pre = a + pltpu.roll(bh, LANES - 1, 0)
        hid = pre * _sigmoid(pre)
        return _dot(hid.astype(BF16), w2_ref[...])

    kc = compress(kin_ref[0], pek_ref, w1k_ref, w2k_ref)
    vc = compress(vin_ref[0], pev_ref, w1v_ref, w2v_ref)
    kc = kc * lax.rsqrt(jnp.sum(kc * kc, axis=-1, keepdims=True) / HEAD_DIM + EPS) * gk_ref[...]
    o_ref[0] = (kc + vc).astype(o_ref.dtype)


def nsa_compress(kin, vin, pek, pev, w1k, w1v, w2k, w2v, gk):
    b = kin.shape[0]
    const = lambda shape: pl.BlockSpec(shape, lambda bi: (0,) * len(shape))
    blk = pl.BlockSpec((1,) + kin.shape[1:], lambda bi: (bi, 0, 0))
    return pl.pallas_call(
        _nsa_compress_kernel, grid=(b,),
        in_specs=[blk, blk, const(pek.shape), const(pev.shape), const(w1k.shape), const(w1v.shape),
                  const(w2k.shape), const(w2v.shape), const(gk.shape)],
        out_specs=pl.BlockSpec((1, LANES, LANES), lambda bi: (bi, 0, 0)),
        out_shape=jax.ShapeDtypeStruct((b, LANES, LANES), BF16),
        compiler_params=_cparams("parallel"),
    )(kin, vin, pek, pev, w1k, w1v, w2k, w2v, gk)


def _nsa_kernel(q_ref, k2_ref, v2_ref, misc_ref, kcv_ref, band_ref, cmpb_ref, gq_ref, gk2_ref,
                o_ref, k2b_ref, v2b_ref):
    tq = LANES
    rows = N_HEADS * tq
    n_band = NSA_WINDOW // LANES
    i = pl.program_id(1)
    lane_s = lax.broadcasted_iota(jnp.int32, (1, LANES), 1)

    @pl.when(i == 0)
    def _():
        k2b_ref[...] = _pair_rms(k2_ref[0], gk2_ref[...], lane_s < HEAD_DIM).astype(BF16)
        v2b_ref[...] = v2_ref[0].astype(BF16)

    lane = lax.broadcasted_iota(jnp.int32, (tq, LANES), 1)
    lo = lane < HEAD_DIM
    scale = HEAD_DIM ** -0.5
    q = q_ref[0]
    qa = _pair_rms(q[:, :LANES], gq_ref[:, :LANES], lo) * scale
    qb = _pair_rms(q[:, LANES:], gq_ref[:, LANES:], lo) * scale
    slc_parts, win_parts = [], []
    for pair in (qa, qb):
        h_even = jnp.where(lo, pair, 0.0)
        h_odd = jnp.where(lo, 0.0, pair)
        slc_parts += [h_even, pltpu.roll(h_odd, HEAD_DIM, 1)]
        win_parts += [pltpu.roll(h_even, HEAD_DIM, 1), h_odd]
    q_lo = jnp.concatenate(slc_parts, axis=0).astype(BF16)
    q_hi = jnp.concatenate(win_parts, axis=0).astype(BF16)

    row = lax.broadcasted_iota(jnp.int32, (rows, LANES), 0)
    lane_r = lax.broadcasted_iota(jnp.int32, (rows, LANES), 1)
    tpos = i * tq + (row & (tq - 1))

    kcv = kcv_ref[0]
    s_c = _nt_dot(q_lo, kcv) + cmpb_ref[...].reshape(rows, LANES)
    valid = (lane_r * NSA_CMP_STRIDE + NSA_CMP_LEN - 1) <= tpos
    s_c = jnp.where(valid, s_c, MASKED)
    e_c = jnp.where(valid, jnp.exp(s_c - jnp.max(s_c, axis=-1, keepdims=True)), 0.0)
    den = jnp.sum(e_c, axis=-1, keepdims=True)
    p_c = e_c / jnp.where(den > 0.0, den, 1.0)
    o_cmp = _dot(p_c.astype(BF16), kcv)

    p_sum = p_c[0:tq] + p_c[tq:2 * tq] + p_c[2 * tq:3 * tq] + p_c[3 * tq:4 * tq]
    jj = lax.broadcasted_iota(jnp.int32, (LANES, LANES), 0)
    c0 = lax.broadcasted_iota(jnp.int32, (LANES, LANES), 1) * NSA_CMP_STRIDE
    j0 = jj * NSA_SEL_LEN
    overlap_t = jnp.where((c0 < j0 + NSA_SEL_LEN) & (c0 + NSA_CMP_LEN > j0), 1.0, 0.0).astype(BF16)
    hi, mid, lo3 = _split3(p_sum)
    imp_t = _nt_dot(overlap_t, hi) + _nt_dot(overlap_t, mid) + _nt_dot(overlap_t, lo3)
    n_sel = k2_ref.shape[1] // NSA_SEL_LEN
    imp = imp_t[0:n_sel]
    jj32 = lax.broadcasted_iota(jnp.int32, (n_sel, tq), 0)
    cur = (i * tq + lax.broadcasted_iota(jnp.int32, (n_sel, tq), 1)) >> 6
    forced = (jj32 == 0) | (jj32 == cur) | (jj32 == cur - 1)
    imp = jnp.where(forced, FORCE, imp)
    imp = jnp.where(jj32 <= cur, imp, -FORCE)
    cnt = jnp.zeros((n_sel, tq), F32)
    for jp in range(n_sel):
        other = imp[jp:jp + 1, :]
        beats = (other > imp) | ((other == imp) & (jj32 > jp))
        cnt = cnt + jnp.where(beats, 1.0, 0.0)
    sel_t = jnp.where(cnt < float(NSA_TOP_N), 1.0, 0.0)
    sel_t = jnp.concatenate([sel_t, jnp.zeros((LANES - n_sel, tq), F32)], axis=0)
    sel = sel_t.T.astype(BF16)

    kk = lax.broadcasted_iota(jnp.int32, (LANES, LANES), 1)

    def tile(kt, carry, with_win):
        off = pl.multiple_of(kt * LANES, LANES)
        k = k2b_ref[pl.ds(off, LANES), :]
        v = v2b_ref[pl.ds(off, LANES), :]
        bias = band_ref[jnp.minimum(i - kt, n_band)]
        kpos = kt * LANES + lane_r
        expand = jnp.where(jj == 2 * kt + (kk >> 6), 1.0, 0.0).astype(BF16)
        key_sel = _dot(sel, expand)
        key_sel = jnp.concatenate([key_sel] * N_HEADS, axis=0)
        c_slc, c_win = carry
        s1 = _nt_dot(q_lo, k) + bias
        c_slc = _online_update(c_slc, s1, (key_sel > 0.5) & (kpos <= tpos), v)
        if with_win:
            s2 = _nt_dot(q_hi, k) + bias
            wd = tpos - kpos
            c_win = _online_update(c_win, s2, (wd >= 0) & (wd < NSA_WINDOW), v)
        return c_slc, c_win

    first_win = jnp.maximum(i - n_band, 0)
    carry = (_softmax_init(rows, LANES), _softmax_init(rows, LANES))
    carry = lax.fori_loop(0, first_win, lambda kt, c: tile(kt, c, False), carry)
    carry = lax.fori_loop(first_win, i + 1, lambda kt, c: tile(kt, c, True), carry)
    (_, l1, a1), (_, l2, a2) = carry
    o_slc = pltpu.roll(a1 / l1, HEAD_DIM, 1)
    o_win = a2 / l2

    g = _sigmoid(misc_ref[0])
    heads = []
    for h in range(N_HEADS):
        sl = slice(h * tq, (h + 1) * tq)
        g_cmp = _lane_col(g, lane, MISC_G + h)
        g_slc = _lane_col(g, lane, MISC_G + N_HEADS + h)
        g_win = _lane_col(g, lane, MISC_G + 2 * N_HEADS + h)
        heads.append(g_cmp * o_cmp[sl] + g_slc * o_slc[sl] + g_win * o_win[sl])
    for pair in range(2):
        packed = jnp.where(lo, pltpu.roll(heads[2 * pair], HEAD_DIM, 1), heads[2 * pair + 1])
        o_ref[0, :, pair * LANES:(pair + 1) * LANES] = packed.astype(o_ref.dtype)


def nsa_attention(y3, kcv, band, cmpb, gq, gk2):
    b, s, _ = y3.shape
    tq = LANES
    return pl.pallas_call(
        _nsa_kernel,
        grid=(b, s // tq),
        in_specs=[pl.BlockSpec((1, tq, MIX_W), lambda bi, i: (bi, i, COL_NSAQ // MIX_W)),
                  pl.BlockSpec((1, s, LANES), lambda bi, i: (bi, 0, COL_K2 // LANES)),
                  pl.BlockSpec((1, s, LANES), lambda bi, i: (bi, 0, COL_V2 // LANES)),
                  pl.BlockSpec((1, tq, LANES), lambda bi, i: (bi, i, COL_MISC // LANES)),
                  pl.BlockSpec((1, LANES, LANES), lambda bi, i: (bi, 0, 0)),
                  pl.BlockSpec(band.shape, lambda bi, i: (0, 0, 0)),
                  pl.BlockSpec((N_HEADS, tq, LANES), lambda bi, i: (0, i, 0)),
                  pl.BlockSpec((1, MIX_W), lambda bi, i: (0, 0)),
                  pl.BlockSpec((1, LANES), lambda bi, i: (0, 0))],
        out_specs=pl.BlockSpec((1, tq, MIX_W), lambda bi, i: (bi, i, 0)),
        out_shape=jax.ShapeDtypeStruct((b, s, MIX_W), BF16),
        scratch_shapes=[pltpu.VMEM((s, LANES), BF16), pltpu.VMEM((s, LANES), BF16)],
        compiler_params=_cparams("parallel", "arbitrary"),
    )(y3, y3, y3, y3, kcv, band, cmpb, gq, gk2)


def _merge_kernel(ysb_ref, ymla_ref, ynsa_ref, yfox_ref, g0_ref, g1_ref, g2_ref, g3_ref,
                  wb_ref, wo_ref, x_ref, o_ref):
    u = None
    for n, (y_ref, g_ref) in enumerate(((ysb_ref, g0_ref), (ymla_ref, g1_ref),
                                        (ynsa_ref, g2_ref), (yfox_ref, g3_ref))):
        term = _sigmoid(g_ref[...]) * _dot(y_ref[...], wb_ref[n])
        u = term if u is None else u + term
    o_ref[...] = x_ref[...] + _dot(u.astype(BF16), wo_ref[...])


def merge_branches(ys, y, wb, wo, x, *, tm=256):
    t, d = x.shape
    gate_blk = COL_GATE // d
    yspec = pl.BlockSpec((tm, MIX_W), lambda i: (i, 0))
    gspecs = [pl.BlockSpec((tm, d), lambda i, n=n: (i, gate_blk + n)) for n in range(N_BRANCH)]
    return pl.pallas_call(
        _merge_kernel, grid=(t // tm,),
        in_specs=[yspec] * 4 + gspecs + [pl.BlockSpec(wb.shape, lambda i: (0, 0, 0)),
                                         pl.BlockSpec(wo.shape, lambda i: (0, 0)),
                                         pl.BlockSpec((tm, d), lambda i: (i, 0))],
        out_specs=pl.BlockSpec((tm, d), lambda i: (i, 0)),
        out_shape=jax.ShapeDtypeStruct((t, d), F32),
        compiler_params=_cparams("parallel"),
    )(*ys, y, y, y, y, wb, wo, x)


def _mlp_kernel(x_ref, g_ref, wu_ref, wd_ref, o_ref, h_ref, acc_ref):
    f = pl.program_id(1)

    @pl.when(f == 0)
    def _():
        h_ref[...] = _rms_rows(x_ref[...], g_ref[...]).astype(BF16)
        acc_ref[...] = jnp.zeros_like(acc_ref)

    a = jnp.maximum(_dot(h_ref[...], wu_ref[...]), 0.0)
    acc_ref[...] += _dot((a * a).astype(BF16), wd_ref[...])

    @pl.when(f == pl.num_programs(1) - 1)
    def _():
        o_ref[...] = x_ref[...] + acc_ref[...]


def mlp(x, g, wu, wd, *, tm=512, tf=512):
    t, d = x.shape
    ff = wu.shape[1]
    return pl.pallas_call(
        _mlp_kernel, grid=(t // tm, ff // tf),
        in_specs=[pl.BlockSpec((tm, d), lambda i, f: (i, 0)),
                  pl.BlockSpec((1, d), lambda i, f: (0, 0)),
                  pl.BlockSpec((d, tf), lambda i, f: (0, f)),
                  pl.BlockSpec((tf, d), lambda i, f: (f, 0))],
        out_specs=pl.BlockSpec((tm, d), lambda i, f: (i, 0)),
        out_shape=jax.ShapeDtypeStruct((t, d), F32),
        scratch_shapes=[pltpu.VMEM((tm, d), BF16), pltpu.VMEM((tm, d), F32)],
        compiler_params=_cparams("parallel", "arbitrary"),
    )(x, g.reshape(1, d), wu, wd)


def _ple_kernel(x_ref, g_ref, wg_ref, p_ref, wp_ref, o_ref):
    x = x_ref[...]
    gate = _sigmoid(_dot(_rms_rows(x, g_ref[...]).astype(BF16), wg_ref[...]))
    o_ref[...] = x + gate * _dot(p_ref[...].astype(BF16), wp_ref[...])


def ple(x, g, wg, p_all, layer, wp, *, tm=512):
    t, d = x.shape
    return pl.pallas_call(
        _ple_kernel, grid=(t // tm,),
        in_specs=[pl.BlockSpec((tm, d), lambda i: (i, 0)),
                  pl.BlockSpec((1, d), lambda i: (0, 0)),
                  pl.BlockSpec((d, d), lambda i: (0, 0)),
                  pl.BlockSpec((None, tm, PLE_DIM), lambda i: (layer, i, 0)),
                  pl.BlockSpec((PLE_DIM, d), lambda i: (0, 0))],
        out_specs=pl.BlockSpec((tm, d), lambda i: (i, 0)),
        out_shape=jax.ShapeDtypeStruct((t, d), F32),
        compiler_params=_cparams("parallel"),
    )(x, g.reshape(1, d), wg, p_all, wp)


def _pack_w_in(w):
    offs = np.concatenate([[0], np.cumsum(IN_WIDTHS)]).tolist()
    (sb_q, sb_k, sb_v, cq, ckv, kr, nsa_q, kc, vc, ks, vs, kw, vw, ng,
     fox_q, fox_k, fox_v, ff, gate) = [w[:, offs[n]:offs[n + 1]] for n in range(len(IN_WIDTHS))]
    d = w.shape[0]
    z = lambda n: jnp.zeros((d, n), w.dtype)
    misc = jnp.concatenate([ff, ng, z(MISC_KR - MISC_G - 3 * N_HEADS), kr,
                            z(LANES - MISC_KR - MLA_ROPE)], axis=1)
    packed = jnp.concatenate([cq, ckv, ks, kw, vs, vw, kc, vc, misc, gate, nsa_q,
                              sb_q, sb_k, sb_v, fox_q, fox_k, fox_v], axis=1)
    return packed.astype(BF16)


def _head_slots(w, width):
    k = w.shape[0]
    w = w.reshape(k, N_HEADS, width)
    return jnp.pad(w, ((0, 0), (0, 0), (0, LANES - width))).reshape(k, N_HEADS * LANES)


def _rope_tables(s):
    half = MLA_ROPE // 2
    inv = jnp.exp(-math.log(ROPE_THETA) * jnp.arange(half, dtype=F32) / half)
    ang = jnp.arange(s, dtype=F32)[:, None] * inv[None, :]
    cos, sin = jnp.cos(ang), jnp.sin(ang)
    ones = jnp.ones((s, MLA_NOPE), F32)
    zeros = lambda n: jnp.zeros((s, n), F32)
    tail = LANES - MLA_QK
    cos_t = jnp.concatenate([ones, cos, cos, jnp.ones((s, tail), F32)], axis=1)
    sa_t = jnp.concatenate([zeros(MLA_NOPE), -sin, zeros(half), zeros(tail)], axis=1)
    sb_t = jnp.concatenate([zeros(MLA_NOPE), zeros(half), sin, zeros(tail)], axis=1)
    return cos_t, sa_t, sb_t


def _pad_lanes(v, left=0):
    v = v.reshape(1, -1)
    return jnp.pad(v, ((0, 0), (left, LANES - left - v.shape[1])))


def kernel(x, p, rel_bias, norm_mix_g, w_in, mla_cq_norm_g, mla_ckv_norm_g, mla_w_uq, mla_w_ukv,
           mla_qn_g, mla_kn_g, nsa_pe_k, nsa_pe_v, nsa_w1_k, nsa_w2_k, nsa_w1_v, nsa_w2_v,
           nsa_qn_g, nsa_kn_g, fox_f_bias, fox_qn_g, fox_kn_g, w_branch, w_o, norm_mlp_g,
           w_mlp_up, w_mlp_down, norm_ple_g, w_ple_gate, w_ple_proj):
    b, s, d = x.shape
    t = b * s
    xf = x.reshape(t, d)
    p_all = p.reshape(DEPTH, t, PLE_DIM)
    band, cmpb = rel_bias_tables(rel_bias.astype(F32), s)
    cos_t, sa_t, sb_t = _rope_tables(s)
    n_cmp_in = NSA_CMP_STRIDE * HEAD_DIM

    for i in range(DEPTH):
        y = norm_matmul(xf, norm_mix_g[i], _pack_w_in(w_in[i]))
        y3 = y.reshape(b, s, N_PACKED)

        y_sb = sb_attention(y3)

        wukv = mla_w_ukv[i].reshape(MLA_KV_LORA, N_HEADS, MLA_NOPE + MLA_V)
        q_m, k_m, v_m = mla_prep(
            y3, mla_cq_norm_g[i].reshape(1, -1), mla_ckv_norm_g[i].reshape(1, -1),
            _head_slots(mla_w_uq[i], MLA_QK).astype(BF16),
            _head_slots(wukv[:, :, :MLA_NOPE].reshape(MLA_KV_LORA, -1), MLA_NOPE).astype(BF16),
            wukv[:, :, MLA_NOPE:].reshape(MLA_KV_LORA, -1).astype(BF16),
            _pad_lanes(mla_qn_g[i]), _pad_lanes(mla_kn_g[i]), cos_t, sa_t, sb_t)
        y_mla = mla_attention(q_m, k_m, v_m)

        kin = y3[:, :, COL_KCVC:COL_KCVC + HEAD_DIM].reshape(b, s // NSA_CMP_STRIDE, n_cmp_in)
        vin = y3[:, :, COL_KCVC + HEAD_DIM:COL_KCVC + 2 * HEAD_DIM].reshape(
            b, s // NSA_CMP_STRIDE, n_cmp_in)
        pad_cols = lambda w, left: jnp.pad(w, ((0, 0), (left, LANES - left - w.shape[1])))
        kcv = nsa_compress(
            kin, vin, nsa_pe_k[i].reshape(1, -1), nsa_pe_v[i].reshape(1, -1),
            pad_cols(nsa_w1_k[i], 0).astype(BF16), pad_cols(nsa_w1_v[i], 0).astype(BF16),
            jnp.pad(pad_cols(nsa_w2_k[i], 0), ((0, LANES - HEAD_DIM), (0, 0))).astype(BF16),
            jnp.pad(pad_cols(nsa_w2_v[i], HEAD_DIM), ((0, LANES - HEAD_DIM), (0, 0))).astype(BF16),
            _pad_lanes(nsa_kn_g[i, 0]))
        y_nsa = nsa_attention(
            y3, kcv, band, cmpb, jnp.tile(nsa_qn_g[i], N_HEADS).reshape(1, -1),
            jnp.concatenate([nsa_kn_g[i, 1], nsa_kn_g[i, 2]]).reshape(1, -1))

        cumc, cumr = fox_cum(y3, _pad_lanes(fox_f_bias[i], MISC_F))
        y_fox = fox_attention(y3, cumc, cumr, jnp.tile(fox_qn_g[i], 2).reshape(1, -1),
                              jnp.tile(fox_kn_g[i], 2).reshape(1, -1))

        ys = [a.reshape(t, MIX_W) for a in (y_sb, y_mla, y_nsa, y_fox)]
        xf = merge_branches(ys, y, w_branch[i].astype(BF16), w_o[i].astype(BF16), xf)
        xf = mlp(xf, norm_mlp_g[i], w_mlp_up[i].astype(BF16), w_mlp_down[i].astype(BF16))
        xf = ple(xf, norm_ple_g[i], w_ple_gate[i].astype(BF16), p_all, i,
                 w_ple_proj[i].astype(BF16))
    return xf.reshape(b, s, d)
```

```python
import functools
import math

import numpy as np
import jax
import jax.numpy as jnp
from jax import lax
from jax.experimental import pallas as pl
from jax.experimental.pallas import tpu as pltpu

F32 = jnp.float32
BF16 = jnp.bfloat16

D_MODEL = 1024
DEPTH = 4
HEAD_DIM = 64
N_HEADS = 4
MIX_W = N_HEADS * HEAD_DIM
N_BRANCH = 4
EPS = 1e-6
FORCE = 1e9
MLA_Q_LORA = 384
MLA_KV_LORA = 128
MLA_NOPE = 64
MLA_ROPE = 32
MLA_V = 64
MLA_QK = MLA_NOPE + MLA_ROPE
ROPE_THETA = 10000.0
NSA_CMP_LEN = 32
NSA_CMP_STRIDE = 16
NSA_SEL_LEN = 64
NSA_TOP_N = 16
NSA_WINDOW = 512
REL_BUCKETS = 32
REL_MAX_DIST = 128
D_FF = 4 * D_MODEL
PLE_DIM = 256

LANES = 128
MASKED = -1e30
VMEM_LIMIT = 56 * 1024 * 1024

IN_WIDTHS = ((MIX_W,) * 3
             + (MLA_Q_LORA, MLA_KV_LORA, MLA_ROPE)
             + (MIX_W,) + (HEAD_DIM,) * 6 + (3 * N_HEADS,)
             + (MIX_W,) * 3 + (N_HEADS,)
             + (N_BRANCH * D_MODEL,))

COL_CQ = 0
COL_CKV = 384
COL_K2 = 512
COL_V2 = 640
COL_KCVC = 768
COL_MISC = 896
COL_GATE = 1024
COL_NSAQ = 5120
COL_SB = 5376
COL_FOX = 6144
N_PACKED = 6912
MISC_F = 0
MISC_G = 4
MISC_KR = 32


def _cparams(*sem):
    return pltpu.CompilerParams(dimension_semantics=sem, vmem_limit_bytes=VMEM_LIMIT)


def _nt_dot(a, b):
    return lax.dot_general(a, b, (((1,), (1,)), ((), ())), preferred_element_type=F32)


def _dot(a, b):
    return jnp.dot(a, b, preferred_element_type=F32)


def _split3(x):
    hi = x.astype(BF16)
    r1 = x - hi.astype(F32)
    mid = r1.astype(BF16)
    lo = (r1 - mid.astype(F32)).astype(BF16)
    return hi, mid, lo


def _softplus(z):
    return jnp.maximum(z, 0.0) + jnp.log(1.0 + jnp.exp(-jnp.abs(z)))


def _sigmoid(z):
    return 1.0 / (1.0 + jnp.exp(-z))


def _rms_rows(x, g):
    r = lax.rsqrt(jnp.mean(x * x, axis=-1, keepdims=True) + EPS)
    return x * r * g


def _pair_rms(x, g, lo):
    x2 = x * x
    s0 = jnp.sum(jnp.where(lo, x2, 0.0), axis=-1, keepdims=True)
    s1 = jnp.sum(jnp.where(lo, 0.0, x2), axis=-1, keepdims=True)
    r = jnp.where(lo, lax.rsqrt(s0 / HEAD_DIM + EPS), lax.rsqrt(s1 / HEAD_DIM + EPS))
    return x * r * g


def _lane_col(x, lane, idx):
    return jnp.sum(jnp.where(lane == idx, x, 0.0), axis=-1, keepdims=True)


def _online_update(carry, s, mask, v):
    m, l, acc = carry
    s = jnp.where(mask, s, MASKED)
    m_new = jnp.maximum(m, jnp.max(s, axis=-1, keepdims=True))
    p = jnp.where(mask, jnp.exp(s - m_new), 0.0)
    alpha = jnp.exp(m - m_new)
    l = alpha * l + jnp.sum(p, axis=-1, keepdims=True)
    acc = alpha * acc + _dot(p.astype(BF16), v)
    return m_new, l, acc


def _softmax_init(rows, width):
    return (jnp.full((rows, 1), MASKED, F32), jnp.zeros((rows, 1), F32),
            jnp.zeros((rows, width), F32))


def _update_t(carry, s, vt, mask):
    m, l, acc = carry
    if mask is not None:
        s = jnp.where(mask, s, MASKED)
    m_new = jnp.maximum(m, jnp.max(s, axis=0, keepdims=True))
    p = jnp.exp(s - m_new)
    if mask is not None:
        p = jnp.where(mask, p, 0.0)
    alpha = jnp.exp(m - m_new)
    l = alpha * l + jnp.sum(p, axis=0, keepdims=True)
    acc = alpha * acc + _dot(vt, p.astype(BF16))
    return m_new, l, acc


def _init_t(tq):
    return (jnp.full((1, tq), MASKED, F32), jnp.zeros((1, tq), F32),
            jnp.zeros((HEAD_DIM, tq), F32))


def _causal_flash_t(i, tq, tk, score_fn, vt_fn):
    assert tk % tq == 0
    diff = (lax.broadcasted_iota(jnp.int32, (tk, tq), 0)
            - lax.broadcasted_iota(jnp.int32, (tk, tq), 1))
    n_full = (i * tq) // tk

    def scores_at(kt):
        off = pl.multiple_of(kt * tk, tk)
        return tuple(score_fn(h, off) for h in range(2))

    def body(kt, carry):
        cur, state = carry
        nxt = scores_at(kt + 1)
        off = pl.multiple_of(kt * tk, tk)
        state = tuple(_update_t(state[h], cur[h], vt_fn(h, off), None) for h in range(2))
        return nxt, state

    carry = (scores_at(0), (_init_t(tq), _init_t(tq)))
    cur, state = lax.fori_loop(0, n_full, body, carry)
    off = pl.multiple_of(n_full * tk, tk)
    mask = diff <= i * tq - off
    state = tuple(_update_t(state[h], cur[h], vt_fn(h, off), mask) for h in range(2))
    return jnp.concatenate([acc / l for (_, l, acc) in state], axis=0)


def _augment(x, col, lane, h, key_side):
    live = (lane < HEAD_DIM) if h == 0 else (lane >= HEAD_DIM)
    a0 = HEAD_DIM if h == 0 else 0
    hi, mid, lo3 = (t.astype(F32) for t in _split3(col))
    c0, o0 = (a0, a0 + 3) if key_side else (a0 + 3, a0)
    aug = jnp.where(lane == c0, hi,
                    jnp.where(lane == c0 + 1, mid,
                              jnp.where(lane == c0 + 2, lo3,
                                        jnp.where((lane >= o0) & (lane < o0 + 3), 1.0, 0.0))))
    return jnp.where(live, x, aug)


def _norm_matmul_kernel(x_ref, g_ref, w_ref, o_ref, h_ref):
    @pl.when(pl.program_id(1) == 0)
    def _():
        h_ref[...] = _rms_rows(x_ref[...], g_ref[...]).astype(BF16)

    o_ref[...] = _dot(h_ref[...], w_ref[...])


def norm_matmul(x, g, w, *, tm=512, tn=768):
    t, d = x.shape
    n = w.shape[1]
    return pl.pallas_call(
        _norm_matmul_kernel,
        grid=(t // tm, n // tn),
        in_specs=[pl.BlockSpec((tm, d), lambda i, j: (i, 0)),
                  pl.BlockSpec((1, d), lambda i, j: (0, 0)),
                  pl.BlockSpec((d, tn), lambda i, j: (0, j))],
        out_specs=pl.BlockSpec((tm, tn), lambda i, j: (i, j)),
        out_shape=jax.ShapeDtypeStruct((t, n), F32),
        scratch_shapes=[pltpu.VMEM((tm, d), BF16)],
        compiler_params=_cparams("parallel", "arbitrary"),
    )(x, g.reshape(1, d), w)


def _sb_kernel(q_ref, k_ref, v_ref, o_ref, kb_ref, vt_ref, *, tq, tk):
    i = pl.program_id(2)

    @pl.when(i == 0)
    def _():
        kb_ref[...] = k_ref[0].astype(BF16)
        vt_ref[...] = v_ref[0].T.astype(BF16)

    sub = LANES
    n_sub = tk // sub
    lane = lax.broadcasted_iota(jnp.int32, (tq, LANES), 1)
    lo = lane < HEAD_DIM
    q = q_ref[0] * (HEAD_DIM ** -0.5)
    qh = (jnp.where(lo, q, 0.0).astype(BF16), jnp.where(lo, 0.0, q).astype(BF16))
    rr = lax.broadcasted_iota(jnp.int32, (sub, sub), 0)
    cc = lax.broadcasted_iota(jnp.int32, (sub, sub), 1)
    upper = jnp.where(cc > rr, 1.0, 0.0).astype(BF16)
    upper2 = jnp.concatenate([upper, upper], axis=1)
    diff = (lax.broadcasted_iota(jnp.int32, (tk, tq), 0)
            - lax.broadcasted_iota(jnp.int32, (tk, tq), 1))

    def logits_at(kt):
        k = kb_ref[pl.ds(pl.multiple_of(kt * tk, tk), tk), :]
        return tuple(_nt_dot(k, qh[h]) for h in range(2))

    def tile(kt, zs, carry, masked):
        off = pl.multiple_of(kt * tk, tk)
        past = (diff < i * tq - off) if masked else None
        out = []
        for h in range(2):
            run, acc = carry[h]
            z = zs[h]
            sp = _softplus(z)
            logit = z - sp
            spm = jnp.where(past, sp, 0.0) if masked else sp
            ws = [None] * n_sub
            for j in reversed(range(n_sub)):
                sl = slice(j * sub, (j + 1) * sub)
                spj = spm[sl]
                hi = spj.astype(BF16)
                mid = (spj - hi.astype(F32)).astype(BF16)
                later = _dot(upper2, jnp.concatenate([hi, mid], axis=0))
                w = jnp.exp(logit[sl] - later - run)
                if masked:
                    w = jnp.where(past[sl], w, 0.0)
                ws[j] = w.astype(BF16)
                run = run + jnp.sum(spj, axis=0, keepdims=True)
            vt = vt_ref[h * HEAD_DIM:(h + 1) * HEAD_DIM, pl.ds(off, tk)]
            acc = acc + _dot(vt, jnp.concatenate(ws, axis=0))
            out.append((run, acc))
        return tuple(out)

    n_full = (i * tq) // tk
    zero = (jnp.zeros((1, tq), F32), jnp.zeros((HEAD_DIM, tq), F32))
    zs_diag = logits_at(n_full)
    zs_next = logits_at(jnp.maximum(n_full - 1, 0))
    carry = tile(n_full, zs_diag, (zero, zero), True)

    def body(n, c):
        zs, carry = c
        kt = n_full - 1 - n
        zs_after = logits_at(jnp.maximum(kt - 1, 0))
        return zs_after, tile(kt, zs, carry, False)

    _, carry = lax.fori_loop(0, n_full, body, (zs_next, carry))
    o_t = jnp.concatenate([carry[0][1], carry[1][1]], axis=0)
    o_ref[0] = o_t.T.astype(o_ref.dtype)


def sb_attention(y3, *, tq=256, tk=512):
    b, s, _ = y3.shape
    qb, kb, vb = COL_SB // LANES, (COL_SB + MIX_W) // LANES, (COL_SB + 2 * MIX_W) // LANES
    return pl.pallas_call(
        functools.partial(_sb_kernel, tq=tq, tk=tk),
        grid=(b, 2, s // tq),
        in_specs=[pl.BlockSpec((1, tq, LANES), lambda bi, hp, i: (bi, i, qb + hp)),
                  pl.BlockSpec((1, s, LANES), lambda bi, hp, i: (bi, 0, kb + hp)),
                  pl.BlockSpec((1, s, LANES), lambda bi, hp, i: (bi, 0, vb + hp))],
        out_specs=pl.BlockSpec((1, tq, LANES), lambda bi, hp, i: (bi, i, hp)),
        out_shape=jax.ShapeDtypeStruct((b, s, MIX_W), BF16),
        scratch_shapes=[pltpu.VMEM((s, LANES), BF16), pltpu.VMEM((LANES, s), BF16)],
        compiler_params=_cparams("parallel", "parallel", "arbitrary"),
    )(y3, y3, y3)


def _fox_cum_kernel(misc_ref, fb_ref, cum_ref, *, s):
    rr = lax.broadcasted_iota(jnp.int32, (LANES, LANES), 0)
    cc = lax.broadcasted_iota(jnp.int32, (LANES, LANES), 1)
    lower = jnp.where(cc <= rr, 1.0, 0.0).astype(BF16)

    def body(n, carry):
        off = pl.multiple_of(n * LANES, LANES)
        x = misc_ref[0, pl.ds(off, LANES), :] + fb_ref[...]
        log_f = jnp.minimum(x, 0.0) - jnp.log(1.0 + jnp.exp(-jnp.abs(x)))
        hi, mid, lo3 = _split3(log_f)
        c = _dot(lower, hi) + _dot(lower, mid) + _dot(lower, lo3) + carry
        cum_ref[0, pl.ds(off, LANES), :] = c
        return c[LANES - 1:LANES, :]

    lax.fori_loop(0, s // LANES, body, jnp.zeros((1, LANES), F32))


def fox_cum(y3, fbias_row):
    b, s, _ = y3.shape
    return pl.pallas_call(
        functools.partial(_fox_cum_kernel, s=s),
        grid=(b,),
        in_specs=[pl.BlockSpec((1, s, LANES), lambda bi: (bi, 0, COL_MISC // LANES)),
                  pl.BlockSpec((1, LANES), lambda bi: (0, 0))],
        out_specs=pl.BlockSpec((1, s, LANES), lambda bi: (bi, 0, 0)),
        out_shape=jax.ShapeDtypeStruct((b, s, LANES), F32),
        compiler_params=_cparams("parallel"),
    )(y3, fbias_row)


def _fox_kernel(q_ref, k_ref, v_ref, cum_ref, gq_ref, gk_ref, o_ref, ka_ref, vt_ref, *, tq, tk):
    hp = pl.program_id(1)
    i = pl.program_id(2)
    s = k_ref.shape[1]

    @pl.when(i == 0)
    def _():
        lane_s = lax.broadcasted_iota(jnp.int32, (s, LANES), 1)
        kn = _pair_rms(k_ref[0], gk_ref[...], lane_s < HEAD_DIM)
        cum = cum_ref[0]
        for h in range(2):
            ck = _lane_col(cum, lane_s, MISC_F + 2 * hp + h)
            ka_ref[:, h * LANES:(h + 1) * LANES] = _augment(kn, -ck, lane_s, h, True).astype(BF16)
        vt_ref[...] = v_ref[0].T.astype(BF16)

    lane = lax.broadcasted_iota(jnp.int32, (tq, LANES), 1)
    qn = _pair_rms(q_ref[0], gq_ref[...], lane < HEAD_DIM) * (HEAD_DIM ** -0.5)
    cum_q = cum_ref[0, pl.ds(pl.multiple_of(i * tq, tq), tq), :]
    qa = [_augment(qn, _lane_col(cum_q, lane, MISC_F + 2 * hp + h), lane, h, False).astype(BF16)
          for h in range(2)]

    def scores(h, off):
        return _nt_dot(ka_ref[pl.ds(off, tk), h * LANES:(h + 1) * LANES], qa[h])

    def values_t(h, off):
        return vt_ref[h * HEAD_DIM:(h + 1) * HEAD_DIM, pl.ds(off, tk)]

    o_ref[0] = _causal_flash_t(i, tq, tk, scores, values_t).T.astype(o_ref.dtype)


def fox_attention(y3, cum, gq, gk, *, tq=256, tk=512):
    b, s, _ = y3.shape
    qb, kb, vb = COL_FOX // LANES, (COL_FOX + MIX_W) // LANES, (COL_FOX + 2 * MIX_W) // LANES
    return pl.pallas_call(
        functools.partial(_fox_kernel, tq=tq, tk=tk),
        grid=(b, 2, s // tq),
        in_specs=[pl.BlockSpec((1, tq, LANES), lambda bi, hp, i: (bi, i, qb + hp)),
                  pl.BlockSpec((1, s, LANES), lambda bi, hp, i: (bi, 0, kb + hp)),
                  pl.BlockSpec((1, s, LANES), lambda bi, hp, i: (bi, 0, vb + hp)),
                  pl.BlockSpec((1, s, LANES), lambda bi, hp, i: (bi, 0, 0)),
                  pl.BlockSpec((1, LANES), lambda bi, hp, i: (0, 0)),
                  pl.BlockSpec((1, LANES), lambda bi, hp, i: (0, 0))],
        out_specs=pl.BlockSpec((1, tq, LANES), lambda bi, hp, i: (bi, i, hp)),
        out_shape=jax.ShapeDtypeStruct((b, s, MIX_W), BF16),
        scratch_shapes=[pltpu.VMEM((s, 2 * LANES), BF16), pltpu.VMEM((LANES, s), BF16)],
        compiler_params=_cparams("parallel", "parallel", "arbitrary"),
    )(y3, y3, y3, cum, gq, gk)


def _mla_prep_kernel(cq_ref, ckv_ref, misc_ref, gcq_ref, gckv_ref, wuq_ref, wuk_ref, wuv_ref,
                     qg_ref, kg_ref, cos_ref, sa_ref, sb_ref, q_out, k_out, v_out, *, ts):
    hq = _rms_rows(cq_ref[0], gcq_ref[...]).astype(BF16)
    hkv = _rms_rows(ckv_ref[0], gckv_ref[...]).astype(BF16)
    q = _dot(hq, wuq_ref[...])
    kn = _dot(hkv, wuk_ref[...])
    v_out[0] = _nt_dot(wuv_ref[...], hkv).astype(v_out.dtype)

    lane = lax.broadcasted_iota(jnp.int32, (ts, LANES), 1)
    misc = misc_ref[0]
    kr = pltpu.roll(jnp.where((lane >= MISC_KR) & (lane < MISC_KR + MLA_ROPE), misc, 0.0),
                    MLA_NOPE - MISC_KR, 1)
    cos, sa, sb = cos_ref[...], sa_ref[...], sb_ref[...]
    half = MLA_ROPE // 2

    def norm_rope(t, g):
        t = t * lax.rsqrt(jnp.sum(t * t, axis=-1, keepdims=True) / MLA_QK + EPS) * g
        return t * cos + pltpu.roll(t, LANES - half, 1) * sa + pltpu.roll(t, half, 1) * sb

    for h in range(N_HEADS):
        sl = slice(h * LANES, (h + 1) * LANES)
        q_out[0, :, sl] = (norm_rope(q[:, sl], qg_ref[...]) * (MLA_QK ** -0.5)).astype(q_out.dtype)
        k_out[0, :, sl] = norm_rope(kn[:, sl] + kr, kg_ref[...]).astype(k_out.dtype)


def mla_prep(y3, gcq, gckv, wuq, wuk, wuv, qg, kg, cos, sa, sb, *, ts=256):
    b, s, _ = y3.shape
    const = lambda shape: pl.BlockSpec(shape, lambda bi, i: (0,) * len(shape))
    return pl.pallas_call(
        functools.partial(_mla_prep_kernel, ts=ts),
        grid=(b, s // ts),
        in_specs=[pl.BlockSpec((1, ts, MLA_Q_LORA), lambda bi, i: (bi, i, COL_CQ // MLA_Q_LORA)),
                  pl.BlockSpec((1, ts, LANES), lambda bi, i: (bi, i, COL_CKV // LANES)),
                  pl.BlockSpec((1, ts, LANES), lambda bi, i: (bi, i, COL_MISC // LANES)),
                  const((1, MLA_Q_LORA)), const((1, MLA_KV_LORA)),
                  const((MLA_Q_LORA, N_HEADS * LANES)), const((MLA_KV_LORA, N_HEADS * LANES)),
                  const((MIX_W, MLA_KV_LORA)), const((1, LANES)), const((1, LANES)),
                  pl.BlockSpec((ts, LANES), lambda bi, i: (i, 0)),
                  pl.BlockSpec((ts, LANES), lambda bi, i: (i, 0)),
                  pl.BlockSpec((ts, LANES), lambda bi, i: (i, 0))],
        out_specs=[pl.BlockSpec((1, ts, N_HEADS * LANES), lambda bi, i: (bi, i, 0)),
                   pl.BlockSpec((1, ts, N_HEADS * LANES), lambda bi, i: (bi, i, 0)),
                   pl.BlockSpec((1, MIX_W, ts), lambda bi, i: (bi, 0, i))],
        out_shape=[jax.ShapeDtypeStruct((b, s, N_HEADS * LANES), BF16),
                   jax.ShapeDtypeStruct((b, s, N_HEADS * LANES), BF16),
                   jax.ShapeDtypeStruct((b, MIX_W, s), BF16)],
        compiler_params=_cparams("parallel", "parallel"),
    )(y3, y3, y3, gcq, gckv, wuq, wuk, wuv, qg, kg, cos, sa, sb)


def _mla_kernel(q_ref, k_ref, vt_ref, o_ref, *, tq, tk):
    i = pl.program_id(2)
    qh = [q_ref[0, :, h * LANES:(h + 1) * LANES] for h in range(2)]

    def scores(h, off):
        return _nt_dot(k_ref[0, pl.ds(off, tk), h * LANES:(h + 1) * LANES], qh[h])

    def values_t(h, off):
        return vt_ref[0, h * HEAD_DIM:(h + 1) * HEAD_DIM, pl.ds(off, tk)]

    o_ref[0] = _causal_flash_t(i, tq, tk, scores, values_t).T.astype(o_ref.dtype)


def mla_attention(q, k, vt, *, tq=256, tk=256):
    b, s, _ = q.shape
    return pl.pallas_call(
        functools.partial(_mla_kernel, tq=tq, tk=tk),
        grid=(b, 2, s // tq),
        in_specs=[pl.BlockSpec((1, tq, 2 * LANES), lambda bi, hp, i: (bi, i, hp)),
                  pl.BlockSpec((1, s, 2 * LANES), lambda bi, hp, i: (bi, 0, hp)),
                  pl.BlockSpec((1, LANES, s), lambda bi, hp, i: (bi, hp, 0))],
        out_specs=pl.BlockSpec((1, tq, LANES), lambda bi, hp, i: (bi, i, hp)),
        out_shape=jax.ShapeDtypeStruct((b, s, MIX_W), BF16),
        compiler_params=_cparams("parallel", "parallel", "arbitrary"),
    )(q, k, vt)


def _rel_bias_tile(dist, tab_ref, h):
    max_exact = REL_BUCKETS // 2
    d = jnp.maximum(dist, 0)
    large = max_exact + (jnp.log(jnp.maximum(d, 1).astype(F32) / max_exact)
                         / math.log(REL_MAX_DIST / max_exact)
                         * (REL_BUCKETS - max_exact)).astype(jnp.int32)
    large = jnp.minimum(large, REL_BUCKETS - 1)
    bucket = jnp.where(d < max_exact, d, large)
    out = jnp.zeros(dist.shape, F32)
    for bkt in range(REL_BUCKETS):
        out = jnp.where(bucket == bkt, tab_ref[bkt, h], out)
    return out


def _band_bias_kernel(tab_ref, o_ref):
    r = pl.program_id(0)
    qi = lax.broadcasted_iota(jnp.int32, (LANES, LANES), 0)
    kj = lax.broadcasted_iota(jnp.int32, (LANES, LANES), 1)
    dist = qi + LANES * r - kj
    for h in range(N_HEADS):
        o_ref[0, h * LANES:(h + 1) * LANES, :] = _rel_bias_tile(dist, tab_ref, h)


def _cmp_bias_kernel(tab_ref, o_ref):
    i = pl.program_id(0)
    s = i * LANES + lax.broadcasted_iota(jnp.int32, (LANES, LANES), 0)
    c = lax.broadcasted_iota(jnp.int32, (LANES, LANES), 1)
    dist = s - (c * NSA_CMP_STRIDE + NSA_CMP_LEN - 1)
    for h in range(N_HEADS):
        o_ref[h] = _rel_bias_tile(dist, tab_ref, h)


def rel_bias_tables(rel_bias, s):
    n_band = NSA_WINDOW // LANES + 1
    smem = pl.BlockSpec(memory_space=pltpu.SMEM)
    band = pl.pallas_call(
        _band_bias_kernel, grid=(n_band,), in_specs=[smem],
        out_specs=pl.BlockSpec((1, N_HEADS * LANES, LANES), lambda r: (r, 0, 0)),
        out_shape=jax.ShapeDtypeStruct((n_band, N_HEADS * LANES, LANES), F32),
    )(rel_bias)
    cmpb = pl.pallas_call(
        _cmp_bias_kernel, grid=(s // LANES,), in_specs=[smem],
        out_specs=pl.BlockSpec((N_HEADS, LANES, LANES), lambda i: (0, i, 0)),
        out_shape=jax.ShapeDtypeStruct((N_HEADS, s, LANES), F32),
    )(rel_bias)
    return band, cmpb


def _nsa_compress_kernel(kin_ref, vin_ref, pek_ref, pev_ref, w1k_ref, w1v_ref, w2k_ref, w2v_ref,
                         gk_ref, o_ref):
    half = NSA_CMP_STRIDE * HEAD_DIM

    def compress(x, pe_ref, w1_ref, w2_ref):
        a = _dot((x + pe_ref[:, :half]).astype(BF16), w1_ref[:half, :])
        bh = _dot((x + pe_ref[:, half:]).astype(BF16), w1_ref[half:, :])
        pre = a + pltpu.roll(bh, LANES - 1, 0)
        hid = pre * _sigmoid(pre)
        return _dot(hid.astype(BF16), w2_ref[...])

    kc = compress(kin_ref[0], pek_ref, w1k_ref, w2k_ref)
    vc = compress(vin_ref[0], pev_ref, w1v_ref, w2v_ref)
    kc = kc * lax.rsqrt(jnp.sum(kc * kc, axis=-1, keepdims=True) / HEAD_DIM + EPS) * gk_ref[...]
    o_ref[0] = (kc + vc).astype(o_ref.dtype)


def nsa_compress(kin, vin, pek, pev, w1k, w1v, w2k, w2v, gk):
    b = kin.shape[0]
    const = lambda shape: pl.BlockSpec(shape, lambda bi: (0,) * len(shape))
    blk = pl.BlockSpec((1,) + kin.shape[1:], lambda bi: (bi, 0, 0))
    return pl.pallas_call(
        _nsa_compress_kernel, grid=(b,),
        in_specs=[blk, blk, const(pek.shape), const(pev.shape), const(w1k.shape), const(w1v.shape),
                  const(w2k.shape), const(w2v.shape), const(gk.shape)],
        out_specs=pl.BlockSpec((1, LANES, LANES), lambda bi: (bi, 0, 0)),
        out_shape=jax.ShapeDtypeStruct((b, LANES, LANES), BF16),
        compiler_params=_cparams("parallel"),
    )(kin, vin, pek, pev, w1k, w1v, w2k, w2v, gk)


def _nsa_kernel(q_ref, k2_ref, v2_ref, misc_ref, kcv_ref, band_ref, cmpb_ref, gq_ref, gk2_ref,
                o_ref, k2b_ref, v2b_ref):
    tq = LANES
    rows = N_HEADS * tq
    n_band = NSA_WINDOW // LANES
    i = pl.program_id(1)
    lane_s = lax.broadcasted_iota(jnp.int32, (1, LANES), 1)

    @pl.when(i == 0)
    def _():
        k2b_ref[...] = _pair_rms(k2_ref[0], gk2_ref[...], lane_s < HEAD_DIM).astype(BF16)
        v2b_ref[...] = v2_ref[0].astype(BF16)

    lane = lax.broadcasted_iota(jnp.int32, (tq, LANES), 1)
    lo = lane < HEAD_DIM
    scale = HEAD_DIM ** -0.5
    q = q_ref[0]
    qa = _pair_rms(q[:, :LANES], gq_ref[:, :LANES], lo) * scale
    qb = _pair_rms(q[:, LANES:], gq_ref[:, LANES:], lo) * scale
    slc_parts, win_parts = [], []
    for pair in (qa, qb):
        h_even = jnp.where(lo, pair, 0.0)
        h_odd = jnp.where(lo, 0.0, pair)
        slc_parts += [h_even, pltpu.roll(h_odd, HEAD_DIM, 1)]
        win_parts += [pltpu.roll(h_even, HEAD_DIM, 1), h_odd]
    q_lo = jnp.concatenate(slc_parts, axis=0).astype(BF16)
    q_hi = jnp.concatenate(win_parts, axis=0).astype(BF16)

    row = lax.broadcasted_iota(jnp.int32, (rows, LANES), 0)
    lane_r = lax.broadcasted_iota(jnp.int32, (rows, LANES), 1)
    tpos = i * tq + (row & (tq - 1))

    kcv = kcv_ref[0]
    s_c = _nt_dot(q_lo, kcv) + cmpb_ref[...].reshape(rows, LANES)
    valid = (lane_r * NSA_CMP_STRIDE + NSA_CMP_LEN - 1) <= tpos
    s_c = jnp.where(valid, s_c, MASKED)
    e_c = jnp.where(valid, jnp.exp(s_c - jnp.max(s_c, axis=-1, keepdims=True)), 0.0)
    den = jnp.sum(e_c, axis=-1, keepdims=True)
    p_c = e_c / jnp.where(den > 0.0, den, 1.0)
    o_cmp = _dot(p_c.astype(BF16), kcv)

    p_sum = p_c[0:tq] + p_c[tq:2 * tq] + p_c[2 * tq:3 * tq] + p_c[3 * tq:4 * tq]
    jj = lax.broadcasted_iota(jnp.int32, (LANES, LANES), 0)
    c0 = lax.broadcasted_iota(jnp.int32, (LANES, LANES), 1) * NSA_CMP_STRIDE
    j0 = jj * NSA_SEL_LEN
    overlap_t = jnp.where((c0 < j0 + NSA_SEL_LEN) & (c0 + NSA_CMP_LEN > j0), 1.0, 0.0).astype(BF16)
    hi, mid, lo3 = _split3(p_sum)
    imp_t = _nt_dot(overlap_t, hi) + _nt_dot(overlap_t, mid) + _nt_dot(overlap_t, lo3)
    n_sel = k2_ref.shape[1] // NSA_SEL_LEN
    imp = imp_t[0:n_sel]
    jj32 = lax.broadcasted_iota(jnp.int32, (n_sel, tq), 0)
    cur = (i * tq + lax.broadcasted_iota(jnp.int32, (n_sel, tq), 1)) >> 6
    forced = (jj32 == 0) | (jj32 == cur) | (jj32 == cur - 1)
    imp = jnp.where(forced, FORCE, imp)
    imp = jnp.where(jj32 <= cur, imp, -FORCE)
    cnt = jnp.zeros((n_sel, tq), F32)
    for jp in range(n_sel):
        other = imp[jp:jp + 1, :]
        beats = (other > imp) | ((other == imp) & (jj32 > jp))
        cnt = cnt + jnp.where(beats, 1.0, 0.0)
    sel_t = jnp.where(cnt < float(NSA_TOP_N), 1.0, 0.0)
    sel_t = jnp.concatenate([sel_t, jnp.zeros((LANES - n_sel, tq), F32)], axis=0)
    sel = sel_t.T.astype(BF16)

    kk = lax.broadcasted_iota(jnp.int32, (LANES, LANES), 1)

    def tile(kt, carry, with_win):
        off = pl.multiple_of(kt * LANES, LANES)
        k = k2b_ref[pl.ds(off, LANES), :]
        v = v2b_ref[pl.ds(off, LANES), :]
        bias = band_ref[jnp.minimum(i - kt, n_band)]
        kpos = kt * LANES + lane_r
        expand = jnp.where(jj == 2 * kt + (kk >> 6), 1.0, 0.0).astype(BF16)
        key_sel = _dot(sel, expand)
        key_sel = jnp.concatenate([key_sel] * N_HEADS, axis=0)
        c_slc, c_win = carry
        s1 = _nt_dot(q_lo, k) + bias
        c_slc = _online_update(c_slc, s1, (key_sel > 0.5) & (kpos <= tpos), v)
        if with_win:
            s2 = _nt_dot(q_hi, k) + bias
            wd = tpos - kpos
            c_win = _online_update(c_win, s2, (wd >= 0) & (wd < NSA_WINDOW), v)
        return c_slc, c_win

    first_win = jnp.maximum(i - n_band, 0)
    carry = (_softmax_init(rows, LANES), _softmax_init(rows, LANES))
    carry = lax.fori_loop(0, first_win, lambda kt, c: tile(kt, c, False), carry)
    carry = lax.fori_loop(first_win, i + 1, lambda kt, c: tile(kt, c, True), carry)
    (_, l1, a1), (_, l2, a2) = carry
    o_slc = pltpu.roll(a1 / l1, HEAD_DIM, 1)
    o_win = a2 / l2

    g = _sigmoid(misc_ref[0])
    heads = []
    for h in range(N_HEADS):
        sl = slice(h * tq, (h + 1) * tq)
        g_cmp = _lane_col(g, lane, MISC_G + h)
        g_slc = _lane_col(g, lane, MISC_G + N_HEADS + h)
        g_win = _lane_col(g, lane, MISC_G + 2 * N_HEADS + h)
        heads.append(g_cmp * o_cmp[sl] + g_slc * o_slc[sl] + g_win * o_win[sl])
    for pair in range(2):
        packed = jnp.where(lo, pltpu.roll(heads[2 * pair], HEAD_DIM, 1), heads[2 * pair + 1])
        o_ref[0, :, pair * LANES:(pair + 1) * LANES] = packed.astype(o_ref.dtype)


def nsa_attention(y3, kcv, band, cmpb, gq, gk2):
    b, s, _ = y3.shape
    tq = LANES
    return pl.pallas_call(
        _nsa_kernel,
        grid=(b, s // tq),
        in_specs=[pl.BlockSpec((1, tq, MIX_W), lambda bi, i: (bi, i, COL_NSAQ // MIX_W)),
                  pl.BlockSpec((1, s, LANES), lambda bi, i: (bi, 0, COL_K2 // LANES)),
                  pl.BlockSpec((1, s, LANES), lambda bi, i: (bi, 0, COL_V2 // LANES)),
                  pl.BlockSpec((1, tq, LANES), lambda bi, i: (bi, i, COL_MISC // LANES)),
                  pl.BlockSpec((1, LANES, LANES), lambda bi, i: (bi, 0, 0)),
                  pl.BlockSpec(band.shape, lambda bi, i: (0, 0, 0)),
                  pl.BlockSpec((N_HEADS, tq, LANES), lambda bi, i: (0, i, 0)),
                  pl.BlockSpec((1, MIX_W), lambda bi, i: (0, 0)),
                  pl.BlockSpec((1, LANES), lambda bi, i: (0, 0))],
        out_specs=pl.BlockSpec((1, tq, MIX_W), lambda bi, i: (bi, i, 0)),
        out_shape=jax.ShapeDtypeStruct((b, s, MIX_W), BF16),
        scratch_shapes=[pltpu.VMEM((s, LANES), BF16), pltpu.VMEM((s, LANES), BF16)],
        compiler_params=_cparams("parallel", "arbitrary"),
    )(y3, y3, y3, y3, kcv, band, cmpb, gq, gk2)


def _merge_kernel(ysb_ref, ymla_ref, ynsa_ref, yfox_ref, g0_ref, g1_ref, g2_ref, g3_ref,
                  wb_ref, wo_ref, x_ref, o_ref):
    u = None
    for n, (y_ref, g_ref) in enumerate(((ysb_ref, g0_ref), (ymla_ref, g1_ref),
                                        (ynsa_ref, g2_ref), (yfox_ref, g3_ref))):
        term = _sigmoid(g_ref[...]) * _dot(y_ref[...], wb_ref[n])
        u = term if u is None else u + term
    o_ref[...] = x_ref[...] + _dot(u.astype(BF16), wo_ref[...])


def merge_branches(ys, y, wb, wo, x, *, tm=256):
    t, d = x.shape
    gate_blk = COL_GATE // d
    yspec = pl.BlockSpec((tm, MIX_W), lambda i: (i, 0))
    gspecs = [pl.BlockSpec((tm, d), lambda i, n=n: (i, gate_blk + n)) for n in range(N_BRANCH)]
    return pl.pallas_call(
        _merge_kernel, grid=(t // tm,),
        in_specs=[yspec] * 4 + gspecs + [pl.BlockSpec(wb.shape, lambda i: (0, 0, 0)),
                                         pl.BlockSpec(wo.shape, lambda i: (0, 0)),
                                         pl.BlockSpec((tm, d), lambda i: (i, 0))],
        out_specs=pl.BlockSpec((tm, d), lambda i: (i, 0)),
        out_shape=jax.ShapeDtypeStruct((t, d), F32),
        compiler_params=_cparams("parallel"),
    )(*ys, y, y, y, y, wb, wo, x)


def _mlp_kernel(x_ref, g_ref, wu_ref, wd_ref, o_ref, h_ref, acc_ref):
    f = pl.program_id(1)

    @pl.when(f == 0)
    def _():
        h_ref[...] = _rms_rows(x_ref[...], g_ref[...]).astype(BF16)
        acc_ref[...] = jnp.zeros_like(acc_ref)

    a = jnp.maximum(_dot(h_ref[...], wu_ref[...]), 0.0)
    acc_ref[...] += _dot((a * a).astype(BF16), wd_ref[...])

    @pl.when(f == pl.num_programs(1) - 1)
    def _():
        o_ref[...] = x_ref[...] + acc_ref[...]


def mlp(x, g, wu, wd, *, tm=512, tf=512):
    t, d = x.shape
    ff = wu.shape[1]
    return pl.pallas_call(
        _mlp_kernel, grid=(t // tm, ff // tf),
        in_specs=[pl.BlockSpec((tm, d), lambda i, f: (i, 0)),
                  pl.BlockSpec((1, d), lambda i, f: (0, 0)),
                  pl.BlockSpec((d, tf), lambda i, f: (0, f)),
                  pl.BlockSpec((tf, d), lambda i, f: (f, 0))],
        out_specs=pl.BlockSpec((tm, d), lambda i, f: (i, 0)),
        out_shape=jax.ShapeDtypeStruct((t, d), F32),
        scratch_shapes=[pltpu.VMEM((tm, d), BF16), pltpu.VMEM((tm, d), F32)],
        compiler_params=_cparams("parallel", "arbitrary"),
    )(x, g.reshape(1, d), wu, wd)


def _ple_kernel(x_ref, g_ref, wg_ref, p_ref, wp_ref, o_ref):
    x = x_ref[...]
    gate = _sigmoid(_dot(_rms_rows(x, g_ref[...]).astype(BF16), wg_ref[...]))
    o_ref[...] = x + gate * _dot(p_ref[...].astype(BF16), wp_ref[...])


def ple(x, g, wg, p_all, layer, wp, *, tm=512):
    t, d = x.shape
    return pl.pallas_call(
        _ple_kernel, grid=(t // tm,),
        in_specs=[pl.BlockSpec((tm, d), lambda i: (i, 0)),
                  pl.BlockSpec((1, d), lambda i: (0, 0)),
                  pl.BlockSpec((d, d), lambda i: (0, 0)),
                  pl.BlockSpec((None, tm, PLE_DIM), lambda i: (layer, i, 0)),
                  pl.BlockSpec((PLE_DIM, d), lambda i: (0, 0))],
        out_specs=pl.BlockSpec((tm, d), lambda i: (i, 0)),
        out_shape=jax.ShapeDtypeStruct((t, d), F32),
        compiler_params=_cparams("parallel"),
    )(x, g.reshape(1, d), wg, p_all, wp)


def _pack_w_in(w):
    offs = np.concatenate([[0], np.cumsum(IN_WIDTHS)]).tolist()
    (sb_q, sb_k, sb_v, cq, ckv, kr, nsa_q, kc, vc, ks, vs, kw, vw, ng,
     fox_q, fox_k, fox_v, ff, gate) = [w[:, offs[n]:offs[n + 1]] for n in range(len(IN_WIDTHS))]
    d = w.shape[0]
    z = lambda n: jnp.zeros((d, n), w.dtype)
    misc = jnp.concatenate([ff, ng, z(MISC_KR - MISC_G - 3 * N_HEADS), kr,
                            z(LANES - MISC_KR - MLA_ROPE)], axis=1)
    packed = jnp.concatenate([cq, ckv, ks, kw, vs, vw, kc, vc, misc, gate, nsa_q,
                              sb_q, sb_k, sb_v, fox_q, fox_k, fox_v], axis=1)
    return packed.astype(BF16)


def _head_slots(w, width):
    k = w.shape[0]
    w = w.reshape(k, N_HEADS, width)
    return jnp.pad(w, ((0, 0), (0, 0), (0, LANES - width))).reshape(k, N_HEADS * LANES)


def _rope_tables(s):
    half = MLA_ROPE // 2
    inv = jnp.exp(-math.log(ROPE_THETA) * jnp.arange(half, dtype=F32) / half)
    ang = jnp.arange(s, dtype=F32)[:, None] * inv[None, :]
    cos, sin = jnp.cos(ang), jnp.sin(ang)
    ones = jnp.ones((s, MLA_NOPE), F32)
    zeros = lambda n: jnp.zeros((s, n), F32)
    tail = LANES - MLA_QK
    cos_t = jnp.concatenate([ones, cos, cos, jnp.ones((s, tail), F32)], axis=1)
    sa_t = jnp.concatenate([zeros(MLA_NOPE), -sin, zeros(half), zeros(tail)], axis=1)
    sb_t = jnp.concatenate([zeros(MLA_NOPE), zeros(half), sin, zeros(tail)], axis=1)
    return cos_t, sa_t, sb_t


def _pad_lanes(v, left=0):
    v = v.reshape(1, -1)
    return jnp.pad(v, ((0, 0), (left, LANES - left - v.shape[1])))


def kernel(x, p, rel_bias, norm_mix_g, w_in, mla_cq_norm_g, mla_ckv_norm_g, mla_w_uq, mla_w_ukv,
           mla_qn_g, mla_kn_g, nsa_pe_k, nsa_pe_v, nsa_w1_k, nsa_w2_k, nsa_w1_v, nsa_w2_v,
           nsa_qn_g, nsa_kn_g, fox_f_bias, fox_qn_g, fox_kn_g, w_branch, w_o, norm_mlp_g,
           w_mlp_up, w_mlp_down, norm_ple_g, w_ple_gate, w_ple_proj):
    b, s, d = x.shape
    t = b * s
    xf = x.reshape(t, d)
    p_all = p.reshape(DEPTH, t, PLE_DIM)
    band, cmpb = rel_bias_tables(rel_bias.astype(F32), s)
    cos_t, sa_t, sb_t = _rope_tables(s)
    n_cmp_in = NSA_CMP_STRIDE * HEAD_DIM

    for i in range(DEPTH):
        y = norm_matmul(xf, norm_mix_g[i], _pack_w_in(w_in[i]))
        y3 = y.reshape(b, s, N_PACKED)

        y_sb = sb_attention(y3)

        wukv = mla_w_ukv[i].reshape(MLA_KV_LORA, N_HEADS, MLA_NOPE + MLA_V)
        q_m, k_m, v_m = mla_prep(
            y3, mla_cq_norm_g[i].reshape(1, -1), mla_ckv_norm_g[i].reshape(1, -1),
            _head_slots(mla_w_uq[i], MLA_QK).astype(BF16),
            _head_slots(wukv[:, :, :MLA_NOPE].reshape(MLA_KV_LORA, -1), MLA_NOPE).astype(BF16),
            wukv[:, :, MLA_NOPE:].reshape(MLA_KV_LORA, -1).T.astype(BF16),
            _pad_lanes(mla_qn_g[i]), _pad_lanes(mla_kn_g[i]), cos_t, sa_t, sb_t)
        y_mla = mla_attention(q_m, k_m, v_m)

        kin = y3[:, :, COL_KCVC:COL_KCVC + HEAD_DIM].reshape(b, s // NSA_CMP_STRIDE, n_cmp_in)
        vin = y3[:, :, COL_KCVC + HEAD_DIM:COL_KCVC + 2 * HEAD_DIM].reshape(
            b, s // NSA_CMP_STRIDE, n_cmp_in)
        pad_cols = lambda w, left: jnp.pad(w, ((0, 0), (left, LANES - left - w.shape[1])))
        kcv = nsa_compress(
            kin, vin, nsa_pe_k[i].reshape(1, -1), nsa_pe_v[i].reshape(1, -1),
            pad_cols(nsa_w1_k[i], 0).astype(BF16), pad_cols(nsa_w1_v[i], 0).astype(BF16),
            jnp.pad(pad_cols(nsa_w2_k[i], 0), ((0, LANES - HEAD_DIM), (0, 0))).astype(BF16),
            jnp.pad(pad_cols(nsa_w2_v[i], HEAD_DIM), ((0, LANES - HEAD_DIM), (0, 0))).astype(BF16),
            _pad_lanes(nsa_kn_g[i, 0]))
        y_nsa = nsa_attention(
            y3, kcv, band, cmpb, jnp.tile(nsa_qn_g[i], N_HEADS).reshape(1, -1),
            jnp.concatenate([nsa_kn_g[i, 1], nsa_kn_g[i, 2]]).reshape(1, -1))

        cum = fox_cum(y3, _pad_lanes(fox_f_bias[i], MISC_F))
        y_fox = fox_attention(y3, cum, jnp.tile(fox_qn_g[i], 2).reshape(1, -1),
                              jnp.tile(fox_kn_g[i], 2).reshape(1, -1))

        ys = [a.reshape(t, MIX_W) for a in (y_sb, y_mla, y_nsa, y_fox)]
        xf = merge_branches(ys, y, w_branch[i].astype(BF16), w_o[i].astype(BF16), xf)
        xf = mlp(xf, norm_mlp_g[i], w_mlp_up[i].astype(BF16), w_mlp_down[i].astype(BF16))
        xf = ple(xf, norm_ple_g[i], w_ple_gate[i].astype(BF16), p_all, i,
                 w_ple_proj[i].astype(BF16))
    return xf.reshape(b, s, d)
```

```python
import functools
import math

import numpy as np
import jax
import jax.numpy as jnp
from jax import lax
from jax.experimental import pallas as pl
from jax.experimental.pallas import tpu as pltpu

F32 = jnp.float32
BF16 = jnp.bfloat16

D_MODEL = 1024
DEPTH = 4
HEAD_DIM = 64
N_HEADS = 4
MIX_W = N_HEADS * HEAD_DIM
N_BRANCH = 4
EPS = 1e-6
FORCE = 1e9
MLA_Q_LORA = 384
MLA_KV_LORA = 128
MLA_NOPE = 64
MLA_ROPE = 32
MLA_V = 64
MLA_QK = MLA_NOPE + MLA_ROPE
ROPE_THETA = 10000.0
NSA_CMP_LEN = 32
NSA_CMP_STRIDE = 16
NSA_SEL_LEN = 64
NSA_TOP_N = 16
NSA_WINDOW = 512
REL_BUCKETS = 32
REL_MAX_DIST = 128
D_FF = 4 * D_MODEL
PLE_DIM = 256

LANES = 128
MASKED = -1e30
VMEM_LIMIT = 56 * 1024 * 1024

IN_WIDTHS = ((MIX_W,) * 3
             + (MLA_Q_LORA, MLA_KV_LORA, MLA_ROPE)
             + (MIX_W,) + (HEAD_DIM,) * 6 + (3 * N_HEADS,)
             + (MIX_W,) * 3 + (N_HEADS,)
             + (N_BRANCH * D_MODEL,))

COL_CQ = 0
COL_CKV = 384
COL_K2 = 512
COL_V2 = 640
COL_KCVC = 768
COL_MISC = 896
COL_GATE = 1024
COL_NSAQ = 5120
COL_SB = 5376
COL_FOX = 6144
N_PACKED = 6912
MISC_F = 0
MISC_G = 4
MISC_KR = 32

NSA_PAD_LANE = HEAD_DIM
NSA_SEL_LANE0 = HEAD_DIM + 1
NSA_FAR_TILE = 512


def _cparams(*sem):
    return pltpu.CompilerParams(dimension_semantics=sem, vmem_limit_bytes=VMEM_LIMIT)


def _nt_dot(a, b):
    return lax.dot_general(a, b, (((1,), (1,)), ((), ())), preferred_element_type=F32)


def _dot(a, b):
    return jnp.dot(a, b, preferred_element_type=F32)


def _split3(x):
    hi = x.astype(BF16)
    r1 = x - hi.astype(F32)
    mid = r1.astype(BF16)
    lo = (r1 - mid.astype(F32)).astype(BF16)
    return hi, mid, lo


def _softplus(z):
    return jnp.maximum(z, 0.0) + jnp.log(1.0 + jnp.exp(-jnp.abs(z)))


def _sigmoid(z):
    return 1.0 / (1.0 + jnp.exp(-z))


def _rms_rows(x, g):
    r = lax.rsqrt(jnp.mean(x * x, axis=-1, keepdims=True) + EPS)
    return x * r * g


def _pair_rms(x, g, lo):
    x2 = x * x
    s0 = jnp.sum(jnp.where(lo, x2, 0.0), axis=-1, keepdims=True)
    s1 = jnp.sum(jnp.where(lo, 0.0, x2), axis=-1, keepdims=True)
    r = jnp.where(lo, lax.rsqrt(s0 / HEAD_DIM + EPS), lax.rsqrt(s1 / HEAD_DIM + EPS))
    return x * r * g


def _lane_col(x, lane, idx):
    return jnp.sum(jnp.where(lane == idx, x, 0.0), axis=-1, keepdims=True)


def _first_t(s, vt):
    m = jnp.max(s, axis=0, keepdims=True)
    p = jnp.exp(s - m)
    return m, jnp.sum(p, axis=0, keepdims=True), _dot(vt, p.astype(BF16))


def _update_t(carry, s, vt, mask):
    m, l, acc = carry
    if mask is not None:
        s = jnp.where(mask, s, MASKED)
    m_new = jnp.maximum(m, jnp.max(s, axis=0, keepdims=True))
    p = jnp.exp(s - m_new)
    if mask is not None:
        p = jnp.where(mask, p, 0.0)
    alpha = jnp.exp(m - m_new)
    l = alpha * l + jnp.sum(p, axis=0, keepdims=True)
    acc = alpha * acc + _dot(vt, p.astype(BF16))
    return m_new, l, acc


def _init_t(tq):
    return (jnp.full((1, tq), MASKED, F32), jnp.zeros((1, tq), F32),
            jnp.zeros((HEAD_DIM, tq), F32))


def _causal_flash_t(i, tq, tk, score_fn, vt_fn):
    assert tk % tq == 0
    diff = (lax.broadcasted_iota(jnp.int32, (tk, tq), 0)
            - lax.broadcasted_iota(jnp.int32, (tk, tq), 1))
    n_full = (i * tq) // tk

    def scores_at(kt):
        off = pl.multiple_of(kt * tk, tk)
        return tuple(score_fn(h, off) for h in range(2))

    def body(kt, carry):
        cur, state = carry
        nxt = scores_at(kt + 1)
        off = pl.multiple_of(kt * tk, tk)
        state = tuple(_update_t(state[h], cur[h], vt_fn(h, off), None) for h in range(2))
        return nxt, state

    carry = (scores_at(0), (_init_t(tq), _init_t(tq)))
    cur, state = lax.fori_loop(0, n_full, body, carry)
    off = pl.multiple_of(n_full * tk, tk)
    mask = diff <= i * tq - off
    state = tuple(_update_t(state[h], cur[h], vt_fn(h, off), mask) for h in range(2))
    return jnp.concatenate([acc / l for (_, l, acc) in state], axis=0)


def _augment(x, col, lane, h, key_side):
    live = (lane < HEAD_DIM) if h == 0 else (lane >= HEAD_DIM)
    a0 = HEAD_DIM if h == 0 else 0
    hi, mid, lo3 = (t.astype(F32) for t in _split3(col))
    c0, o0 = (a0, a0 + 3) if key_side else (a0 + 3, a0)
    aug = jnp.where(lane == c0, hi,
                    jnp.where(lane == c0 + 1, mid,
                              jnp.where(lane == c0 + 2, lo3,
                                        jnp.where((lane >= o0) & (lane < o0 + 3), 1.0, 0.0))))
    return jnp.where(live, x, aug)


def _norm_matmul_kernel(x_ref, g_ref, w_ref, o_ref, h_ref):
    @pl.when(pl.program_id(1) == 0)
    def _():
        h_ref[...] = _rms_rows(x_ref[...], g_ref[...]).astype(BF16)

    o_ref[...] = _dot(h_ref[...], w_ref[...])


def norm_matmul(x, g, w, *, tm=1024, tn=768):
    t, d = x.shape
    n = w.shape[1]
    return pl.pallas_call(
        _norm_matmul_kernel,
        grid=(t // tm, n // tn),
        in_specs=[pl.BlockSpec((tm, d), lambda i, j: (i, 0)),
                  pl.BlockSpec((1, d), lambda i, j: (0, 0)),
                  pl.BlockSpec((d, tn), lambda i, j: (0, j))],
        out_specs=pl.BlockSpec((tm, tn), lambda i, j: (i, j)),
        out_shape=jax.ShapeDtypeStruct((t, n), F32),
        scratch_shapes=[pltpu.VMEM((tm, d), BF16)],
        compiler_params=_cparams("parallel", "arbitrary"),
    )(x, g.reshape(1, d), w)


def _sb_kernel(q_ref, k_ref, v_ref, o_ref, kb_ref, vt_ref, *, tq, tk):
    i = pl.program_id(2)

    @pl.when(i == 0)
    def _():
        kb_ref[...] = k_ref[0].astype(BF16)
        vt_ref[...] = v_ref[0].T.astype(BF16)

    sub = LANES
    n_sub = tk // sub
    lane = lax.broadcasted_iota(jnp.int32, (tq, LANES), 1)
    lo = lane < HEAD_DIM
    q = q_ref[0] * (HEAD_DIM ** -0.5)
    qh = (jnp.where(lo, q, 0.0).astype(BF16), jnp.where(lo, 0.0, q).astype(BF16))
    rr = lax.broadcasted_iota(jnp.int32, (sub, sub), 0)
    cc = lax.broadcasted_iota(jnp.int32, (sub, sub), 1)
    upper = jnp.where(cc > rr, 1.0, 0.0).astype(BF16)
    upper2 = jnp.concatenate([upper, upper], axis=1)
    diff = (lax.broadcasted_iota(jnp.int32, (tk, tq), 0)
            - lax.broadcasted_iota(jnp.int32, (tk, tq), 1))

    def logits_at(kt):
        k = kb_ref[pl.ds(pl.multiple_of(kt * tk, tk), tk), :]
        return tuple(_nt_dot(k, qh[h]) for h in range(2))

    def tile(kt, zs, carry, masked):
        off = pl.multiple_of(kt * tk, tk)
        past = (diff < i * tq - off) if masked else None
        out = []
        for h in range(2):
            run, acc = carry[h]
            z = zs[h]
            sp = _softplus(z)
            logit = z - sp
            spm = jnp.where(past, sp, 0.0) if masked else sp
            ws = [None] * n_sub
            for j in reversed(range(n_sub)):
                sl = slice(j * sub, (j + 1) * sub)
                spj = spm[sl]
                hi = spj.astype(BF16)
                mid = (spj - hi.astype(F32)).astype(BF16)
                later = _dot(upper2, jnp.concatenate([hi, mid], axis=0))
                w = jnp.exp(logit[sl] - later - run)
                if masked:
                    w = jnp.where(past[sl], w, 0.0)
                ws[j] = w.astype(BF16)
                run = run + jnp.sum(spj, axis=0, keepdims=True)
            vt = vt_ref[h * HEAD_DIM:(h + 1) * HEAD_DIM, pl.ds(off, tk)]
            out.append((run, acc + _dot(vt, jnp.concatenate(ws, axis=0))))
        return tuple(out)

    n_full = (i * tq) // tk
    zero = (jnp.zeros((1, tq), F32), jnp.zeros((HEAD_DIM, tq), F32))
    zs_diag = logits_at(n_full)
    zs_next = logits_at(jnp.maximum(n_full - 1, 0))
    carry = tile(n_full, zs_diag, (zero, zero), True)

    def body(n, c):
        zs, carry = c
        kt = n_full - 1 - n
        zs_after = logits_at(jnp.maximum(kt - 1, 0))
        return zs_after, tile(kt, zs, carry, False)

    _, carry = lax.fori_loop(0, n_full, body, (zs_next, carry))
    o_t = jnp.concatenate([carry[0][1], carry[1][1]], axis=0)
    o_ref[0] = o_t.T.astype(o_ref.dtype)


def sb_attention(y3, *, tq=256, tk=512):
    b, s, _ = y3.shape
    qb, kb, vb = COL_SB // LANES, (COL_SB + MIX_W) // LANES, (COL_SB + 2 * MIX_W) // LANES
    return pl.pallas_call(
        functools.partial(_sb_kernel, tq=tq, tk=tk),
        grid=(b, 2, s // tq),
        in_specs=[pl.BlockSpec((1, tq, LANES), lambda bi, hp, i: (bi, i, qb + hp)),
                  pl.BlockSpec((1, s, LANES), lambda bi, hp, i: (bi, 0, kb + hp)),
                  pl.BlockSpec((1, s, LANES), lambda bi, hp, i: (bi, 0, vb + hp))],
        out_specs=pl.BlockSpec((1, tq, LANES), lambda bi, hp, i: (bi, i, hp)),
        out_shape=jax.ShapeDtypeStruct((b, s, MIX_W), BF16),
        scratch_shapes=[pltpu.VMEM((s, LANES), BF16), pltpu.VMEM((LANES, s), BF16)],
        compiler_params=_cparams("parallel", "parallel", "arbitrary"),
    )(y3, y3, y3)


def _fox_cum_kernel(misc_ref, fb_ref, cum_ref, *, s):
    rr = lax.broadcasted_iota(jnp.int32, (LANES, LANES), 0)
    cc = lax.broadcasted_iota(jnp.int32, (LANES, LANES), 1)
    lower = jnp.where(cc <= rr, 1.0, 0.0).astype(BF16)

    def body(n, carry):
        off = pl.multiple_of(n * LANES, LANES)
        x = misc_ref[0, pl.ds(off, LANES), :] + fb_ref[...]
        log_f = jnp.minimum(x, 0.0) - jnp.log(1.0 + jnp.exp(-jnp.abs(x)))
        hi, mid, lo3 = _split3(log_f)
        c = _dot(lower, hi) + _dot(lower, mid) + _dot(lower, lo3) + carry
        cum_ref[0, pl.ds(off, LANES), :] = c
        return c[LANES - 1:LANES, :]

    lax.fori_loop(0, s // LANES, body, jnp.zeros((1, LANES), F32))


def fox_cum(y3, fbias_row):
    b, s, _ = y3.shape
    return pl.pallas_call(
        functools.partial(_fox_cum_kernel, s=s),
        grid=(b,),
        in_specs=[pl.BlockSpec((1, s, LANES), lambda bi: (bi, 0, COL_MISC // LANES)),
                  pl.BlockSpec((1, LANES), lambda bi: (0, 0))],
        out_specs=pl.BlockSpec((1, s, LANES), lambda bi: (bi, 0, 0)),
        out_shape=jax.ShapeDtypeStruct((b, s, LANES), F32),
        compiler_params=_cparams("parallel"),
    )(y3, fbias_row)


def _fox_kernel(q_ref, k_ref, v_ref, cum_ref, gq_ref, gk_ref, o_ref, ka_ref, vt_ref, *, tq, tk):
    hp = pl.program_id(1)
    i = pl.program_id(2)
    s = k_ref.shape[1]

    @pl.when(i == 0)
    def _():
        lane_s = lax.broadcasted_iota(jnp.int32, (s, LANES), 1)
        kn = _pair_rms(k_ref[0], gk_ref[...], lane_s < HEAD_DIM)
        cum = cum_ref[0]
        for h in range(2):
            ck = _lane_col(cum, lane_s, MISC_F + 2 * hp + h)
            ka_ref[:, h * LANES:(h + 1) * LANES] = _augment(kn, -ck, lane_s, h, True).astype(BF16)
        vt_ref[...] = v_ref[0].T.astype(BF16)

    lane = lax.broadcasted_iota(jnp.int32, (tq, LANES), 1)
    qn = _pair_rms(q_ref[0], gq_ref[...], lane < HEAD_DIM) * (HEAD_DIM ** -0.5)
    cum_q = cum_ref[0, pl.ds(pl.multiple_of(i * tq, tq), tq), :]
    qa = [_augment(qn, _lane_col(cum_q, lane, MISC_F + 2 * hp + h), lane, h, False).astype(BF16)
          for h in range(2)]

    def scores(h, off):
        return _nt_dot(ka_ref[pl.ds(off, tk), h * LANES:(h + 1) * LANES], qa[h])

    def values_t(h, off):
        return vt_ref[h * HEAD_DIM:(h + 1) * HEAD_DIM, pl.ds(off, tk)]

    o_ref[0] = _causal_flash_t(i, tq, tk, scores, values_t).T.astype(o_ref.dtype)


def fox_attention(y3, cum, gq, gk, *, tq=256, tk=512):
    b, s, _ = y3.shape
    qb, kb, vb = COL_FOX // LANES, (COL_FOX + MIX_W) // LANES, (COL_FOX + 2 * MIX_W) // LANES
    return pl.pallas_call(
        functools.partial(_fox_kernel, tq=tq, tk=tk),
        grid=(b, 2, s // tq),
        in_specs=[pl.BlockSpec((1, tq, LANES), lambda bi, hp, i: (bi, i, qb + hp)),
                  pl.BlockSpec((1, s, LANES), lambda bi, hp, i: (bi, 0, kb + hp)),
                  pl.BlockSpec((1, s, LANES), lambda bi, hp, i: (bi, 0, vb + hp)),
                  pl.BlockSpec((1, s, LANES), lambda bi, hp, i: (bi, 0, 0)),
                  pl.BlockSpec((1, LANES), lambda bi, hp, i: (0, 0)),
                  pl.BlockSpec((1, LANES), lambda bi, hp, i: (0, 0))],
        out_specs=pl.BlockSpec((1, tq, LANES), lambda bi, hp, i: (bi, i, hp)),
        out_shape=jax.ShapeDtypeStruct((b, s, MIX_W), BF16),
        scratch_shapes=[pltpu.VMEM((s, 2 * LANES), BF16), pltpu.VMEM((LANES, s), BF16)],
        compiler_params=_cparams("parallel", "parallel", "arbitrary"),
    )(y3, y3, y3, cum, gq, gk)


def _mla_prep_kernel(cq_ref, ckv_ref, misc_ref, gcq_ref, gckv_ref, wuq_ref, wuk_ref, wuv_ref,
                     qg_ref, kg_ref, cos_ref, sa_ref, sb_ref, q_out, k_out, v_out, *, ts):
    hq = _rms_rows(cq_ref[0], gcq_ref[...]).astype(BF16)
    hkv = _rms_rows(ckv_ref[0], gckv_ref[...]).astype(BF16)
    q = _dot(hq, wuq_ref[...])
    kn = _dot(hkv, wuk_ref[...])
    v_out[0] = _nt_dot(wuv_ref[...], hkv).astype(v_out.dtype)

    lane = lax.broadcasted_iota(jnp.int32, (ts, LANES), 1)
    misc = misc_ref[0]
    kr = pltpu.roll(jnp.where((lane >= MISC_KR) & (lane < MISC_KR + MLA_ROPE), misc, 0.0),
                    MLA_NOPE - MISC_KR, 1)
    cos, sa, sb = cos_ref[...], sa_ref[...], sb_ref[...]
    half = MLA_ROPE // 2

    def norm_rope(t, g):
        t = t * lax.rsqrt(jnp.sum(t * t, axis=-1, keepdims=True) / MLA_QK + EPS) * g
        return t * cos + pltpu.roll(t, LANES - half, 1) * sa + pltpu.roll(t, half, 1) * sb

    for h in range(N_HEADS):
        sl = slice(h * LANES, (h + 1) * LANES)
        q_out[0, :, sl] = (norm_rope(q[:, sl], qg_ref[...]) * (MLA_QK ** -0.5)).astype(q_out.dtype)
        k_out[0, :, sl] = norm_rope(kn[:, sl] + kr, kg_ref[...]).astype(k_out.dtype)


def mla_prep(y3, gcq, gckv, wuq, wuk, wuv, qg, kg, cos, sa, sb, *, ts=256):
    b, s, _ = y3.shape
    const = lambda shape: pl.BlockSpec(shape, lambda bi, i: (0,) * len(shape))
    return pl.pallas_call(
        functools.partial(_mla_prep_kernel, ts=ts),
        grid=(b, s // ts),
        in_specs=[pl.BlockSpec((1, ts, MLA_Q_LORA), lambda bi, i: (bi, i, COL_CQ // MLA_Q_LORA)),
                  pl.BlockSpec((1, ts, LANES), lambda bi, i: (bi, i, COL_CKV // LANES)),
                  pl.BlockSpec((1, ts, LANES), lambda bi, i: (bi, i, COL_MISC // LANES)),
                  const((1, MLA_Q_LORA)), const((1, MLA_KV_LORA)),
                  const((MLA_Q_LORA, N_HEADS * LANES)), const((MLA_KV_LORA, N_HEADS * LANES)),
                  const((MIX_W, MLA_KV_LORA)), const((1, LANES)), const((1, LANES)),
                  pl.BlockSpec((ts, LANES), lambda bi, i: (i, 0)),
                  pl.BlockSpec((ts, LANES), lambda bi, i: (i, 0)),
                  pl.BlockSpec((ts, LANES), lambda bi, i: (i, 0))],
        out_specs=[pl.BlockSpec((1, ts, N_HEADS * LANES), lambda bi, i: (bi, i, 0)),
                   pl.BlockSpec((1, ts, N_HEADS * LANES), lambda bi, i: (bi, i, 0)),
                   pl.BlockSpec((1, MIX_W, ts), lambda bi, i: (bi, 0, i))],
        out_shape=[jax.ShapeDtypeStruct((b, s, N_HEADS * LANES), BF16),
                   jax.ShapeDtypeStruct((b, s, N_HEADS * LANES), BF16),
                   jax.ShapeDtypeStruct((b, MIX_W, s), BF16)],
        compiler_params=_cparams("parallel", "parallel"),
    )(y3, y3, y3, gcq, gckv, wuq, wuk, wuv, qg, kg, cos, sa, sb)


def _mla_kernel(q_ref, k_ref, vt_ref, o_ref, *, tq, tk):
    i = pl.program_id(2)
    qh = [q_ref[0, :, h * LANES:(h + 1) * LANES] for h in range(2)]

    def scores(h, off):
        return _nt_dot(k_ref[0, pl.ds(off, tk), h * LANES:(h + 1) * LANES], qh[h])

    def values_t(h, off):
        return vt_ref[0, h * HEAD_DIM:(h + 1) * HEAD_DIM, pl.ds(off, tk)]

    o_ref[0] = _causal_flash_t(i, tq, tk, scores, values_t).T.astype(o_ref.dtype)


def mla_attention(q, k, vt, *, tq=256, tk=256):
    b, s, _ = q.shape
    return pl.pallas_call(
        functools.partial(_mla_kernel, tq=tq, tk=tk),
        grid=(b, 2, s // tq),
        in_specs=[pl.BlockSpec((1, tq, 2 * LANES), lambda bi, hp, i: (bi, i, hp)),
                  pl.BlockSpec((1, s, 2 * LANES), lambda bi, hp, i: (bi, 0, hp)),
                  pl.BlockSpec((1, LANES, s), lambda bi, hp, i: (bi, hp, 0))],
        out_specs=pl.BlockSpec((1, tq, LANES), lambda bi, hp, i: (bi, i, hp)),
        out_shape=jax.ShapeDtypeStruct((b, s, MIX_W), BF16),
        compiler_params=_cparams("parallel", "parallel", "arbitrary"),
    )(q, k, vt)


def _rel_bias_tile(dist, tab_ref, h):
    max_exact = REL_BUCKETS // 2
    d = jnp.maximum(dist, 0)
    large = max_exact + (jnp.log(jnp.maximum(d, 1).astype(F32) / max_exact)
                         / math.log(REL_MAX_DIST / max_exact)
                         * (REL_BUCKETS - max_exact)).astype(jnp.int32)
    large = jnp.minimum(large, REL_BUCKETS - 1)
    bucket = jnp.where(d < max_exact, d, large)
    out = jnp.zeros(dist.shape, F32)
    for bkt in range(REL_BUCKETS):
        out = jnp.where(bucket == bkt, tab_ref[bkt, h], out)
    return out


def _band_bias_kernel(tab_ref, slc_ref, win_ref):
    r = pl.program_id(0)
    kj = r * LANES + lax.broadcasted_iota(jnp.int32, (LANES, LANES), 0)
    qi = lax.broadcasted_iota(jnp.int32, (LANES, LANES), 1)
    dist = qi + NSA_WINDOW - kj
    for h in range(N_HEADS):
        delta = _rel_bias_tile(dist, tab_ref, h) - tab_ref[REL_BUCKETS - 1, h]
        slc_ref[h] = jnp.where(dist >= 0, delta, MASKED)
        win_ref[h] = jnp.where((dist >= 0) & (dist < NSA_WINDOW), delta, MASKED)


def _cmp_bias_kernel(tab_ref, o_ref):
    i = pl.program_id(0)
    c = lax.broadcasted_iota(jnp.int32, (LANES, LANES), 0)
    s = i * LANES + lax.broadcasted_iota(jnp.int32, (LANES, LANES), 1)
    dist = s - (c * NSA_CMP_STRIDE + NSA_CMP_LEN - 1)
    for h in range(N_HEADS):
        o_ref[h] = jnp.where(dist >= 0, _rel_bias_tile(dist, tab_ref, h), MASKED)


def rel_bias_tables(rel_bias, s):
    assert REL_BUCKETS == 32 and REL_MAX_DIST == 128 and NSA_WINDOW >= 113
    band = NSA_WINDOW + LANES
    smem = pl.BlockSpec(memory_space=pltpu.SMEM)
    band_spec = pl.BlockSpec((N_HEADS, LANES, LANES), lambda r: (0, r, 0))
    band_shape = jax.ShapeDtypeStruct((N_HEADS, band, LANES), F32)
    band_slc, band_win = pl.pallas_call(
        _band_bias_kernel, grid=(band // LANES,), in_specs=[smem],
        out_specs=[band_spec, band_spec], out_shape=[band_shape, band_shape],
    )(rel_bias)
    cmpb = pl.pallas_call(
        _cmp_bias_kernel, grid=(s // LANES,), in_specs=[smem],
        out_specs=pl.BlockSpec((N_HEADS, LANES, LANES), lambda i: (0, 0, i)),
        out_shape=jax.ShapeDtypeStruct((N_HEADS, LANES, s), F32),
    )(rel_bias)
    return band_slc, band_win, cmpb


def _nsa_compress_kernel(kin_ref, vin_ref, pek_ref, pev_ref, w1k_ref, w1v_ref, w2k_ref, w2v_ref,
                         gk_ref, kc_ref, vct_ref):
    half = NSA_CMP_STRIDE * HEAD_DIM

    def hidden(x, pe_ref, w1_ref):
        a = _dot((x + pe_ref[:, :half]).astype(BF16), w1_ref[:half, :])
        bh = _dot((x + pe_ref[:, half:]).astype(BF16), w1_ref[half:, :])
        pre = a + pltpu.roll(bh, LANES - 1, 0)
        return (pre * _sigmoid(pre)).astype(BF16)

    kc = _dot(hidden(kin_ref[0], pek_ref, w1k_ref), w2k_ref[...])
    kc = kc * lax.rsqrt(jnp.sum(kc * kc, axis=-1, keepdims=True) / HEAD_DIM + EPS) * gk_ref[...]
    kc_ref[0] = kc.astype(kc_ref.dtype)
    vct_ref[0] = _nt_dot(w2v_ref[...], hidden(vin_ref[0], pev_ref, w1v_ref)).astype(vct_ref.dtype)


def nsa_compress(kin, vin, pek, pev, w1k, w1v, w2k, w2v, gk):
    b = kin.shape[0]
    const = lambda shape: pl.BlockSpec(shape, lambda bi: (0,) * len(shape))
    blk = pl.BlockSpec((1,) + kin.shape[1:], lambda bi: (bi, 0, 0))
    out = pl.BlockSpec((1, LANES, LANES), lambda bi: (bi, 0, 0))
    return pl.pallas_call(
        _nsa_compress_kernel, grid=(b,),
        in_specs=[blk, blk, const(pek.shape), const(pev.shape), const(w1k.shape), const(w1v.shape),
                  const(w2k.shape), const(w2v.shape), const(gk.shape)],
        out_specs=[out, out],
        out_shape=[jax.ShapeDtypeStruct((b, LANES, LANES), BF16)] * 2,
        compiler_params=_cparams("parallel"),
    )(kin, vin, pek, pev, w1k, w1v, w2k, w2v, gk)


def _nsa_kernel(q_ref, k2_ref, v2_ref, misc_ref, kc_ref, vct_ref, bslc_ref, bwin_ref, cmpb_ref,
                gq_ref, gk2_ref, o_ref, ks_ref, kw_ref, vt_ref, win_ref, stage_ref):
    tq = LANES
    s = k2_ref.shape[1]
    pad = NSA_WINDOW
    band = NSA_WINDOW + tq
    n_sel = s // NSA_SEL_LEN
    assert NSA_SEL_LANE0 + n_sel <= LANES and pad % NSA_FAR_TILE == 0
    i = pl.program_id(1)

    @pl.when(i == 0)
    def _():
        lane_s = lax.broadcasted_iota(jnp.int32, (s, LANES), 1)
        row_s = lax.broadcasted_iota(jnp.int32, (s, LANES), 0)
        kn = _pair_rms(k2_ref[0], gk2_ref[...], lane_s < HEAD_DIM)
        sel_lane = NSA_SEL_LANE0 + (row_s >> 6)
        ks_ref[pad:, :] = jnp.where(lane_s < HEAD_DIM, kn,
                                    jnp.where(lane_s == sel_lane, 1.0, 0.0)).astype(BF16)
        kw_ref[pad:, :] = jnp.where(lane_s < HEAD_DIM, pltpu.roll(kn, HEAD_DIM, 1),
                                    0.0).astype(BF16)
        lane_p = lax.broadcasted_iota(jnp.int32, (pad, LANES), 1)
        before = jnp.where(lane_p == NSA_PAD_LANE, MASKED, 0.0).astype(BF16)
        ks_ref[:pad, :] = before
        kw_ref[:pad, :] = before
        vt_ref[:, pad:] = v2_ref[0].T.astype(BF16)
        vt_ref[:, :pad] = jnp.zeros((LANES, pad), BF16)

    lane = lax.broadcasted_iota(jnp.int32, (tq, LANES), 1)
    lo = lane < HEAD_DIM
    scale = HEAD_DIM ** -0.5
    q = q_ref[0]
    q_base = []
    for pair in range(2):
        cols = slice(pair * LANES, (pair + 1) * LANES)
        pn = _pair_rms(q[:, cols], gq_ref[:, cols], lo) * scale
        for head in (jnp.where(lo, pn, 0.0), pltpu.roll(jnp.where(lo, 0.0, pn), HEAD_DIM, 1)):
            q_base.append(jnp.where(lane == NSA_PAD_LANE, 1.0, head))
    q_plain = [x.astype(BF16) for x in q_base]

    b0 = pl.multiple_of(i * tq, tq)
    ks_band = ks_ref[pl.ds(b0, band), :]
    kw_band = kw_ref[pl.ds(b0, band), :]
    vs_band = vt_ref[:HEAD_DIM, pl.ds(b0, band)]
    vw_band = vt_ref[HEAD_DIM:, pl.ds(b0, band)]

    kc = kc_ref[0]
    vct = vct_ref[0, :HEAD_DIM, :]
    s_cmp = [_nt_dot(kc, q_plain[h]) + cmpb_ref[h] for h in range(N_HEADS)]
    for h in range(N_HEADS):
        win_ref[h] = _nt_dot(kw_band, q_plain[h])

    def window(h):
        _, l_w, a_w = _first_t(win_ref[h] + bwin_ref[h], vw_band)
        return a_w / l_w

    o_cmp = []
    p_sum = jnp.zeros((LANES, tq), F32)
    for h in range(N_HEADS):
        s_c = s_cmp[h]
        e_c = jnp.where(s_c > 0.5 * MASKED,
                        jnp.exp(s_c - jnp.max(s_c, axis=0, keepdims=True)), 0.0)
        den = jnp.sum(e_c, axis=0, keepdims=True)
        p_c = e_c / jnp.where(den > 0.0, den, 1.0)
        o_cmp.append(_dot(vct, p_c.astype(BF16)))
        p_sum = p_sum + p_c

    jj = lax.broadcasted_iota(jnp.int32, (LANES, LANES), 0)
    c0 = lax.broadcasted_iota(jnp.int32, (LANES, LANES), 1) * NSA_CMP_STRIDE
    j0 = jj * NSA_SEL_LEN
    overlap = jnp.where((c0 < j0 + NSA_SEL_LEN) & (c0 + NSA_CMP_LEN > j0), 1.0, 0.0).astype(BF16)
    hi, mid, lo3 = _split3(p_sum)
    imp = (_dot(overlap, hi) + _dot(overlap, mid) + _dot(overlap, lo3))[0:n_sel]
    o_win = [window(0), window(1)]
    jj32 =lax.broadcasted_iota(jnp.int32, (n_sel, tq), 0)
    cur = (i * tq + lax.broadcasted_iota(jnp.int32, (n_sel, tq), 1)) >> 6
    forced = (jj32 == 0) | (jj32 == cur) | (jj32 == cur - 1)
    imp = jnp.where(forced, FORCE, imp)
    imp = jnp.where(jj32 <= cur, imp, -FORCE)
    cnt = jnp.zeros((n_sel, tq), F32)
    for jp in range(n_sel):
        other = imp[jp:jp + 1, :]
        beats = (other > imp) | ((other == imp) & (jj32 > jp))
        cnt = cnt + jnp.where(beats, 1.0, 0.0)
    sel_neg = jnp.where(cnt < float(NSA_TOP_N), 0.0, MASKED)
    sel_neg = jnp.concatenate([sel_neg, jnp.zeros((LANES - n_sel, tq), F32)], axis=0).T
    sel_neg = pltpu.roll(sel_neg, NSA_SEL_LANE0, 1)
    in_sel = (lane >= NSA_SEL_LANE0) & (lane < NSA_SEL_LANE0 + n_sel)
    band_block0 = NSA_SEL_LANE0 + ((i * tq - pad) >> 6)
    q_band = [jnp.where(in_sel, sel_neg, x).astype(BF16) for x in q_base]
    q_far = [jnp.where(in_sel, jnp.where(lane >= band_block0, MASKED, sel_neg), x).astype(BF16)
             for x in q_base]

    stage_ref[0] = _nt_dot(ks_band, q_band[0])
    stage_ref[1] = _nt_dot(ks_band, q_band[1])
    o_win += [window(2), window(3)]
    slc = []
    for h in range(N_HEADS):
        slc.append(_first_t(stage_ref[h % 2] + bslc_ref[h], vs_band))
        if h + 2 < N_HEADS:
            stage_ref[h % 2] = _nt_dot(ks_band, q_band[h + 2])

    n_far = (jnp.maximum(i * tq - pad, 0) + NSA_FAR_TILE - 1) // NSA_FAR_TILE

    def far_body(kt, state):
        off = pl.multiple_of(pad + kt * NSA_FAR_TILE, NSA_FAR_TILE)
        k_far = ks_ref[pl.ds(off, NSA_FAR_TILE), :]
        v_far = vt_ref[:HEAD_DIM, pl.ds(off, NSA_FAR_TILE)]
        sc = [_nt_dot(k_far, q_far[0]), _nt_dot(k_far, q_far[1])]
        out = []
        for h in range(N_HEADS):
            if h + 2 < N_HEADS:
                sc.append(_nt_dot(k_far, q_far[h + 2]))
            out.append(_update_t(state[h], sc[h], v_far, None))
        return tuple(out)

    slc = lax.fori_loop(0, n_far, far_body, tuple(slc))

    g_t = _sigmoid(misc_ref[0]).T
    heads = []
    for h in range(N_HEADS):
        _, l_s, a_s = slc[h]
        row = lambda n: g_t[MISC_G + n * N_HEADS + h:MISC_G + n * N_HEADS + h + 1, :]
        heads.append(row(0) * o_cmp[h] + row(1) * (a_s / l_s) + row(2) * o_win[h])
    o_ref[0] = jnp.concatenate(heads, axis=0).T.astype(o_ref.dtype)


def nsa_attention(y3, kc, vct, band_slc, band_win, cmpb, gq, gk2):
    b, s, _ = y3.shape
    tq = LANES
    band = NSA_WINDOW + tq
    return pl.pallas_call(
        _nsa_kernel,
        grid=(b, s // tq),
        in_specs=[pl.BlockSpec((1, tq, MIX_W), lambda bi, i: (bi, i, COL_NSAQ // MIX_W)),
                  pl.BlockSpec((1, s, LANES), lambda bi, i: (bi, 0, COL_K2 // LANES)),
                  pl.BlockSpec((1, s, LANES), lambda bi, i: (bi, 0, COL_V2 // LANES)),
                  pl.BlockSpec((1, tq, LANES), lambda bi, i: (bi, i, COL_MISC // LANES)),
                  pl.BlockSpec((1, LANES, LANES), lambda bi, i: (bi, 0, 0)),
                  pl.BlockSpec((1, LANES, LANES), lambda bi, i: (bi, 0, 0)),
                  pl.BlockSpec(band_slc.shape, lambda bi, i: (0, 0, 0)),
                  pl.BlockSpec(band_win.shape, lambda bi, i: (0, 0, 0)),
                  pl.BlockSpec((N_HEADS, LANES, tq), lambda bi, i: (0, 0, i)),
                  pl.BlockSpec((1, MIX_W), lambda bi, i: (0, 0)),
                  pl.BlockSpec((1, LANES), lambda bi, i: (0, 0))],
        out_specs=pl.BlockSpec((1, tq, MIX_W), lambda bi, i: (bi, i, 0)),
        out_shape=jax.ShapeDtypeStruct((b, s, MIX_W), BF16),
        scratch_shapes=[pltpu.VMEM((s + NSA_WINDOW, LANES), BF16),
                        pltpu.VMEM((s + NSA_WINDOW, LANES), BF16),
                        pltpu.VMEM((LANES, s + NSA_WINDOW), BF16),
                        pltpu.VMEM((N_HEADS, band, tq), F32),
                        pltpu.VMEM((2, band, tq), F32)],
        compiler_params=_cparams("parallel", "arbitrary"),
    )(y3, y3, y3, y3, kc, vct, band_slc, band_win, cmpb, gq, gk2)


def _merge_kernel(ysb_ref, ymla_ref, ynsa_ref, yfox_ref, g0_ref, g1_ref, g2_ref, g3_ref,
                  wb_ref, wo_ref, x_ref, o_ref):
    u = None
    for n, (y_ref, g_ref) in enumerate(((ysb_ref, g0_ref), (ymla_ref, g1_ref),
                                        (ynsa_ref, g2_ref), (yfox_ref, g3_ref))):
        term = _sigmoid(g_ref[...]) * _dot(y_ref[...], wb_ref[n])
        u = term if u is None else u + term
    o_ref[...] = x_ref[...] + _dot(u.astype(BF16), wo_ref[...])


def merge_branches(ys, y, wb, wo, x, *, tm=256):
    t, d = x.shape
    gate_blk = COL_GATE // d
    yspec = pl.BlockSpec((tm, MIX_W), lambda i: (i, 0))
    gspecs = [pl.BlockSpec((tm, d), lambda i, n=n: (i, gate_blk + n)) for n in range(N_BRANCH)]
    return pl.pallas_call(
        _merge_kernel, grid=(t // tm,),
        in_specs=[yspec] * 4 + gspecs + [pl.BlockSpec(wb.shape, lambda i: (0, 0, 0)),
                                         pl.BlockSpec(wo.shape, lambda i: (0, 0)),
                                         pl.BlockSpec((tm, d), lambda i: (i, 0))],
        out_specs=pl.BlockSpec((tm, d), lambda i: (i, 0)),
        out_shape=jax.ShapeDtypeStruct((t, d), F32),
        compiler_params=_cparams("parallel"),
    )(*ys, y, y, y, y, wb, wo, x)


def _mlp_kernel(x_ref, g_ref, wu_ref, wd_ref, o_ref, h_ref, acc_ref):
    f = pl.program_id(1)

    @pl.when(f == 0)
    def _():
        h_ref[...] = _rms_rows(x_ref[...], g_ref[...]).astype(BF16)
        acc_ref[...] = jnp.zeros_like(acc_ref)

    a = jnp.maximum(_dot(h_ref[...], wu_ref[...]), 0.0)
    acc_ref[...] += _dot((a * a).astype(BF16), wd_ref[...])

    @pl.when(f == pl.num_programs(1) - 1)
    def _():
        o_ref[...] = x_ref[...] + acc_ref[...]


def mlp(x, g, wu, wd, *, tm=1024, tf=512):
    t, d = x.shape
    ff = wu.shape[1]
    return pl.pallas_call(
        _mlp_kernel, grid=(t // tm, ff // tf),
        in_specs=[pl.BlockSpec((tm, d), lambda i, f: (i, 0)),
                  pl.BlockSpec((1, d), lambda i, f: (0, 0)),
                  pl.BlockSpec((d, tf), lambda i, f: (0, f)),
                  pl.BlockSpec((tf, d), lambda i, f: (f, 0))],
        out_specs=pl.BlockSpec((tm, d), lambda i, f: (i, 0)),
        out_shape=jax.ShapeDtypeStruct((t, d), F32),
        scratch_shapes=[pltpu.VMEM((tm, d), BF16), pltpu.VMEM((tm, d), F32)],
        compiler_params=_cparams("parallel", "arbitrary"),
    )(x, g.reshape(1, d), wu, wd)


def _ple_kernel(x_ref, g_ref, wg_ref, p_ref, wp_ref, o_ref):
    x = x_ref[...]
    gate = _sigmoid(_dot(_rms_rows(x, g_ref[...]).astype(BF16), wg_ref[...]))
    o_ref[...] = x + gate * _dot(p_ref[...].astype(BF16), wp_ref[...])


def ple(x, g, wg, p_all, layer, wp, *, tm=512):
    t, d = x.shape
    return pl.pallas_call(
        _ple_kernel, grid=(t // tm,),
        in_specs=[pl.BlockSpec((tm, d), lambda i: (i, 0)),
                  pl.BlockSpec((1, d), lambda i: (0, 0)),
                  pl.BlockSpec((d, d), lambda i: (0, 0)),
                  pl.BlockSpec((None, tm, PLE_DIM), lambda i: (layer, i, 0)),
                  pl.BlockSpec((PLE_DIM, d), lambda i: (0, 0))],
        out_specs=pl.BlockSpec((tm, d), lambda i: (i, 0)),
        out_shape=jax.ShapeDtypeStruct((t, d), F32),
        compiler_params=_cparams("parallel"),
    )(x, g.reshape(1, d), wg, p_all, wp)


def _pack_w_in(w):
    offs = np.concatenate([[0], np.cumsum(IN_WIDTHS)]).tolist()
    (sb_q, sb_k, sb_v, cq, ckv, kr, nsa_q, kc, vc, ks, vs, kw, vw, ng,
     fox_q, fox_k, fox_v, ff, gate) = [w[:, offs[n]:offs[n + 1]] for n in range(len(IN_WIDTHS))]
    d = w.shape[0]
    z = lambda n: jnp.zeros((d, n), w.dtype)
    misc = jnp.concatenate([ff, ng, z(MISC_KR - MISC_G - 3 * N_HEADS), kr,
                            z(LANES - MISC_KR - MLA_ROPE)], axis=1)
    packed = jnp.concatenate([cq, ckv, ks, kw, vs, vw, kc, vc, misc, gate, nsa_q,
                              sb_q, sb_k, sb_v, fox_q, fox_k, fox_v], axis=1)
    return packed.astype(BF16)


def _head_slots(w, width):
    k = w.shape[0]
    w = w.reshape(k, N_HEADS, width)
    return jnp.pad(w, ((0, 0), (0, 0), (0, LANES - width))).reshape(k, N_HEADS * LANES)


def _rope_tables(s):
    half = MLA_ROPE // 2
    inv = jnp.exp(-math.log(ROPE_THETA) * jnp.arange(half, dtype=F32) / half)
    ang = jnp.arange(s, dtype=F32)[:, None] * inv[None, :]
    cos, sin = jnp.cos(ang), jnp.sin(ang)
    ones = jnp.ones((s, MLA_NOPE), F32)
    zeros = lambda n: jnp.zeros((s, n), F32)
    tail = LANES - MLA_QK
    cos_t = jnp.concatenate([ones, cos, cos, jnp.ones((s, tail), F32)], axis=1)
    sa_t = jnp.concatenate([zeros(MLA_NOPE), -sin, zeros(half), zeros(tail)], axis=1)
    sb_t = jnp.concatenate([zeros(MLA_NOPE), zeros(half), sin, zeros(tail)], axis=1)
    return cos_t, sa_t, sb_t


def _pad_lanes(v, left=0):
    v = v.reshape(1, -1)
    return jnp.pad(v, ((0, 0), (left, LANES - left - v.shape[1])))


def _pad_to_lanes(w):
    return jnp.pad(w, ((0, 0), (0, LANES - w.shape[1])))


def kernel(x, p, rel_bias, norm_mix_g, w_in, mla_cq_norm_g, mla_ckv_norm_g, mla_w_uq, mla_w_ukv,
           mla_qn_g, mla_kn_g, nsa_pe_k, nsa_pe_v, nsa_w1_k, nsa_w2_k, nsa_w1_v, nsa_w2_v,
           nsa_qn_g, nsa_kn_g, fox_f_bias, fox_qn_g, fox_kn_g, w_branch, w_o, norm_mlp_g,
           w_mlp_up, w_mlp_down, norm_ple_g, w_ple_gate, w_ple_proj):
    b, s, d = x.shape
    t = b * s
    xf = x.reshape(t, d)
    p_all = p.reshape(DEPTH, t, PLE_DIM)
    band_slc, band_win, cmpb = rel_bias_tables(rel_bias.astype(F32), s)
    cos_t, sa_t, sb_t = _rope_tables(s)
    n_cmp_in = NSA_CMP_STRIDE * HEAD_DIM

    for i in range(DEPTH):
        y = norm_matmul(xf, norm_mix_g[i], _pack_w_in(w_in[i]))
        y3 = y.reshape(b, s, N_PACKED)

        y_sb = sb_attention(y3)

        wukv = mla_w_ukv[i].reshape(MLA_KV_LORA, N_HEADS, MLA_NOPE + MLA_V)
        q_m, k_m, vt_m = mla_prep(
            y3, mla_cq_norm_g[i].reshape(1, -1), mla_ckv_norm_g[i].reshape(1, -1),
            _head_slots(mla_w_uq[i], MLA_QK).astype(BF16),
            _head_slots(wukv[:, :, :MLA_NOPE].reshape(MLA_KV_LORA, -1), MLA_NOPE).astype(BF16),
            wukv[:, :, MLA_NOPE:].reshape(MLA_KV_LORA, -1).T.astype(BF16),
            _pad_lanes(mla_qn_g[i]), _pad_lanes(mla_kn_g[i]), cos_t, sa_t, sb_t)
        y_mla = mla_attention(q_m, k_m, vt_m)

        kin = y3[:, :, COL_KCVC:COL_KCVC + HEAD_DIM].reshape(b, s // NSA_CMP_STRIDE, n_cmp_in)
        vin = y3[:, :, COL_KCVC + HEAD_DIM:COL_KCVC + 2 * HEAD_DIM].reshape(
            b, s // NSA_CMP_STRIDE, n_cmp_in)
        pad_sq = lambda w: jnp.pad(_pad_to_lanes(w), ((0, LANES - w.shape[0]), (0, 0)))
        kc, vct = nsa_compress(
            kin, vin, nsa_pe_k[i].reshape(1, -1), nsa_pe_v[i].reshape(1, -1),
            _pad_to_lanes(nsa_w1_k[i]).astype(BF16), _pad_to_lanes(nsa_w1_v[i]).astype(BF16),
            pad_sq(nsa_w2_k[i]).astype(BF16), pad_sq(nsa_w2_v[i].T).astype(BF16),
            _pad_lanes(nsa_kn_g[i, 0]))
        y_nsa = nsa_attention(
            y3, kc, vct, band_slc, band_win, cmpb,
            jnp.tile(nsa_qn_g[i], N_HEADS).reshape(1, -1),
            jnp.concatenate([nsa_kn_g[i, 1], nsa_kn_g[i, 2]]).reshape(1, -1))

        cum = fox_cum(y3, _pad_lanes(fox_f_bias[i], MISC_F))
        y_fox = fox_attention(y3, cum, jnp.tile(fox_qn_g[i], 2).reshape(1, -1),
                              jnp.tile(fox_kn_g[i], 2).reshape(1, -1))

        ys = [a.reshape(t, MIX_W) for a in (y_sb, y_mla, y_nsa, y_fox)]
        xf = merge_branches(ys, y, w_branch[i].astype(BF16), w_o[i].astype(BF16), xf)
        xf = mlp(xf, norm_mlp_g[i], w_mlp_up[i].astype(BF16), w_mlp_down[i].astype(BF16))
        xf = ple(xf, norm_ple_g[i], w_ple_gate[i].astype(BF16), p_all, i,
                 w_ple_proj[i].astype(BF16))
    return xf.reshape(b, s, d)
```

```python
import functools
import math

import numpy as np
import jax
import jax.numpy as jnp
from jax import lax
from jax.experimental import pallas as pl
from jax.experimental.pallas import tpu as pltpu

F32 = jnp.float32
BF16 = jnp.bfloat16

D_MODEL = 1024
DEPTH = 4
HEAD_DIM = 64
N_HEADS = 4
MIX_W = N_HEADS * HEAD_DIM
N_BRANCH = 4
EPS = 1e-6
FORCE = 1e9
MLA_Q_LORA = 384
MLA_KV_LORA = 128
MLA_NOPE = 64
MLA_ROPE = 32
MLA_V = 64
MLA_QK = MLA_NOPE + MLA_ROPE
ROPE_THETA = 10000.0
NSA_CMP_LEN = 32
NSA_CMP_STRIDE = 16
NSA_SEL_LEN = 64
NSA_TOP_N = 16
NSA_WINDOW = 512
REL_BUCKETS = 32
REL_MAX_DIST = 128
D_FF = 4 * D_MODEL
PLE_DIM = 256

LANES = 128
MASKED = -1e30
VMEM_LIMIT = 56 * 1024 * 1024

IN_WIDTHS = ((MIX_W,) * 3
             + (MLA_Q_LORA, MLA_KV_LORA, MLA_ROPE)
             + (MIX_W,) + (HEAD_DIM,) * 6 + (3 * N_HEADS,)
             + (MIX_W,) * 3 + (N_HEADS,)
             + (N_BRANCH * D_MODEL,))

COL_CQ = 0
COL_CKV = 384
COL_K2 = 512
COL_V2 = 640
COL_KCVC = 768
COL_MISC = 896
COL_NSAQ = 1024
N_F32 = 1280
COL_GATE = 0
COL_SB = 4096
COL_FOX = 4864
N_BF16 = 5632
MISC_F = 0
MISC_G = 4
MISC_KR = 32

NSA_PAD_LANE = HEAD_DIM
NSA_SEL_LANE0 = HEAD_DIM + 1
NSA_FAR_TILE = 512


def _cparams(*sem):
    return pltpu.CompilerParams(dimension_semantics=sem, vmem_limit_bytes=VMEM_LIMIT)


def _nt_dot(a, b):
    return lax.dot_general(a, b, (((1,), (1,)), ((), ())), preferred_element_type=F32)


def _dot(a, b):
    return jnp.dot(a, b, preferred_element_type=F32)


def _split3(x):
    hi = x.astype(BF16)
    r1 = x - hi.astype(F32)
    mid = r1.astype(BF16)
    lo = (r1 - mid.astype(F32)).astype(BF16)
    return hi, mid, lo


def _softplus(z):
    return jnp.maximum(z, 0.0) + jnp.log(1.0 + jnp.exp(-jnp.abs(z)))


def _sigmoid(z):
    return 1.0 / (1.0 + jnp.exp(-z))


def _rms_rows(x, g):
    r = lax.rsqrt(jnp.mean(x * x, axis=-1, keepdims=True) + EPS)
    return x * r * g


def _pair_rms(x, g, lo):
    x2 = x * x
    s0 = jnp.sum(jnp.where(lo, x2, 0.0), axis=-1, keepdims=True)
    s1 = jnp.sum(jnp.where(lo, 0.0, x2), axis=-1, keepdims=True)
    r = jnp.where(lo, lax.rsqrt(s0 / HEAD_DIM + EPS), lax.rsqrt(s1 / HEAD_DIM + EPS))
    return x * r * g


def _lane_col(x, lane, idx):
    return jnp.sum(jnp.where(lane == idx, x, 0.0), axis=-1, keepdims=True)


def _first_t(s, vt):
    m = jnp.max(s, axis=0, keepdims=True)
    p = jnp.exp(s - m)
    return m, jnp.sum(p, axis=0, keepdims=True), _dot(vt, p.astype(BF16))


def _update_t(carry, s, vt, mask):
    m, l, acc = carry
    if mask is not None:
        s = jnp.where(mask, s, MASKED)
    m_new = jnp.maximum(m, jnp.max(s, axis=0, keepdims=True))
    p = jnp.exp(s - m_new)
    if mask is not None:
        p = jnp.where(mask, p, 0.0)
    alpha = jnp.exp(m - m_new)
    l = alpha * l + jnp.sum(p, axis=0, keepdims=True)
    acc = alpha * acc + _dot(vt, p.astype(BF16))
    return m_new, l, acc


def _init_t(tq):
    return (jnp.full((1, tq), MASKED, F32), jnp.zeros((1, tq), F32),
            jnp.zeros((HEAD_DIM, tq), F32))


def _causal_flash_t(i, tq, tk, score_fn, vt_fn):
    assert tk % tq == 0
    diff = (lax.broadcasted_iota(jnp.int32, (tk, tq), 0)
            - lax.broadcasted_iota(jnp.int32, (tk, tq), 1))
    n_full = (i * tq) // tk

    def scores_at(kt):
        off = pl.multiple_of(kt * tk, tk)
        return tuple(score_fn(h, off) for h in range(2))

    def body(kt, carry):
        cur, state = carry
        nxt = scores_at(kt + 1)
        off = pl.multiple_of(kt * tk, tk)
        state = tuple(_update_t(state[h], cur[h], vt_fn(h, off), None) for h in range(2))
        return nxt, state

    carry = (scores_at(0), (_init_t(tq), _init_t(tq)))
    cur, state = lax.fori_loop(0, n_full, body, carry)
    off = pl.multiple_of(n_full * tk, tk)
    mask = diff <= i * tq - off
    state = tuple(_update_t(state[h], cur[h], vt_fn(h, off), mask) for h in range(2))
    return jnp.concatenate([acc / l for (_, l, acc) in state], axis=0)


def _causal_flash_static(c, tq, tk, score_fn, vt_fn):
    assert tk % tq == 0
    diff = (lax.broadcasted_iota(jnp.int32, (tk, tq), 0)
            - lax.broadcasted_iota(jnp.int32, (tk, tq), 1))
    n_full = (c * tq) // tk
    cur = [score_fn(h, 0) for h in range(2)]
    state = [_init_t(tq), _init_t(tq)]
    for kt in range(n_full + 1):
        nxt = [score_fn(h, (kt + 1) * tk) for h in range(2)] if kt < n_full else None
        mask = (diff <= c * tq - kt * tk) if kt == n_full else None
        state = [_update_t(state[h], cur[h], vt_fn(h, kt * tk), mask) for h in range(2)]
        cur = nxt
    return jnp.concatenate([acc / l for (_, l, acc) in state], axis=0)


def _augment(x, col, lane, h, key_side):
    live = (lane < HEAD_DIM) if h == 0 else (lane >= HEAD_DIM)
    a0 = HEAD_DIM if h == 0 else 0
    hi, mid, lo3 = (t.astype(F32) for t in _split3(col))
    c0, o0 = (a0, a0 + 3) if key_side else (a0 + 3, a0)
    aug = jnp.where(lane == c0, hi,
                    jnp.where(lane == c0 + 1, mid,
                              jnp.where(lane == c0 + 2, lo3,
                                        jnp.where((lane >= o0) & (lane < o0 + 3), 1.0, 0.0))))
    return jnp.where(live, x, aug)


def _norm_proj_kernel(x_ref, g_ref, w_ref, o_ref, h_ref):
    h = _rms_rows(x_ref[...], g_ref[...]).astype(BF16)
    h_ref[...] = h
    o_ref[...] = _dot(h, w_ref[...])


def norm_proj(x, g, w, *, tm=1024):
    t, d = x.shape
    n = w.shape[1]
    return pl.pallas_call(
        _norm_proj_kernel,
        grid=(t // tm,),
        in_specs=[pl.BlockSpec((tm, d), lambda i: (i, 0)),
                  pl.BlockSpec((1, d), lambda i: (0, 0)),
                  pl.BlockSpec((d, n), lambda i: (0, 0))],
        out_specs=[pl.BlockSpec((tm, n), lambda i: (i, 0)),
                   pl.BlockSpec((tm, d), lambda i: (i, 0))],
        out_shape=[jax.ShapeDtypeStruct((t, n), F32), jax.ShapeDtypeStruct((t, d), BF16)],
        compiler_params=_cparams("parallel"),
    )(x, g.reshape(1, d), w)


def _matmul_kernel(h_ref, w_ref, o_ref):
    o_ref[...] = _dot(h_ref[...], w_ref[...]).astype(o_ref.dtype)


def matmul_bf16(h, w, *, tm=2048, tn=512):
    t, d = h.shape
    n = w.shape[1]
    return pl.pallas_call(
        _matmul_kernel,
        grid=(t // tm, n // tn),
        in_specs=[pl.BlockSpec((tm, d), lambda i, j: (i, 0)),
                  pl.BlockSpec((d, tn), lambda i, j: (0, j))],
        out_specs=pl.BlockSpec((tm, tn), lambda i, j: (i, j)),
        out_shape=jax.ShapeDtypeStruct((t, n), BF16),
        compiler_params=_cparams("parallel", "parallel"),
    )(h, w)


def _sb_kernel(q_ref, k_ref, v_ref, o_ref, vt_ref, *, tq, tk):
    i = pl.program_id(2)

    @pl.when(i == 0)
    def _():
        vt_ref[...] = v_ref[0].astype(F32).T.astype(BF16)

    sub = LANES
    n_sub = tk // sub
    lane = lax.broadcasted_iota(jnp.int32, (tq, LANES), 1)
    lo = lane < HEAD_DIM
    q = q_ref[0].astype(F32) * (HEAD_DIM ** -0.5)
    qh = (jnp.where(lo, q, 0.0).astype(BF16), jnp.where(lo, 0.0, q).astype(BF16))
    rr = lax.broadcasted_iota(jnp.int32, (sub, sub), 0)
    cc = lax.broadcasted_iota(jnp.int32, (sub, sub), 1)
    upper = jnp.where(cc > rr, 1.0, 0.0).astype(BF16)
    upper2 = jnp.concatenate([upper, upper], axis=1)
    diff = (lax.broadcasted_iota(jnp.int32, (tk, tq), 0)
            - lax.broadcasted_iota(jnp.int32, (tk, tq), 1))

    def logits_at(kt):
        k = k_ref[0, pl.ds(kt * tk, tk), :]
        return tuple(_nt_dot(k, qh[h]) for h in range(2))

    def tile(c, kt, zs, carry, masked):
        off = kt * tk
        past = (diff < c * tq - off) if masked else None
        out = []
        for h in range(2):
            run, acc = carry[h]
            z = zs[h]
            sp = _softplus(z)
            logit = z - sp
            spm = jnp.where(past, sp, 0.0) if masked else sp
            ws = [None] * n_sub
            for j in reversed(range(n_sub)):
                sl = slice(j * sub, (j + 1) * sub)
                spj = spm[sl]
                hi = spj.astype(BF16)
                mid = (spj - hi.astype(F32)).astype(BF16)
                later = _dot(upper2, jnp.concatenate([hi, mid], axis=0))
                w = jnp.exp(logit[sl] - later - run)
                if masked:
                    w = jnp.where(past[sl], w, 0.0)
                ws[j] = w.astype(BF16)
                run = run + jnp.sum(spj, axis=0, keepdims=True)
            vt = vt_ref[h * HEAD_DIM:(h + 1) * HEAD_DIM, pl.ds(off, tk)]
            out.append((run, acc + _dot(vt, jnp.concatenate(ws, axis=0))))
        return tuple(out)

    zero = (jnp.zeros((1, tq), F32), jnp.zeros((HEAD_DIM, tq), F32))
    for c in range(k_ref.shape[1] // tq):
        @pl.when(i == c)
        def _(c=c):
            n_full = (c * tq) // tk
            zs = logits_at(n_full)
            carry = (zero, zero)
            for kt in range(n_full, -1, -1):
                zs_next = logits_at(kt - 1) if kt > 0 else None
                carry = tile(c, kt, zs, carry, kt == n_full)
                zs = zs_next
            o_t = jnp.concatenate([carry[0][1], carry[1][1]], axis=0)
            o_ref[0] = o_t.T.astype(o_ref.dtype)


def sb_attention(y3, *, tq=256, tk=256):
    b, s, _ = y3.shape
    qb, kb, vb = COL_SB // LANES, (COL_SB + MIX_W) // LANES, (COL_SB + 2 * MIX_W) // LANES
    return pl.pallas_call(
        functools.partial(_sb_kernel, tq=tq, tk=tk),
        grid=(b, 2, s // tq),
        in_specs=[pl.BlockSpec((1, tq, LANES), lambda bi, hp, i: (bi, i, qb + hp)),
                  pl.BlockSpec((1, s, LANES), lambda bi, hp, i: (bi, 0, kb + hp)),
                  pl.BlockSpec((1, s, LANES), lambda bi, hp, i: (bi, 0, vb + hp))],
        out_specs=pl.BlockSpec((1, tq, LANES), lambda bi, hp, i: (bi, i, hp)),
        out_shape=jax.ShapeDtypeStruct((b, s, MIX_W), BF16),
        scratch_shapes=[pltpu.VMEM((LANES, s), BF16)],
        compiler_params=_cparams("parallel", "parallel", "arbitrary"),
    )(y3, y3, y3)


def _fox_cum_kernel(misc_ref, fb_ref, cum_ref, *, s):
    rr = lax.broadcasted_iota(jnp.int32, (LANES, LANES), 0)
    cc = lax.broadcasted_iota(jnp.int32, (LANES, LANES), 1)
    lower = jnp.where(cc <= rr, 1.0, 0.0).astype(BF16)

    def body(n, carry):
        off = pl.multiple_of(n * LANES, LANES)
        x = misc_ref[0, pl.ds(off, LANES), :] + fb_ref[...]
        log_f = jnp.minimum(x, 0.0) - jnp.log(1.0 + jnp.exp(-jnp.abs(x)))
        hi, mid, lo3 = _split3(log_f)
        c = _dot(lower, hi) + _dot(lower, mid) + _dot(lower, lo3) + carry
        cum_ref[0, pl.ds(off, LANES), :] = c
        return c[LANES - 1:LANES, :]

    lax.fori_loop(0, s // LANES, body, jnp.zeros((1, LANES), F32))


def fox_cum(y3, fbias_row):
    b, s, _ = y3.shape
    return pl.pallas_call(
        functools.partial(_fox_cum_kernel, s=s),
        grid=(b,),
        in_specs=[pl.BlockSpec((1, s, LANES), lambda bi: (bi, 0, COL_MISC // LANES)),
                  pl.BlockSpec((1, LANES), lambda bi: (0, 0))],
        out_specs=pl.BlockSpec((1, s, LANES), lambda bi: (bi, 0, 0)),
        out_shape=jax.ShapeDtypeStruct((b, s, LANES), F32),
        compiler_params=_cparams("parallel"),
    )(y3, fbias_row)


def _fox_kernel(q_ref, k_ref, v_ref, cum_ref, gq_ref, gk_ref, o_ref, ka_ref, vt_ref, *, tq, tk):
    hp = pl.program_id(1)
    i = pl.program_id(2)
    s = k_ref.shape[1]

    @pl.when(i == 0)
    def _():
        lane_s = lax.broadcasted_iota(jnp.int32, (s, LANES), 1)
        kn = _pair_rms(k_ref[0].astype(F32), gk_ref[...], lane_s < HEAD_DIM)
        cum = cum_ref[0]
        for h in range(2):
            ck = _lane_col(cum, lane_s, MISC_F + 2 * hp + h)
            ka_ref[:, h * LANES:(h + 1) * LANES] = _augment(kn, -ck, lane_s, h, True).astype(BF16)
        vt_ref[...] = v_ref[0].astype(F32).T.astype(BF16)

    lane = lax.broadcasted_iota(jnp.int32, (tq, LANES), 1)
    qn = _pair_rms(q_ref[0].astype(F32), gq_ref[...], lane < HEAD_DIM) * (HEAD_DIM ** -0.5)
    cum_q = cum_ref[0, pl.ds(pl.multiple_of(i * tq, tq), tq), :]
    qa = [_augment(qn, _lane_col(cum_q, lane, MISC_F + 2 * hp + h), lane, h, False).astype(BF16)
          for h in range(2)]

    def scores(h, off):
        return _nt_dot(ka_ref[pl.ds(off, tk), h * LANES:(h + 1) * LANES], qa[h])

    def values_t(h, off):
        return vt_ref[h * HEAD_DIM:(h + 1) * HEAD_DIM, pl.ds(off, tk)]

    for c in range(s // tq):
        @pl.when(i == c)
        def _(c=c):
            o_ref[0] = _causal_flash_static(c, tq, tk, scores, values_t).T.astype(o_ref.dtype)


def fox_attention(y3, cum, gq, gk, *, tq=256, tk=256):
    b, s, _ = y3.shape
    qb, kb, vb = COL_FOX // LANES, (COL_FOX + MIX_W) // LANES, (COL_FOX + 2 * MIX_W) // LANES
    return pl.pallas_call(
        functools.partial(_fox_kernel, tq=tq, tk=tk),
        grid=(b, 2, s // tq),
        in_specs=[pl.BlockSpec((1, tq, LANES), lambda bi, hp, i: (bi, i, qb + hp)),
                  pl.BlockSpec((1, s, LANES), lambda bi, hp, i: (bi, 0, kb + hp)),
                  pl.BlockSpec((1, s, LANES), lambda bi, hp, i: (bi, 0, vb + hp)),
                  pl.BlockSpec((1, s, LANES), lambda bi, hp, i: (bi, 0, 0)),
                  pl.BlockSpec((1, LANES), lambda bi, hp, i: (0, 0)),
                  pl.BlockSpec((1, LANES), lambda bi, hp, i: (0, 0))],
        out_specs=pl.BlockSpec((1, tq, LANES), lambda bi, hp, i: (bi, i, hp)),
        out_shape=jax.ShapeDtypeStruct((b, s, MIX_W), BF16),
        scratch_shapes=[pltpu.VMEM((s, 2 * LANES), BF16), pltpu.VMEM((LANES, s), BF16)],
        compiler_params=_cparams("parallel", "parallel", "arbitrary"),
    )(y3, y3, y3, cum, gq, gk)


def _mla_prep_kernel(cq_ref, ckv_ref, misc_ref, gcq_ref, gckv_ref, wuq_ref, wuk_ref, wuv_ref,
                     qg_ref, kg_ref, cos_ref, sa_ref, sb_ref, q_out, k_out, v_out, *, ts):
    hq = _rms_rows(cq_ref[0], gcq_ref[...]).astype(BF16)
    hkv = _rms_rows(ckv_ref[0], gckv_ref[...]).astype(BF16)
    q = _dot(hq, wuq_ref[...])
    kn = _dot(hkv, wuk_ref[...])
    v_out[0] = _nt_dot(wuv_ref[...], hkv).astype(v_out.dtype)

    lane = lax.broadcasted_iota(jnp.int32, (ts, LANES), 1)
    misc = misc_ref[0]
    kr = pltpu.roll(jnp.where((lane >= MISC_KR) & (lane < MISC_KR + MLA_ROPE), misc, 0.0),
                    MLA_NOPE - MISC_KR, 1)
    cos, sa, sb = cos_ref[...], sa_ref[...], sb_ref[...]
    half = MLA_ROPE // 2

    def norm_rope(t, g):
        t = t * lax.rsqrt(jnp.sum(t * t, axis=-1, keepdims=True) / MLA_QK + EPS) * g
        return t * cos + pltpu.roll(t, LANES - half, 1) * sa + pltpu.roll(t, half, 1) * sb

    for h in range(N_HEADS):
        sl = slice(h * LANES, (h + 1) * LANES)
        q_out[0, :, sl] = (norm_rope(q[:, sl], qg_ref[...]) * (MLA_QK ** -0.5)).astype(q_out.dtype)
        k_out[0, :, sl] = norm_rope(kn[:, sl] + kr, kg_ref[...]).astype(k_out.dtype)


def mla_prep(y3, gcq, gckv, wuq, wuk, wuv, qg, kg, cos, sa, sb, *, ts=512):
    b, s, _ = y3.shape
    const = lambda shape: pl.BlockSpec(shape, lambda bi, i: (0,) * len(shape))
    return pl.pallas_call(
        functools.partial(_mla_prep_kernel, ts=ts),
        grid=(b, s // ts),
        in_specs=[pl.BlockSpec((1, ts, MLA_Q_LORA), lambda bi, i: (bi, i, COL_CQ // MLA_Q_LORA)),
                  pl.BlockSpec((1, ts, LANES), lambda bi, i: (bi, i, COL_CKV // LANES)),
                  pl.BlockSpec((1, ts, LANES), lambda bi, i: (bi, i, COL_MISC // LANES)),
                  const((1, MLA_Q_LORA)), const((1, MLA_KV_LORA)),
                  const((MLA_Q_LORA, N_HEADS * LANES)), const((MLA_KV_LORA, N_HEADS * LANES)),
                  const((MIX_W, MLA_KV_LORA)), const((1, LANES)), const((1, LANES)),
                  pl.BlockSpec((ts, LANES), lambda bi, i: (i, 0)),
                  pl.BlockSpec((ts, LANES), lambda bi, i: (i, 0)),
                  pl.BlockSpec((ts, LANES), lambda bi, i: (i, 0))],
        out_specs=[pl.BlockSpec((1, ts, N_HEADS * LANES), lambda bi, i: (bi, i, 0)),
                   pl.BlockSpec((1, ts, N_HEADS * LANES), lambda bi, i: (bi, i, 0)),
                   pl.BlockSpec((1, MIX_W, ts), lambda bi, i: (bi, 0, i))],
        out_shape=[jax.ShapeDtypeStruct((b, s, N_HEADS * LANES), BF16),
                   jax.ShapeDtypeStruct((b, s, N_HEADS * LANES), BF16),
                   jax.ShapeDtypeStruct((b, MIX_W, s), BF16)],
        compiler_params=_cparams("parallel", "parallel"),
    )(y3, y3, y3, gcq, gckv, wuq, wuk, wuv, qg, kg, cos, sa, sb)


def _mla_kernel(q_ref, k_ref, vt_ref, o_ref, *, tq, tk):
    i = pl.program_id(2)
    qh = [q_ref[0, :, h * LANES:(h + 1) * LANES] for h in range(2)]

    def scores(h, off):
        return _nt_dot(k_ref[0, pl.ds(off, tk), h * LANES:(h + 1) * LANES], qh[h])

    def values_t(h, off):
        return vt_ref[0, h * HEAD_DIM:(h + 1) * HEAD_DIM, pl.ds(off, tk)]

    for c in range(k_ref.shape[1] // tq):
        @pl.when(i == c)
        def _(c=c):
            o_ref[0] = _causal_flash_static(c, tq, tk, scores, values_t).T.astype(o_ref.dtype)


def mla_attention(q, k, vt, *, tq=256, tk=256):
    b, s, _ = q.shape
    return pl.pallas_call(
        functools.partial(_mla_kernel, tq=tq, tk=tk),
        grid=(b, 2, s // tq),
        in_specs=[pl.BlockSpec((1, tq, 2 * LANES), lambda bi, hp, i: (bi, i, hp)),
                  pl.BlockSpec((1, s, 2 * LANES), lambda bi, hp, i: (bi, 0, hp)),
                  pl.BlockSpec((1, LANES, s), lambda bi, hp, i: (bi, hp, 0))],
        out_specs=pl.BlockSpec((1, tq, LANES), lambda bi, hp, i: (bi, i, hp)),
        out_shape=jax.ShapeDtypeStruct((b, s, MIX_W), BF16),
        compiler_params=_cparams("parallel", "parallel", "arbitrary"),
    )(q, k, vt)


def _rel_bias_tile(dist, tab_ref, h):
    max_exact = REL_BUCKETS // 2
    d = jnp.maximum(dist, 0)
    large = max_exact + (jnp.log(jnp.maximum(d, 1).astype(F32) / max_exact)
                         / math.log(REL_MAX_DIST / max_exact)
                         * (REL_BUCKETS - max_exact)).astype(jnp.int32)
    large = jnp.minimum(large, REL_BUCKETS - 1)
    bucket = jnp.where(d < max_exact, d, large)
    out = jnp.zeros(dist.shape, F32)
    for bkt in range(REL_BUCKETS):
        out = jnp.where(bucket == bkt, tab_ref[bkt, h], out)
    return out


def _band_bias_kernel(tab_ref, slc_ref, win_ref):
    r = pl.program_id(0)
    kj = r * LANES + lax.broadcasted_iota(jnp.int32, (LANES, LANES), 0)
    qi = lax.broadcasted_iota(jnp.int32, (LANES, LANES), 1)
    dist = qi + NSA_WINDOW - kj
    for h in range(N_HEADS):
        delta = _rel_bias_tile(dist, tab_ref, h) - tab_ref[REL_BUCKETS - 1, h]
        slc_ref[h] = jnp.where(dist >= 0, delta, MASKED)
        win_ref[h] = jnp.where((dist >= 0) & (dist < NSA_WINDOW), delta, MASKED)


def _cmp_bias_kernel(tab_ref, o_ref):
    i = pl.program_id(0)
    c = lax.broadcasted_iota(jnp.int32, (LANES, LANES), 0)
    s = i * LANES + lax.broadcasted_iota(jnp.int32, (LANES, LANES), 1)
    dist = s - (c * NSA_CMP_STRIDE + NSA_CMP_LEN - 1)
    for h in range(N_HEADS):
        o_ref[h] = jnp.where(dist >= 0, _rel_bias_tile(dist, tab_ref, h), MASKED)


def rel_bias_tables(rel_bias, s):
    assert REL_BUCKETS == 32 and REL_MAX_DIST == 128 and NSA_WINDOW >= 113
    band = NSA_WINDOW + LANES
    smem = pl.BlockSpec(memory_space=pltpu.SMEM)
    band_spec = pl.BlockSpec((N_HEADS, LANES, LANES), lambda r: (0, r, 0))
    band_shape = jax.ShapeDtypeStruct((N_HEADS, band, LANES), F32)
    band_slc, band_win = pl.pallas_call(
        _band_bias_kernel, grid=(band // LANES,), in_specs=[smem],
        out_specs=[band_spec, band_spec], out_shape=[band_shape, band_shape],
    )(rel_bias)
    cmpb = pl.pallas_call(
        _cmp_bias_kernel, grid=(s // LANES,), in_specs=[smem],
        out_specs=pl.BlockSpec((N_HEADS, LANES, LANES), lambda i: (0, 0, i)),
        out_shape=jax.ShapeDtypeStruct((N_HEADS, LANES, s), F32),
    )(rel_bias)
    return band_slc, band_win, cmpb


def _nsa_compress_kernel(kin_ref, vin_ref, pek_ref, pev_ref, w1k_ref, w1v_ref, w2k_ref, w2v_ref,
                         gk_ref, kc_ref, vct_ref):
    half = NSA_CMP_STRIDE * HEAD_DIM

    def hidden(x, pe_ref, w1_ref):
        a = _dot((x + pe_ref[:, :half]).astype(BF16), w1_ref[:half, :])
        bh = _dot((x + pe_ref[:, half:]).astype(BF16), w1_ref[half:, :])
        pre = a + pltpu.roll(bh, LANES - 1, 0)
        return (pre * _sigmoid(pre)).astype(BF16)

    kc = _dot(hidden(kin_ref[0], pek_ref, w1k_ref), w2k_ref[...])
    kc = kc * lax.rsqrt(jnp.sum(kc * kc, axis=-1, keepdims=True) / HEAD_DIM + EPS) * gk_ref[...]
    kc_ref[0] = kc.astype(kc_ref.dtype)
    vct_ref[0] = _nt_dot(w2v_ref[...], hidden(vin_ref[0], pev_ref, w1v_ref)).astype(vct_ref.dtype)


def nsa_compress(kin, vin, pek, pev, w1k, w1v, w2k, w2v, gk):
    b = kin.shape[0]
    const = lambda shape: pl.BlockSpec(shape, lambda bi: (0,) * len(shape))
    blk = pl.BlockSpec((1,) + kin.shape[1:], lambda bi: (bi, 0, 0))
    out = pl.BlockSpec((1, LANES, LANES), lambda bi: (bi, 0, 0))
    return pl.pallas_call(
        _nsa_compress_kernel, grid=(b,),
        in_specs=[blk, blk, const(pek.shape), const(pev.shape), const(w1k.shape), const(w1v.shape),
                  const(w2k.shape), const(w2v.shape), const(gk.shape)],
        out_specs=[out, out],
        out_shape=[jax.ShapeDtypeStruct((b, LANES, LANES), BF16)] * 2,
        compiler_params=_cparams("parallel"),
    )(kin, vin, pek, pev, w1k, w1v, w2k, w2v, gk)


def _nsa_kernel(q_ref, k2_ref, v2_ref, misc_ref, kc_ref, vct_ref, bslc_ref, bwin_ref, cmpb_ref,
                gq_ref, gk2_ref, o_ref, ks_ref, kw_ref, vt_ref, win_ref, stage_ref):
    tq = LANES
    s = k2_ref.shape[1]
    pad = NSA_WINDOW
    band = NSA_WINDOW + tq
    n_sel = s // NSA_SEL_LEN
    assert NSA_SEL_LANE0 + n_sel <= LANES and pad % NSA_FAR_TILE == 0
    i = pl.program_id(1)

    @pl.when(i == 0)
    def _():
        lane_s = lax.broadcasted_iota(jnp.int32, (s, LANES), 1)
        row_s = lax.broadcasted_iota(jnp.int32, (s, LANES), 0)
        kn = _pair_rms(k2_ref[0], gk2_ref[...], lane_s < HEAD_DIM)
        sel_lane = NSA_SEL_LANE0 + (row_s >> 6)
        ks_ref[pad:, :] = jnp.where(lane_s < HEAD_DIM, kn,
                                    jnp.where(lane_s == sel_lane, 1.0, 0.0)).astype(BF16)
        kw_ref[pad:, :] = jnp.where(lane_s < HEAD_DIM, pltpu.roll(kn, HEAD_DIM, 1),
                                    0.0).astype(BF16)
        lane_p = lax.broadcasted_iota(jnp.int32, (pad, LANES), 1)
        before = jnp.where(lane_p == NSA_PAD_LANE, MASKED, 0.0).astype(BF16)
        ks_ref[:pad, :] = before
        kw_ref[:pad, :] = before
        vt_ref[:, pad:] = v2_ref[0].T.astype(BF16)
        vt_ref[:, :pad] = jnp.zeros((LANES, pad), BF16)

    lane = lax.broadcasted_iota(jnp.int32, (tq, LANES), 1)
    lo = lane < HEAD_DIM
    scale = HEAD_DIM ** -0.5
    q = q_ref[0]
    q_base = []
    for pair in range(2):
        cols = slice(pair * LANES, (pair + 1) * LANES)
        pn = _pair_rms(q[:, cols], gq_ref[:, cols], lo) * scale
        for head in (jnp.where(lo, pn, 0.0), pltpu.roll(jnp.where(lo, 0.0, pn), HEAD_DIM, 1)):
            q_base.append(jnp.where(lane == NSA_PAD_LANE, 1.0, head))
    q_plain = [x.astype(BF16) for x in q_base]

    b0 = pl.multiple_of(i * tq, tq)
    ks_band = ks_ref[pl.ds(b0, band), :]
    kw_band = kw_ref[pl.ds(b0, band), :]
    vs_band = vt_ref[:HEAD_DIM, pl.ds(b0, band)]
    vw_band = vt_ref[HEAD_DIM:, pl.ds(b0, band)]

    kc = kc_ref[0]
    vct = vct_ref[0, :HEAD_DIM, :]
    s_cmp = [_nt_dot(kc, q_plain[h]) + cmpb_ref[h] for h in range(N_HEADS)]
    for h in range(N_HEADS):
        win_ref[h] = _nt_dot(kw_band, q_plain[h])

    def window(h):
        _, l_w, a_w = _first_t(win_ref[h] + bwin_ref[h], vw_band)
        return a_w / l_w

    o_cmp = []
    p_sum = jnp.zeros((LANES, tq), F32)
    for h in range(N_HEADS):
        s_c = s_cmp[h]
        e_c = jnp.where(s_c > 0.5 * MASKED,
                        jnp.exp(s_c - jnp.max(s_c, axis=0, keepdims=True)), 0.0)
        den = jnp.sum(e_c, axis=0, keepdims=True)
        p_c = e_c / jnp.where(den > 0.0, den, 1.0)
        o_cmp.append(_dot(vct, p_c.astype(BF16)))
        p_sum = p_sum + p_c

    jj = lax.broadcasted_iota(jnp.int32, (LANES, LANES), 0)
    c0 = lax.broadcasted_iota(jnp.int32, (LANES, LANES), 1) * NSA_CMP_STRIDE
    j0 = jj * NSA_SEL_LEN
    overlap = jnp.where((c0 < j0 + NSA_SEL_LEN) & (c0 + NSA_CMP_LEN > j0), 1.0, 0.0).astype(BF16)
    hi, mid, lo3 = _split3(p_sum)
    imp = (_dot(overlap, hi) + _dot(overlap, mid) + _dot(overlap, lo3))[0:n_sel]
    o_win = [window(0), window(1)]
    jj32 =lax.broadcasted_iota(jnp.int32, (n_sel, tq), 0)
    cur = (i * tq + lax.broadcasted_iota(jnp.int32, (n_sel, tq), 1)) >> 6
    forced = (jj32 == 0) | (jj32 == cur) | (jj32 == cur - 1)
    imp = jnp.where(forced, FORCE, imp)
    imp = jnp.where(jj32 <= cur, imp, -FORCE)
    cnt = jnp.zeros((n_sel, tq), F32)
    for jp in range(n_sel):
        other = imp[jp:jp + 1, :]
        beats = (other > imp) | ((other == imp) & (jj32 > jp))
        cnt = cnt + jnp.where(beats, 1.0, 0.0)
    sel_neg = jnp.where(cnt < float(NSA_TOP_N), 0.0, MASKED)
    sel_neg = jnp.concatenate([sel_neg, jnp.zeros((LANES - n_sel, tq), F32)], axis=0).T
    sel_neg = pltpu.roll(sel_neg, NSA_SEL_LANE0, 1)
    in_sel = (lane >= NSA_SEL_LANE0) & (lane < NSA_SEL_LANE0 + n_sel)
    band_block0 = NSA_SEL_LANE0 + ((i * tq - pad) >> 6)
    q_band = [jnp.where(in_sel, sel_neg, x).astype(BF16) for x in q_base]
    q_far = [jnp.where(in_sel, jnp.where(lane >= band_block0, MASKED, sel_neg), x).astype(BF16)
             for x in q_base]

    stage_ref[0] = _nt_dot(ks_band, q_band[0])
    stage_ref[1] = _nt_dot(ks_band, q_band[1])
    o_win += [window(2), window(3)]
    slc = []
    for h in range(N_HEADS):
        slc.append(_first_t(stage_ref[h % 2] + bslc_ref[h], vs_band))
        if h + 2 < N_HEADS:
            stage_ref[h % 2] = _nt_dot(ks_band, q_band[h + 2])

    n_far = (jnp.maximum(i * tq - pad, 0) + NSA_FAR_TILE - 1) // NSA_FAR_TILE

    def far_tile(kt, state):
        off = pad + kt * NSA_FAR_TILE
        k_far = ks_ref[pl.ds(off, NSA_FAR_TILE), :]
        v_far = vt_ref[:HEAD_DIM, pl.ds(off, NSA_FAR_TILE)]
        sc = [_nt_dot(k_far, q_far[0]), _nt_dot(k_far, q_far[1])]
        out = []
        for h in range(N_HEADS):
            if h + 2 < N_HEADS:
                sc.append(_nt_dot(k_far, q_far[h + 2]))
            out.append(_update_t(state[h], sc[h], v_far, None))
        return out

    g_t = _sigmoid(misc_ref[0]).T

    def finish(state):
        heads = []
        for h in range(N_HEADS):
            _, l_s, a_s = state[h]
            row = lambda n: g_t[MISC_G + n * N_HEADS + h:MISC_G + n * N_HEADS + h + 1, :]
            heads.append(row(0) * o_cmp[h] + row(1) * (a_s / l_s) + row(2) * o_win[h])
        o_ref[0] = jnp.concatenate(heads, axis=0).T.astype(o_ref.dtype)

    for c in range((s - tq - pad + NSA_FAR_TILE - 1) // NSA_FAR_TILE + 1):
        @pl.when(n_far == c)
        def _(c=c):
            state = slc
            for kt in range(c):
                state = far_tile(kt, state)
            finish(state)


def nsa_attention(y3, kc, vct, band_slc, band_win, cmpb, gq, gk2):
    b, s, _ = y3.shape
    tq = LANES
    band = NSA_WINDOW + tq
    return pl.pallas_call(
        _nsa_kernel,
        grid=(b, s // tq),
        in_specs=[pl.BlockSpec((1, tq, MIX_W), lambda bi, i: (bi, i, COL_NSAQ // MIX_W)),
                  pl.BlockSpec((1, s, LANES), lambda bi, i: (bi, 0, COL_K2 // LANES)),
                  pl.BlockSpec((1, s, LANES), lambda bi, i: (bi, 0, COL_V2 // LANES)),
                  pl.BlockSpec((1, tq, LANES), lambda bi, i: (bi, i, COL_MISC // LANES)),
                  pl.BlockSpec((1, LANES, LANES), lambda bi, i: (bi, 0, 0)),
                  pl.BlockSpec((1, LANES, LANES), lambda bi, i: (bi, 0, 0)),
                  pl.BlockSpec(band_slc.shape, lambda bi, i: (0, 0, 0)),
                  pl.BlockSpec(band_win.shape, lambda bi, i: (0, 0, 0)),
                  pl.BlockSpec((N_HEADS, LANES, tq), lambda bi, i: (0, 0, i)),
                  pl.BlockSpec((1, MIX_W), lambda bi, i: (0, 0)),
                  pl.BlockSpec((1, LANES), lambda bi, i: (0, 0))],
        out_specs=pl.BlockSpec((1, tq, MIX_W), lambda bi, i: (bi, i, 0)),
        out_shape=jax.ShapeDtypeStruct((b, s, MIX_W), BF16),
        scratch_shapes=[pltpu.VMEM((s + NSA_WINDOW, LANES), BF16),
                        pltpu.VMEM((s + NSA_WINDOW, LANES), BF16),
                        pltpu.VMEM((LANES, s + NSA_WINDOW), BF16),
                        pltpu.VMEM((N_HEADS, band, tq), F32),
                        pltpu.VMEM((2, band, tq), F32)],
        compiler_params=_cparams("parallel", "arbitrary"),
    )(y3, y3, y3, y3, kc, vct, band_slc, band_win, cmpb, gq, gk2)


def _merge_kernel(ysb_ref, ymla_ref, ynsa_ref, yfox_ref, g0_ref, g1_ref, g2_ref, g3_ref,
                  wb_ref, wo_ref, x_ref, o_ref):
    u = None
    for n, (y_ref, g_ref) in enumerate(((ysb_ref, g0_ref), (ymla_ref, g1_ref),
                                        (ynsa_ref, g2_ref), (yfox_ref, g3_ref))):
        term = _sigmoid(g_ref[...].astype(F32)) * _dot(y_ref[...], wb_ref[n])
        u = term if u is None else u + term
    o_ref[...] = x_ref[...] + _dot(u.astype(BF16), wo_ref[...])


def merge_branches(ys, y, wb, wo, x, *, tm=512):
    t, d = x.shape
    gate_blk = COL_GATE // d
    yspec = pl.BlockSpec((tm, MIX_W), lambda i: (i, 0))
    gspecs = [pl.BlockSpec((tm, d), lambda i, n=n: (i, gate_blk + n)) for n in range(N_BRANCH)]
    return pl.pallas_call(
        _merge_kernel, grid=(t // tm,),
        in_specs=[yspec] * 4 + gspecs + [pl.BlockSpec(wb.shape, lambda i: (0, 0, 0)),
                                         pl.BlockSpec(wo.shape, lambda i: (0, 0)),
                                         pl.BlockSpec((tm, d), lambda i: (i, 0))],
        out_specs=pl.BlockSpec((tm, d), lambda i: (i, 0)),
        out_shape=jax.ShapeDtypeStruct((t, d), F32),
        compiler_params=_cparams("parallel"),
    )(*ys, y, y, y, y, wb, wo, x)


def _mlp_ple_kernel(x_ref, g_ref, wu_ref, wd_ref, gp_ref, wg_ref, p_ref, wp_ref, o_ref,
                    h_ref, acc_ref):
    f = pl.program_id(1)

    @pl.when(f == 0)
    def _():
        h_ref[...] = _rms_rows(x_ref[...], g_ref[...]).astype(BF16)
        acc_ref[...] = jnp.zeros_like(acc_ref)

    a = jnp.maximum(_dot(h_ref[...], wu_ref[...]), 0.0)
    acc_ref[...] += _dot((a * a).astype(BF16), wd_ref[...])

    @pl.when(f == pl.num_programs(1) - 1)
    def _():
        x1 = x_ref[...] + acc_ref[...]
        gate = _sigmoid(_dot(_rms_rows(x1, gp_ref[...]).astype(BF16), wg_ref[...]))
        o_ref[...] = x1 + gate * _dot(p_ref[...].astype(BF16), wp_ref[...])


def mlp_ple(x, g, wu, wd, gp, wg, p_all, layer, wp, *, tm=1024, tf=1024):
    t, d = x.shape
    ff = wu.shape[1]
    return pl.pallas_call(
        _mlp_ple_kernel, grid=(t // tm, ff // tf),
        in_specs=[pl.BlockSpec((tm, d), lambda i, f: (i, 0)),
                  pl.BlockSpec((1, d), lambda i, f: (0, 0)),
                  pl.BlockSpec((d, tf), lambda i, f: (0, f)),
                  pl.BlockSpec((tf, d), lambda i, f: (f, 0)),
                  pl.BlockSpec((1, d), lambda i, f: (0, 0)),
                  pl.BlockSpec((d, d), lambda i, f: (0, 0)),
                  pl.BlockSpec((None, tm, PLE_DIM), lambda i, f: (layer, i, 0)),
                  pl.BlockSpec((PLE_DIM, d), lambda i, f: (0, 0))],
        out_specs=pl.BlockSpec((tm, d), lambda i, f: (i, 0)),
        out_shape=jax.ShapeDtypeStruct((t, d), F32),
        scratch_shapes=[pltpu.VMEM((tm, d), BF16), pltpu.VMEM((tm, d), F32)],
        compiler_params=_cparams("parallel", "arbitrary"),
    )(x, g.reshape(1, d), wu, wd, gp.reshape(1, d), wg, p_all, wp)


def _pack_w_in(w):
    offs = np.concatenate([[0], np.cumsum(IN_WIDTHS)]).tolist()
    (sb_q, sb_k, sb_v, cq, ckv, kr, nsa_q, kc, vc, ks, vs, kw, vw, ng,
     fox_q, fox_k, fox_v, ff, gate) = [w[:, offs[n]:offs[n + 1]] for n in range(len(IN_WIDTHS))]
    d = w.shape[0]
    z = lambda n: jnp.zeros((d, n), w.dtype)
    misc = jnp.concatenate([ff, ng, z(MISC_KR - MISC_G - 3 * N_HEADS), kr,
                            z(LANES - MISC_KR - MLA_ROPE)], axis=1)
    w_f32_part = jnp.concatenate([cq, ckv, ks, kw, vs, vw, kc, vc, misc, nsa_q], axis=1)
    w_bf16_part = jnp.concatenate([gate, sb_q, sb_k, sb_v, fox_q, fox_k, fox_v], axis=1)
    return w_f32_part.astype(BF16), w_bf16_part.astype(BF16)


def _head_slots(w, width):
    k = w.shape[0]
    w = w.reshape(k, N_HEADS, width)
    return jnp.pad(w, ((0, 0), (0, 0), (0, LANES - width))).reshape(k, N_HEADS * LANES)


def _rope_tables(s):
    half = MLA_ROPE // 2
    inv = jnp.exp(-math.log(ROPE_THETA) * jnp.arange(half, dtype=F32) / half)
    ang = jnp.arange(s, dtype=F32)[:, None] * inv[None, :]
    cos, sin = jnp.cos(ang), jnp.sin(ang)
    ones = jnp.ones((s, MLA_NOPE), F32)
    zeros = lambda n: jnp.zeros((s, n), F32)
    tail = LANES - MLA_QK
    cos_t = jnp.concatenate([ones, cos, cos, jnp.ones((s, tail), F32)], axis=1)
    sa_t = jnp.concatenate([zeros(MLA_NOPE), -sin, zeros(half), zeros(tail)], axis=1)
    sb_t = jnp.concatenate([zeros(MLA_NOPE), zeros(half), sin, zeros(tail)], axis=1)
    return cos_t, sa_t, sb_t


def _pad_lanes(v, left=0):
    v = v.reshape(1, -1)
    return jnp.pad(v, ((0, 0), (left, LANES - left - v.shape[1])))


def _pad_to_lanes(w):
    return jnp.pad(w, ((0, 0), (0, LANES - w.shape[1])))


def kernel(x, p, rel_bias, norm_mix_g, w_in, mla_cq_norm_g, mla_ckv_norm_g, mla_w_uq, mla_w_ukv,
           mla_qn_g, mla_kn_g, nsa_pe_k, nsa_pe_v, nsa_w1_k, nsa_w2_k, nsa_w1_v, nsa_w2_v,
           nsa_qn_g, nsa_kn_g, fox_f_bias, fox_qn_g, fox_kn_g, w_branch, w_o, norm_mlp_g,
           w_mlp_up, w_mlp_down, norm_ple_g, w_ple_gate, w_ple_proj):
    b, s, d = x.shape
    t = b * s
    xf = x.reshape(t, d)
    p_all = p.reshape(DEPTH, t, PLE_DIM)
    band_slc, band_win, cmpb = rel_bias_tables(rel_bias.astype(F32), s)
    cos_t, sa_t, sb_t = _rope_tables(s)
    n_cmp_in = NSA_CMP_STRIDE * HEAD_DIM

    for i in range(DEPTH):
        w_a, w_b = _pack_w_in(w_in[i])
        ya, h = norm_proj(xf, norm_mix_g[i], w_a)
        yb = matmul_bf16(h, w_b)
        y3 = ya.reshape(b, s, N_F32)
        yb3 = yb.reshape(b, s, N_BF16)

        y_sb = sb_attention(yb3)

        wukv = mla_w_ukv[i].reshape(MLA_KV_LORA, N_HEADS, MLA_NOPE + MLA_V)
        q_m, k_m, vt_m = mla_prep(
            y3, mla_cq_norm_g[i].reshape(1, -1), mla_ckv_norm_g[i].reshape(1, -1),
            _head_slots(mla_w_uq[i], MLA_QK).astype(BF16),
            _head_slots(wukv[:, :, :MLA_NOPE].reshape(MLA_KV_LORA, -1), MLA_NOPE).astype(BF16),
            wukv[:, :, MLA_NOPE:].reshape(MLA_KV_LORA, -1).T.astype(BF16),
            _pad_lanes(mla_qn_g[i]), _pad_lanes(mla_kn_g[i]), cos_t, sa_t, sb_t)
        y_mla = mla_attention(q_m, k_m, vt_m)

        kin = y3[:, :, COL_KCVC:COL_KCVC + HEAD_DIM].reshape(b, s // NSA_CMP_STRIDE, n_cmp_in)
        vin = y3[:, :, COL_KCVC + HEAD_DIM:COL_KCVC + 2 * HEAD_DIM].reshape(
            b, s // NSA_CMP_STRIDE, n_cmp_in)
        pad_sq = lambda w: jnp.pad(_pad_to_lanes(w), ((0, LANES - w.shape[0]), (0, 0)))
        kc, vct = nsa_compress(
            kin, vin, nsa_pe_k[i].reshape(1, -1), nsa_pe_v[i].reshape(1, -1),
            _pad_to_lanes(nsa_w1_k[i]).astype(BF16), _pad_to_lanes(nsa_w1_v[i]).astype(BF16),
            pad_sq(nsa_w2_k[i]).astype(BF16), pad_sq(nsa_w2_v[i].T).astype(BF16),
            _pad_lanes(nsa_kn_g[i, 0]))
        y_nsa = nsa_attention(
            y3, kc, vct, band_slc, band_win, cmpb,
            jnp.tile(nsa_qn_g[i], N_HEADS).reshape(1, -1),
            jnp.concatenate([nsa_kn_g[i, 1], nsa_kn_g[i, 2]]).reshape(1, -1))

        cum = fox_cum(y3, _pad_lanes(fox_f_bias[i], MISC_F))
        y_fox = fox_attention(yb3, cum, jnp.tile(fox_qn_g[i], 2).reshape(1, -1),
                              jnp.tile(fox_kn_g[i], 2).reshape(1, -1))

        ys = [a.reshape(t, MIX_W) for a in (y_sb, y_mla, y_nsa, y_fox)]
        xf = merge_branches(ys, yb, w_branch[i].astype(BF16), w_o[i].astype(BF16), xf)
        xf = mlp_ple(xf, norm_mlp_g[i], w_mlp_up[i].astype(BF16), w_mlp_down[i].astype(BF16),
                     norm_ple_g[i], w_ple_gate[i].astype(BF16), p_all, i,
                     w_ple_proj[i].astype(BF16))
    return xf.reshape(b, s, d)
```

```python
import functools
import math

import numpy as np
import jax
import jax.numpy as jnp
from jax import lax
from jax.experimental import pallas as pl
from jax.experimental.pallas import tpu as pltpu

F32 = jnp.float32
BF16 = jnp.bfloat16

D_MODEL = 1024
DEPTH = 4
HEAD_DIM = 64
N_HEADS = 4
MIX_W = N_HEADS * HEAD_DIM
N_BRANCH = 4
EPS = 1e-6
FORCE = 1e9
MLA_Q_LORA = 384
MLA_KV_LORA = 128
MLA_NOPE = 64
MLA_ROPE = 32
MLA_V = 64
MLA_QK = MLA_NOPE + MLA_ROPE
ROPE_THETA = 10000.0
NSA_CMP_LEN = 32
NSA_CMP_STRIDE = 16
NSA_SEL_LEN = 64
NSA_TOP_N = 16
NSA_WINDOW = 512
REL_BUCKETS = 32
REL_MAX_DIST = 128
D_FF = 4 * D_MODEL
PLE_DIM = 256

LANES = 128
MASKED = -1e30
VMEM_LIMIT = 56 * 1024 * 1024

IN_WIDTHS = ((MIX_W,) * 3
             + (MLA_Q_LORA, MLA_KV_LORA, MLA_ROPE)
             + (MIX_W,) + (HEAD_DIM,) * 6 + (3 * N_HEADS,)
             + (MIX_W,) * 3 + (N_HEADS,)
             + (N_BRANCH * D_MODEL,))

COL_CQ = 0
COL_CKV = 384
COL_K2 = 512
COL_V2 = 640
COL_KCVC = 768
COL_MISC = 896
COL_NSAQ = 1024
N_F32 = 1280
COL_GATE = 0
COL_SB = 4096
COL_FOX = 4864
N_BF16 = 5632
MISC_F = 0
MISC_G = 4
MISC_KR = 32

NSA_PAD_LANE = HEAD_DIM
NSA_SEL_LANE0 = HEAD_DIM + 1
NSA_FAR_TILE = 512


def _cparams(*sem):
    return pltpu.CompilerParams(dimension_semantics=sem, vmem_limit_bytes=VMEM_LIMIT)


def _nt_dot(a, b):
    return lax.dot_general(a, b, (((1,), (1,)), ((), ())), preferred_element_type=F32)


def _dot(a, b):
    return jnp.dot(a, b, preferred_element_type=F32)


def _split3(x):
    hi = x.astype(BF16)
    r1 = x - hi.astype(F32)
    mid = r1.astype(BF16)
    lo = (r1 - mid.astype(F32)).astype(BF16)
    return hi, mid, lo


def _softplus(z):
    return jnp.maximum(z, 0.0) + jnp.log(1.0 + jnp.exp(-jnp.abs(z)))


def _sigmoid(z):
    return 1.0 / (1.0 + jnp.exp(-z))


def _rms_rows(x, g):
    r = lax.rsqrt(jnp.mean(x * x, axis=-1, keepdims=True) + EPS)
    return x * r * g


def _pair_rms(x, g, lo):
    x2 = x * x
    s0 = jnp.sum(jnp.where(lo, x2, 0.0), axis=-1, keepdims=True)
    s1 = jnp.sum(jnp.where(lo, 0.0, x2), axis=-1, keepdims=True)
    r = jnp.where(lo, lax.rsqrt(s0 / HEAD_DIM + EPS), lax.rsqrt(s1 / HEAD_DIM + EPS))
    return x * r * g


def _lane_col(x, lane, idx):
    return jnp.sum(jnp.where(lane == idx, x, 0.0), axis=-1, keepdims=True)


def _first_t(s, vt):
    m = jnp.max(s, axis=0, keepdims=True)
    p = jnp.exp(s - m)
    return m, jnp.sum(p, axis=0, keepdims=True), _dot(vt, p.astype(BF16))


def _update_t(carry, s, vt, mask):
    m, l, acc = carry
    if mask is not None:
        s = jnp.where(mask, s, MASKED)
    m_new = jnp.maximum(m, jnp.max(s, axis=0, keepdims=True))
    p = jnp.exp(s - m_new)
    if mask is not None:
        p = jnp.where(mask, p, 0.0)
    alpha = jnp.exp(m - m_new)
    l = alpha * l + jnp.sum(p, axis=0, keepdims=True)
    acc = alpha * acc + _dot(vt, p.astype(BF16))
    return m_new, l, acc


def _init_t(tq):
    return (jnp.full((1, tq), MASKED, F32), jnp.zeros((1, tq), F32),
            jnp.zeros((HEAD_DIM, tq), F32))


def _causal_flash_t(i, tq, tk, score_fn, vt_fn):
    assert tk % tq == 0
    diff = (lax.broadcasted_iota(jnp.int32, (tk, tq), 0)
            - lax.broadcasted_iota(jnp.int32, (tk, tq), 1))
    n_full = (i * tq) // tk

    def scores_at(kt):
        off = pl.multiple_of(kt * tk, tk)
        return tuple(score_fn(h, off) for h in range(2))

    def body(kt, carry):
        cur, state = carry
        nxt = scores_at(kt + 1)
        off = pl.multiple_of(kt * tk, tk)
        state = tuple(_update_t(state[h], cur[h], vt_fn(h, off), None) for h in range(2))
        return nxt, state

    carry = (scores_at(0), (_init_t(tq), _init_t(tq)))
    cur, state = lax.fori_loop(0, n_full, body, carry)
    off = pl.multiple_of(n_full * tk, tk)
    mask = diff <= i * tq - off
    state = tuple(_update_t(state[h], cur[h], vt_fn(h, off), mask) for h in range(2))
    return jnp.concatenate([acc / l for (_, l, acc) in state], axis=0)


def _causal_flash_static(c, tq, tk, score_fn, vt_fn):
    assert tk % tq == 0
    diff = (lax.broadcasted_iota(jnp.int32, (tk, tq), 0)
            - lax.broadcasted_iota(jnp.int32, (tk, tq), 1))
    n_full = (c * tq) // tk
    cur = [score_fn(h, 0) for h in range(2)]
    state = [_init_t(tq), _init_t(tq)]
    for kt in range(n_full + 1):
        nxt = [score_fn(h, (kt + 1) * tk) for h in range(2)] if kt < n_full else None
        mask = (diff <= c * tq - kt * tk) if kt == n_full else None
        state = [_update_t(state[h], cur[h], vt_fn(h, kt * tk), mask) for h in range(2)]
        cur = nxt
    return jnp.concatenate([acc / l for (_, l, acc) in state], axis=0)


def _augment(x, col, lane, h, key_side):
    live = (lane < HEAD_DIM) if h == 0 else (lane >= HEAD_DIM)
    a0 = HEAD_DIM if h == 0 else 0
    hi, mid, lo3 = (t.astype(F32) for t in _split3(col))
    c0, o0 = (a0, a0 + 3) if key_side else (a0 + 3, a0)
    aug = jnp.where(lane == c0, hi,
                    jnp.where(lane == c0 + 1, mid,
                              jnp.where(lane == c0 + 2, lo3,
                                        jnp.where((lane >= o0) & (lane < o0 + 3), 1.0, 0.0))))
    return jnp.where(live, x, aug)


def _norm_proj_kernel(x_ref, g_ref, w_ref, o_ref, h_ref):
    h = _rms_rows(x_ref[...], g_ref[...]).astype(BF16)
    h_ref[...] = h
    o_ref[...] = _dot(h, w_ref[...])


def norm_proj(x, g, w, *, tm=1024):
    t, d = x.shape
    n = w.shape[1]
    return pl.pallas_call(
        _norm_proj_kernel,
        grid=(t // tm,),
        in_specs=[pl.BlockSpec((tm, d), lambda i: (i, 0)),
                  pl.BlockSpec((1, d), lambda i: (0, 0)),
                  pl.BlockSpec((d, n), lambda i: (0, 0))],
        out_specs=[pl.BlockSpec((tm, n), lambda i: (i, 0)),
                   pl.BlockSpec((tm, d), lambda i: (i, 0))],
        out_shape=[jax.ShapeDtypeStruct((t, n), F32), jax.ShapeDtypeStruct((t, d), BF16)],
        compiler_params=_cparams("parallel"),
    )(x, g.reshape(1, d), w)


def _matmul_kernel(h_ref, w_ref, o_ref):
    o_ref[...] = _dot(h_ref[...], w_ref[...]).astype(o_ref.dtype)


def matmul_bf16(h, w, *, tm=2048, tn=512):
    t, d = h.shape
    n = w.shape[1]
    return pl.pallas_call(
        _matmul_kernel,
        grid=(t // tm, n // tn),
        in_specs=[pl.BlockSpec((tm, d), lambda i, j: (i, 0)),
                  pl.BlockSpec((d, tn), lambda i, j: (0, j))],
        out_specs=pl.BlockSpec((tm, tn), lambda i, j: (i, j)),
        out_shape=jax.ShapeDtypeStruct((t, n), BF16),
        compiler_params=_cparams("parallel", "parallel"),
    )(h, w)


def _sb_kernel(q_ref, k_ref, v_ref, o_ref, vt_ref, *, tq, tk):
    i = pl.program_id(2)

    @pl.when(i == 0)
    def _():
        vt_ref[...] = v_ref[0].astype(F32).T.astype(BF16)

    sub = LANES
    n_sub = tk // sub
    lane = lax.broadcasted_iota(jnp.int32, (tq, LANES), 1)
    lo = lane < HEAD_DIM
    q = q_ref[0].astype(F32) * (HEAD_DIM ** -0.5)
    qh = (jnp.where(lo, q, 0.0).astype(BF16), jnp.where(lo, 0.0, q).astype(BF16))
    rr = lax.broadcasted_iota(jnp.int32, (sub, sub), 0)
    cc = lax.broadcasted_iota(jnp.int32, (sub, sub), 1)
    upper = jnp.where(cc >= rr, 1.0, 0.0).astype(BF16)
    upper2 = jnp.concatenate([upper, upper], axis=1)
    diff = (lax.broadcasted_iota(jnp.int32, (tk, tq), 0)
            - lax.broadcasted_iota(jnp.int32, (tk, tq), 1))

    def logits_at(kt):
        k = k_ref[0, pl.ds(kt * tk, tk), :]
        return tuple(_nt_dot(k, qh[h]) for h in range(2))

    def tile(c, kt, zs, carry, masked):
        off = kt * tk
        past = (diff < c * tq - off) if masked else None
        out = []
        for h in range(2):
            run, acc = carry[h]
            z = zs[h]
            sp = _softplus(z)
            spm = jnp.where(past, sp, 0.0) if masked else sp
            ws = [None] * n_sub
            for j in reversed(range(n_sub)):
                sl = slice(j * sub, (j + 1) * sub)
                spj = spm[sl]
                hi = spj.astype(BF16)
                mid = (spj - hi.astype(F32)).astype(BF16)
                tail = _dot(upper2, jnp.concatenate([hi, mid], axis=0))
                w = jnp.exp(z[sl] - tail - run)
                if masked:
                    w = jnp.where(past[sl], w, 0.0)
                ws[j] = w.astype(BF16)
                run = run + tail[0:1, :]
            vt = vt_ref[h * HEAD_DIM:(h + 1) * HEAD_DIM, pl.ds(off, tk)]
            out.append((run, acc + _dot(vt, jnp.concatenate(ws, axis=0))))
        return tuple(out)

    zero = (jnp.zeros((1, tq), F32), jnp.zeros((HEAD_DIM, tq), F32))
    for c in range(k_ref.shape[1] // tq):
        @pl.when(i == c)
        def _(c=c):
            n_full = (c * tq) // tk
            zs = logits_at(n_full)
            carry = (zero, zero)
            for kt in range(n_full, -1, -1):
                zs_next = logits_at(kt - 1) if kt > 0 else None
                carry = tile(c, kt, zs, carry, kt == n_full)
                zs = zs_next
            o_t = jnp.concatenate([carry[0][1], carry[1][1]], axis=0)
            o_ref[0] = o_t.T.astype(o_ref.dtype)


def sb_attention(y3, *, tq=256, tk=256):
    b, s, _ = y3.shape
    qb, kb, vb = COL_SB // LANES, (COL_SB + MIX_W) // LANES, (COL_SB + 2 * MIX_W) // LANES
    return pl.pallas_call(
        functools.partial(_sb_kernel, tq=tq, tk=tk),
        grid=(b, 2, s // tq),
        in_specs=[pl.BlockSpec((1, tq, LANES), lambda bi, hp, i: (bi, i, qb + hp)),
                  pl.BlockSpec((1, s, LANES), lambda bi, hp, i: (bi, 0, kb + hp)),
                  pl.BlockSpec((1, s, LANES), lambda bi, hp, i: (bi, 0, vb + hp))],
        out_specs=pl.BlockSpec((1, tq, LANES), lambda bi, hp, i: (bi, i, hp)),
        out_shape=jax.ShapeDtypeStruct((b, s, MIX_W), BF16),
        scratch_shapes=[pltpu.VMEM((LANES, s), BF16)],
        compiler_params=_cparams("parallel", "parallel", "arbitrary"),
    )(y3, y3, y3)


def _fox_cum_kernel(misc_ref, fb_ref, cum_ref, *, s):
    rr = lax.broadcasted_iota(jnp.int32, (LANES, LANES), 0)
    cc = lax.broadcasted_iota(jnp.int32, (LANES, LANES), 1)
    lower = jnp.where(cc <= rr, 1.0, 0.0).astype(BF16)

    def body(n, carry):
        off = pl.multiple_of(n * LANES, LANES)
        x = misc_ref[0, pl.ds(off, LANES), :] + fb_ref[...]
        log_f = jnp.minimum(x, 0.0) - jnp.log(1.0 + jnp.exp(-jnp.abs(x)))
        hi, mid, lo3 = _split3(log_f)
        c = _dot(lower, hi) + _dot(lower, mid) + _dot(lower, lo3) + carry
        cum_ref[0, pl.ds(off, LANES), :] = c
        return c[LANES - 1:LANES, :]

    lax.fori_loop(0, s // LANES, body, jnp.zeros((1, LANES), F32))


def fox_cum(y3, fbias_row):
    b, s, _ = y3.shape
    return pl.pallas_call(
        functools.partial(_fox_cum_kernel, s=s),
        grid=(b,),
        in_specs=[pl.BlockSpec((1, s, LANES), lambda bi: (bi, 0, COL_MISC // LANES)),
                  pl.BlockSpec((1, LANES), lambda bi: (0, 0))],
        out_specs=pl.BlockSpec((1, s, LANES), lambda bi: (bi, 0, 0)),
        out_shape=jax.ShapeDtypeStruct((b, s, LANES), F32),
        compiler_params=_cparams("parallel"),
    )(y3, fbias_row)


def _fox_kernel(q_ref, k_ref, v_ref, cum_ref, gq_ref, gk_ref, o_ref, ka_ref, vt_ref, *, tq, tk):
    hp = pl.program_id(1)
    i = pl.program_id(2)
    s = k_ref.shape[1]

    @pl.when(i == 0)
    def _():
        lane_s = lax.broadcasted_iota(jnp.int32, (s, LANES), 1)
        kn = _pair_rms(k_ref[0].astype(F32), gk_ref[...], lane_s < HEAD_DIM)
        cum = cum_ref[0]
        for h in range(2):
            ck = _lane_col(cum, lane_s, MISC_F + 2 * hp + h)
            ka_ref[:, h * LANES:(h + 1) * LANES] = _augment(kn, -ck, lane_s, h, True).astype(BF16)
        vt_ref[...] = v_ref[0].astype(F32).T.astype(BF16)

    lane = lax.broadcasted_iota(jnp.int32, (tq, LANES), 1)
    qn = _pair_rms(q_ref[0].astype(F32), gq_ref[...], lane < HEAD_DIM) * (HEAD_DIM ** -0.5)
    cum_q = cum_ref[0, pl.ds(pl.multiple_of(i * tq, tq), tq), :]
    qa = [_augment(qn, _lane_col(cum_q, lane, MISC_F + 2 * hp + h), lane, h, False).astype(BF16)
          for h in range(2)]

    def scores(h, off):
        return _nt_dot(ka_ref[pl.ds(off, tk), h * LANES:(h + 1) * LANES], qa[h])

    def values_t(h, off):
        return vt_ref[h * HEAD_DIM:(h + 1) * HEAD_DIM, pl.ds(off, tk)]

    for c in range(s // tq):
        @pl.when(i == c)
        def _(c=c):
            o_ref[0] = _causal_flash_static(c, tq, tk, scores, values_t).T.astype(o_ref.dtype)


def fox_attention(y3, cum, gq, gk, *, tq=256, tk=256):
    b, s, _ = y3.shape
    qb, kb, vb = COL_FOX // LANES, (COL_FOX + MIX_W) // LANES, (COL_FOX + 2 * MIX_W) // LANES
    return pl.pallas_call(
        functools.partial(_fox_kernel, tq=tq, tk=tk),
        grid=(b, 2, s // tq),
        in_specs=[pl.BlockSpec((1, tq, LANES), lambda bi, hp, i: (bi, i, qb + hp)),
                  pl.BlockSpec((1, s, LANES), lambda bi, hp, i: (bi, 0, kb + hp)),
                  pl.BlockSpec((1, s, LANES), lambda bi, hp, i: (bi, 0, vb + hp)),
                  pl.BlockSpec((1, s, LANES), lambda bi, hp, i: (bi, 0, 0)),
                  pl.BlockSpec((1, LANES), lambda bi, hp, i: (0, 0)),
                  pl.BlockSpec((1, LANES), lambda bi, hp, i: (0, 0))],
        out_specs=pl.BlockSpec((1, tq, LANES), lambda bi, hp, i: (bi, i, hp)),
        out_shape=jax.ShapeDtypeStruct((b, s, MIX_W), BF16),
        scratch_shapes=[pltpu.VMEM((s, 2 * LANES), BF16), pltpu.VMEM((LANES, s), BF16)],
        compiler_params=_cparams("parallel", "parallel", "arbitrary"),
    )(y3, y3, y3, cum, gq, gk)


def _mla_prep_kernel(cq_ref, ckv_ref, misc_ref, gcq_ref, gckv_ref, wuq_ref, wuk_ref, wuv_ref,
                     qg_ref, kg_ref, cos_ref, sa_ref, sb_ref, q_out, k_out, v_out, *, ts):
    hq = _rms_rows(cq_ref[0], gcq_ref[...]).astype(BF16)
    hkv = _rms_rows(ckv_ref[0], gckv_ref[...]).astype(BF16)
    q = _dot(hq, wuq_ref[...])
    kn = _dot(hkv, wuk_ref[...])
    v_out[0] = _nt_dot(wuv_ref[...], hkv).astype(v_out.dtype)

    lane = lax.broadcasted_iota(jnp.int32, (ts, LANES), 1)
    misc = misc_ref[0]
    kr = pltpu.roll(jnp.where((lane >= MISC_KR) & (lane < MISC_KR + MLA_ROPE), misc, 0.0),
                    MLA_NOPE - MISC_KR, 1)
    cos, sin = cos_ref[...], sa_ref[...] + sb_ref[...]
    half = MLA_ROPE // 2

    kk = lax.broadcasted_iota(jnp.int32, (2 * LANES, LANES), 0) & (LANES - 1)
    ll = lax.broadcasted_iota(jnp.int32, (2 * LANES, LANES), 1)
    ones2 = jnp.ones((2 * LANES, LANES), BF16)
    first = (ll >= MLA_NOPE) & (ll < MLA_NOPE + half)
    second = (ll >= MLA_NOPE + half) & (ll < MLA_QK)
    swap2 = jnp.where((first & (kk == ll + half)) | (second & (kk == ll - half)),
                      1.0, 0.0).astype(BF16)

    def split2(x):
        hi = x.astype(BF16)
        return jnp.concatenate([hi, (x - hi.astype(F32)).astype(BF16)], axis=1)

    def norm_rope(t, g):
        t = t * lax.rsqrt(_dot(split2(t * t), ones2) / MLA_QK + EPS) * g
        return t * cos + _dot(split2(t), swap2) * sin

    for h in range(N_HEADS):
        sl = slice(h * LANES, (h + 1) * LANES)
        q_out[0, :, sl] = (norm_rope(q[:, sl], qg_ref[...]) * (MLA_QK ** -0.5)).astype(q_out.dtype)
        k_out[0, :, sl] = norm_rope(kn[:, sl] + kr, kg_ref[...]).astype(k_out.dtype)


def mla_prep(y3, gcq, gckv, wuq, wuk, wuv, qg, kg, cos, sa, sb, *, ts=512):
    b, s, _ = y3.shape
    const = lambda shape: pl.BlockSpec(shape, lambda bi, i: (0,) * len(shape))
    return pl.pallas_call(
        functools.partial(_mla_prep_kernel, ts=ts),
        grid=(b, s // ts),
        in_specs=[pl.BlockSpec((1, ts, MLA_Q_LORA), lambda bi, i: (bi, i, COL_CQ // MLA_Q_LORA)),
                  pl.BlockSpec((1, ts, LANES), lambda bi, i: (bi, i, COL_CKV // LANES)),
                  pl.BlockSpec((1, ts, LANES), lambda bi, i: (bi, i, COL_MISC // LANES)),
                  const((1, MLA_Q_LORA)), const((1, MLA_KV_LORA)),
                  const((MLA_Q_LORA, N_HEADS * LANES)), const((MLA_KV_LORA, N_HEADS * LANES)),
                  const((MIX_W, MLA_KV_LORA)), const((1, LANES)), const((1, LANES)),
                  pl.BlockSpec((ts, LANES), lambda bi, i: (i, 0)),
                  pl.BlockSpec((ts, LANES), lambda bi, i: (i, 0)),
                  pl.BlockSpec((ts, LANES), lambda bi, i: (i, 0))],
        out_specs=[pl.BlockSpec((1, ts, N_HEADS * LANES), lambda bi, i: (bi, i, 0)),
                   pl.BlockSpec((1, ts, N_HEADS * LANES), lambda bi, i: (bi, i, 0)),
                   pl.BlockSpec((1, MIX_W, ts), lambda bi, i: (bi, 0, i))],
        out_shape=[jax.ShapeDtypeStruct((b, s, N_HEADS * LANES), BF16),
                   jax.ShapeDtypeStruct((b, s, N_HEADS * LANES), BF16),
                   jax.ShapeDtypeStruct((b, MIX_W, s), BF16)],
        compiler_params=_cparams("parallel", "parallel"),
    )(y3, y3, y3, gcq, gckv, wuq, wuk, wuv, qg, kg, cos, sa, sb)


def _mla_kernel(q_ref, k_ref, vt_ref, o_ref, *, tq, tk):
    i = pl.program_id(2)
    qh = [q_ref[0, :, h * LANES:(h + 1) * LANES] for h in range(2)]

    def scores(h, off):
        return _nt_dot(k_ref[0, pl.ds(off, tk), h * LANES:(h + 1) * LANES], qh[h])

    def values_t(h, off):
        return vt_ref[0, h * HEAD_DIM:(h + 1) * HEAD_DIM, pl.ds(off, tk)]

    for c in range(k_ref.shape[1] // tq):
        @pl.when(i == c)
        def _(c=c):
            o_ref[0] = _causal_flash_static(c, tq, tk, scores, values_t).T.astype(o_ref.dtype)


def mla_attention(q, k, vt, *, tq=256, tk=256):
    b, s, _ = q.shape
    return pl.pallas_call(
        functools.partial(_mla_kernel, tq=tq, tk=tk),
        grid=(b, 2, s // tq),
        in_specs=[pl.BlockSpec((1, tq, 2 * LANES), lambda bi, hp, i: (bi, i, hp)),
                  pl.BlockSpec((1, s, 2 * LANES), lambda bi, hp, i: (bi, 0, hp)),
                  pl.BlockSpec((1, LANES, s), lambda bi, hp, i: (bi, hp, 0))],
        out_specs=pl.BlockSpec((1, tq, LANES), lambda bi, hp, i: (bi, i, hp)),
        out_shape=jax.ShapeDtypeStruct((b, s, MIX_W), BF16),
        compiler_params=_cparams("parallel", "parallel", "arbitrary"),
    )(q, k, vt)


def _rel_bias_tile(dist, tab_ref, h):
    max_exact = REL_BUCKETS // 2
    d = jnp.maximum(dist, 0)
    large = max_exact + (jnp.log(jnp.maximum(d, 1).astype(F32) / max_exact)
                         / math.log(REL_MAX_DIST / max_exact)
                         * (REL_BUCKETS - max_exact)).astype(jnp.int32)
    large = jnp.minimum(large, REL_BUCKETS - 1)
    bucket = jnp.where(d < max_exact, d, large)
    out = jnp.zeros(dist.shape, F32)
    for bkt in range(REL_BUCKETS):
        out = jnp.where(bucket == bkt, tab_ref[bkt, h], out)
    return out


def _band_bias_kernel(tab_ref, slc_ref, win_ref):
    r = pl.program_id(0)
    kj = r * LANES + lax.broadcasted_iota(jnp.int32, (LANES, LANES), 0)
    qi = lax.broadcasted_iota(jnp.int32, (LANES, LANES), 1)
    dist = qi + NSA_WINDOW - kj
    for h in range(N_HEADS):
        delta = _rel_bias_tile(dist, tab_ref, h) - tab_ref[REL_BUCKETS - 1, h]
        slc_ref[h] = jnp.where(dist >= 0, delta, MASKED)
        win_ref[h] = jnp.where((dist >= 0) & (dist < NSA_WINDOW), delta, MASKED)


def _cmp_bias_kernel(tab_ref, o_ref):
    i = pl.program_id(0)
    c = lax.broadcasted_iota(jnp.int32, (LANES, LANES), 0)
    s = i * LANES + lax.broadcasted_iota(jnp.int32, (LANES, LANES), 1)
    dist = s - (c * NSA_CMP_STRIDE + NSA_CMP_LEN - 1)
    for h in range(N_HEADS):
        o_ref[h] = jnp.where(dist >= 0, _rel_bias_tile(dist, tab_ref, h), MASKED)


def rel_bias_tables(rel_bias, s):
    assert REL_BUCKETS == 32 and REL_MAX_DIST == 128 and NSA_WINDOW >= 113
    band = NSA_WINDOW + LANES
    smem = pl.BlockSpec(memory_space=pltpu.SMEM)
    band_spec = pl.BlockSpec((N_HEADS, LANES, LANES), lambda r: (0, r, 0))
    band_shape = jax.ShapeDtypeStruct((N_HEADS, band, LANES), F32)
    band_slc, band_win = pl.pallas_call(
        _band_bias_kernel, grid=(band // LANES,), in_specs=[smem],
        out_specs=[band_spec, band_spec], out_shape=[band_shape, band_shape],
    )(rel_bias)
    cmpb = pl.pallas_call(
        _cmp_bias_kernel, grid=(s // LANES,), in_specs=[smem],
        out_specs=pl.BlockSpec((N_HEADS, LANES, LANES), lambda i: (0, 0, i)),
        out_shape=jax.ShapeDtypeStruct((N_HEADS, LANES, s), F32),
    )(rel_bias)
    return band_slc, band_win, cmpb


def _nsa_compress_kernel(kin_ref, vin_ref, pek_ref, pev_ref, w1k_ref, w1v_ref, w2k_ref, w2v_ref,
                         gk_ref, kc_ref, vct_ref):
    half = NSA_CMP_STRIDE * HEAD_DIM

    def hidden(x, pe_ref, w1_ref):
        a = _dot((x + pe_ref[:, :half]).astype(BF16), w1_ref[:half, :])
        bh = _dot((x + pe_ref[:, half:]).astype(BF16), w1_ref[half:, :])
        pre = a + pltpu.roll(bh, LANES - 1, 0)
        return (pre * _sigmoid(pre)).astype(BF16)

    kc = _dot(hidden(kin_ref[0], pek_ref, w1k_ref), w2k_ref[...])
    kc = kc * lax.rsqrt(jnp.sum(kc * kc, axis=-1, keepdims=True) / HEAD_DIM + EPS) * gk_ref[...]
    kc_ref[0] = kc.astype(kc_ref.dtype)
    vct_ref[0] = _nt_dot(w2v_ref[...], hidden(vin_ref[0], pev_ref, w1v_ref)).astype(vct_ref.dtype)


def nsa_compress(kin, vin, pek, pev, w1k, w1v, w2k, w2v, gk):
    b = kin.shape[0]
    const = lambda shape: pl.BlockSpec(shape, lambda bi: (0,) * len(shape))
    blk = pl.BlockSpec((1,) + kin.shape[1:], lambda bi: (bi, 0, 0))
    out = pl.BlockSpec((1, LANES, LANES), lambda bi: (bi, 0, 0))
    return pl.pallas_call(
        _nsa_compress_kernel, grid=(b,),
        in_specs=[blk, blk, const(pek.shape), const(pev.shape), const(w1k.shape), const(w1v.shape),
                  const(w2k.shape), const(w2v.shape), const(gk.shape)],
        out_specs=[out, out],
        out_shape=[jax.ShapeDtypeStruct((b, LANES, LANES), BF16)] * 2,
        compiler_params=_cparams("parallel"),
    )(kin, vin, pek, pev, w1k, w1v, w2k, w2v, gk)


def _nsa_kernel(q_ref, k2_ref, v2_ref, misc_ref, kc_ref, vct_ref, bslc_ref, bwin_ref, cmpb_ref,
                gq_ref, gk2_ref, o_ref, ks_ref, kw_ref, vt_ref, win_ref, stage_ref):
    tq = LANES
    n_blk = q_ref.shape[1] // tq
    s = k2_ref.shape[1]
    pad = NSA_WINDOW
    band = NSA_WINDOW + tq
    n_sel = s // NSA_SEL_LEN
    assert NSA_SEL_LANE0 + n_sel <= LANES and pad % NSA_FAR_TILE == 0
    ip = pl.program_id(1)

    @pl.when(ip == 0)
    def _():
        lane_s = lax.broadcasted_iota(jnp.int32, (s, LANES), 1)
        row_s = lax.broadcasted_iota(jnp.int32, (s, LANES), 0)
        kn = _pair_rms(k2_ref[0], gk2_ref[...], lane_s < HEAD_DIM)
        sel_lane = NSA_SEL_LANE0 + (row_s >> 6)
        ks_ref[pad:, :] = jnp.where(lane_s < HEAD_DIM, kn,
                                    jnp.where(lane_s == sel_lane, 1.0, 0.0)).astype(BF16)
        kw_ref[pad:, :] = jnp.where(lane_s < HEAD_DIM, pltpu.roll(kn, HEAD_DIM, 1),
                                    0.0).astype(BF16)
        lane_p = lax.broadcasted_iota(jnp.int32, (pad, LANES), 1)
        before = jnp.where(lane_p == NSA_PAD_LANE, MASKED, 0.0).astype(BF16)
        ks_ref[:pad, :] = before
        kw_ref[:pad, :] = before
        vt_ref[:, pad:] = v2_ref[0].T.astype(BF16)
        vt_ref[:, :pad] = jnp.zeros((LANES, pad), BF16)

    lane = lax.broadcasted_iota(jnp.int32, (tq, LANES), 1)
    lo = lane < HEAD_DIM
    scale = HEAD_DIM ** -0.5
    kc = kc_ref[0]
    vct = vct_ref[0, :HEAD_DIM, :]
    jj = lax.broadcasted_iota(jnp.int32, (LANES, LANES), 0)
    c0 = lax.broadcasted_iota(jnp.int32, (LANES, LANES), 1) * NSA_CMP_STRIDE
    j0 = jj * NSA_SEL_LEN
    overlap = jnp.where((c0 < j0 + NSA_SEL_LEN) & (c0 + NSA_CMP_LEN > j0), 1.0, 0.0).astype(BF16)
    jj32 = lax.broadcasted_iota(jnp.int32, (n_sel, tq), 0)
    in_sel = (lane >= NSA_SEL_LANE0) & (lane < NSA_SEL_LANE0 + n_sel)
    blocks = [dict() for _ in range(n_blk)]

    def prepare(u):
        blk = blocks[u]
        i = ip * n_blk + u
        q = q_ref[0, u * tq:(u + 1) * tq, :]
        q_base = []
        for pair in range(2):
            cols = slice(pair * LANES, (pair + 1) * LANES)
            pn = _pair_rms(q[:, cols], gq_ref[:, cols], lo) * scale
            for head in (jnp.where(lo, pn, 0.0), pltpu.roll(jnp.where(lo, 0.0, pn), HEAD_DIM, 1)):
                q_base.append(jnp.where(lane == NSA_PAD_LANE, 1.0, head))
        q_plain = [x.astype(BF16) for x in q_base]
        b0 = pl.multiple_of(i * tq, tq)
        blk.update(i=i, q_base=q_base, q_plain=q_plain,
                   ks_band=ks_ref[pl.ds(b0, band), :],
                   vs_band=vt_ref[:HEAD_DIM, pl.ds(b0, band)],
                   vw_band=vt_ref[HEAD_DIM:, pl.ds(b0, band)])
        blk['s_cmp'] = [_nt_dot(kc, q_plain[h]) + cmpb_ref[h, :, u * tq:(u + 1) * tq]
                        for h in range(N_HEADS)]
        kw_band = kw_ref[pl.ds(b0, band), :]
        for h in range(N_HEADS):
            win_ref[u, h] = _nt_dot(kw_band, q_plain[h])

    def window(u, h):
        _, l_w, a_w = _first_t(win_ref[u, h] + bwin_ref[h], blocks[u]['vw_band'])
        return a_w / l_w

    def compressed(u):
        blk = blocks[u]
        o_cmp = []
        p_sum = jnp.zeros((LANES, tq), F32)
        for h in range(N_HEADS):
            s_c = blk['s_cmp'][h]
            e_c = jnp.where(s_c > 0.5 * MASKED,
                            jnp.exp(s_c - jnp.max(s_c, axis=0, keepdims=True)), 0.0)
            den = jnp.sum(e_c, axis=0, keepdims=True)
            p_c = e_c / jnp.where(den > 0.0, den, 1.0)
            o_cmp.append(_dot(vct, p_c.astype(BF16)))
            p_sum = p_sum + p_c
        hi, mid, lo3 = _split3(p_sum)
        blk['o_cmp'] = o_cmp
        blk['imp'] = (_dot(overlap, hi) + _dot(overlap, mid) + _dot(overlap, lo3))[0:n_sel]

    def select(u):
        blk = blocks[u]
        i = blk['i']
        cur = (i * tq + lax.broadcasted_iota(jnp.int32, (n_sel, tq), 1)) >> 6
        forced = (jj32 == 0) | (jj32 == cur) | (jj32 == cur - 1)
        imp = jnp.where(forced, FORCE, blk['imp'])
        imp = jnp.where(jj32 <= cur, imp, -FORCE)
        cnt = jnp.zeros((n_sel, tq), F32)
        for jp in range(n_sel):
            other = imp[jp:jp + 1, :]
            beats = (other > imp) | ((other == imp) & (jj32 > jp))
            cnt = cnt + jnp.where(beats, 1.0, 0.0)
        sel_neg = jnp.where(cnt < float(NSA_TOP_N), 0.0, MASKED)
        sel_neg = jnp.concatenate([sel_neg, jnp.zeros((LANES - n_sel, tq), F32)], axis=0).T
        sel_neg = pltpu.roll(sel_neg, NSA_SEL_LANE0, 1)
        band_block0 = NSA_SEL_LANE0 + ((i * tq - pad) >> 6)
        blk['q_band'] = [jnp.where(in_sel, sel_neg, x).astype(BF16) for x in blk['q_base']]
        blk['q_far'] = [jnp.where(in_sel, jnp.where(lane >= band_block0, MASKED, sel_neg),
                                  x).astype(BF16) for x in blk['q_base']]

    def stage_selected(u, h):
        stage_ref[u, h % 2] = _nt_dot(blocks[u]['ks_band'], blocks[u]['q_band'][h])

    def selected_band(u):
        blk = blocks[u]
        slc = []
        for h in range(N_HEADS):
            slc.append(_first_t(stage_ref[u, h % 2] + bslc_ref[h], blk['vs_band']))
            if h + 2 < N_HEADS:
                stage_selected(u, h + 2)
        blk['slc'] = slc

    def far_tile(u, kt):
        blk = blocks[u]
        off = pad + kt * NSA_FAR_TILE
        k_far = ks_ref[pl.ds(off, NSA_FAR_TILE), :]
        v_far = vt_ref[:HEAD_DIM, pl.ds(off, NSA_FAR_TILE)]
        q_far = blk['q_far']
        sc = [_nt_dot(k_far, q_far[0]), _nt_dot(k_far, q_far[1])]
        out = []
        for h in range(N_HEADS):
            if h + 2 < N_HEADS:
                sc.append(_nt_dot(k_far, q_far[h + 2]))
            out.append(_update_t(blk['slc'][h], sc[h], v_far, None))
        blk['slc'] = out

    def finish(u):
        blk = blocks[u]
        rows = slice(u * tq, (u + 1) * tq)
        g_t = _sigmoid(misc_ref[0, rows, :]).T
        heads = []
        for h in range(N_HEADS):
            _, l_s, a_s = blk['slc'][h]
            row = lambda n: g_t[MISC_G + n * N_HEADS + h:MISC_G + n * N_HEADS + h + 1, :]
            heads.append(row(0) * blk['o_cmp'][h] + row(1) * (a_s / l_s) + row(2) * blk['o_win'][h])
        o_ref[0, rows, :] = jnp.concatenate(heads, axis=0).T.astype(o_ref.dtype)

    both = range(n_blk)
    for u in both:
        prepare(u)
    for u in both:
        compressed(u)
    for u in both:
        blocks[u]['o_win'] = [window(u, 0), window(u, 1)]
    for u in both:
        select(u)
    for u in both:
        stage_selected(u, 0)
        stage_selected(u, 1)
    for u in both:
        blocks[u]['o_win'] += [window(u, 2), window(u, 3)]
    for u in both:
        selected_band(u)

    for c in range(s // (n_blk * tq)):
        @pl.when(ip == c)
        def _(c=c):
            saved = [blocks[u]['slc'] for u in both]
            n_far = [(max((c * n_blk + u) * tq - pad, 0) + NSA_FAR_TILE - 1) // NSA_FAR_TILE
                     for u in both]
            for kt in range(max(n_far)):
                for u in both:
                    if kt < n_far[u]:
                        far_tile(u, kt)
            for u in both:
                finish(u)
                blocks[u]['slc'] = saved[u]


def nsa_attention(y3, kc, vct, band_slc, band_win, cmpb, gq, gk2):
    b, s, _ = y3.shape
    n_blk = 2
    tq = n_blk * LANES
    band = NSA_WINDOW + LANES
    return pl.pallas_call(
        _nsa_kernel,
        grid=(b, s // tq),
        in_specs=[pl.BlockSpec((1, tq, MIX_W), lambda bi, i: (bi, i, COL_NSAQ // MIX_W)),
                  pl.BlockSpec((1, s, LANES), lambda bi, i: (bi, 0, COL_K2 // LANES)),
                  pl.BlockSpec((1, s, LANES), lambda bi, i: (bi, 0, COL_V2 // LANES)),
                  pl.BlockSpec((1, tq, LANES), lambda bi, i: (bi, i, COL_MISC // LANES)),
                  pl.BlockSpec((1, LANES, LANES), lambda bi, i: (bi, 0, 0)),
                  pl.BlockSpec((1, LANES, LANES), lambda bi, i: (bi, 0, 0)),
                  pl.BlockSpec(band_slc.shape, lambda bi, i: (0, 0, 0)),
                  pl.BlockSpec(band_win.shape, lambda bi, i: (0, 0, 0)),
                  pl.BlockSpec((N_HEADS, LANES, tq), lambda bi, i: (0, 0, i)),
                  pl.BlockSpec((1, MIX_W), lambda bi, i: (0, 0)),
                  pl.BlockSpec((1, LANES), lambda bi, i: (0, 0))],
        out_specs=pl.BlockSpec((1, tq, MIX_W), lambda bi, i: (bi, i, 0)),
        out_shape=jax.ShapeDtypeStruct((b, s, MIX_W), BF16),
        scratch_shapes=[pltpu.VMEM((s + NSA_WINDOW, LANES), BF16),
                        pltpu.VMEM((s + NSA_WINDOW, LANES), BF16),
                        pltpu.VMEM((LANES, s + NSA_WINDOW), BF16),
                        pltpu.VMEM((n_blk, N_HEADS, band, LANES), F32),
                        pltpu.VMEM((n_blk, 2, band, LANES), F32)],
        compiler_params=_cparams("parallel", "arbitrary"),
    )(y3, y3, y3, y3, kc, vct, band_slc, band_win, cmpb, gq, gk2)


def _merge_kernel(ysb_ref, ymla_ref, ynsa_ref, yfox_ref, g0_ref, g1_ref, g2_ref, g3_ref,
                  wb_ref, wo_ref, x_ref, o_ref):
    u = None
    for n, (y_ref, g_ref) in enumerate(((ysb_ref, g0_ref), (ymla_ref, g1_ref),
                                        (ynsa_ref, g2_ref), (yfox_ref, g3_ref))):
        term = _sigmoid(g_ref[...].astype(F32)) * _dot(y_ref[...], wb_ref[n])
        u = term if u is None else u + term
    o_ref[...] = x_ref[...] + _dot(u.astype(BF16), wo_ref[...])


def merge_branches(ys, y, wb, wo, x, *, tm=512):
    t, d = x.shape
    gate_blk = COL_GATE // d
    yspec = pl.BlockSpec((tm, MIX_W), lambda i: (i, 0))
    gspecs = [pl.BlockSpec((tm, d), lambda i, n=n: (i, gate_blk + n)) for n in range(N_BRANCH)]
    return pl.pallas_call(
        _merge_kernel, grid=(t // tm,),
        in_specs=[yspec] * 4 + gspecs + [pl.BlockSpec(wb.shape, lambda i: (0, 0, 0)),
                                         pl.BlockSpec(wo.shape, lambda i: (0, 0)),
                                         pl.BlockSpec((tm, d), lambda i: (i, 0))],
        out_specs=pl.BlockSpec((tm, d), lambda i: (i, 0)),
        out_shape=jax.ShapeDtypeStruct((t, d), F32),
        compiler_params=_cparams("parallel"),
    )(*ys, y, y, y, y, wb, wo, x)


def _mlp_ple_kernel(x_ref, g_ref, wu_ref, wd_ref, gp_ref, wg_ref, p_ref, wp_ref, o_ref,
                    h_ref, acc_ref):
    f = pl.program_id(1)

    @pl.when(f == 0)
    def _():
        h_ref[...] = _rms_rows(x_ref[...], g_ref[...]).astype(BF16)
        acc_ref[...] = jnp.zeros_like(acc_ref)

    a = jnp.maximum(_dot(h_ref[...], wu_ref[...]), 0.0)
    acc_ref[...] += _dot((a * a).astype(BF16), wd_ref[...])

    @pl.when(f == pl.num_programs(1) - 1)
    def _():
        x1 = x_ref[...] + acc_ref[...]
        gate = _sigmoid(_dot(_rms_rows(x1, gp_ref[...]).astype(BF16), wg_ref[...]))
        o_ref[...] = x1 + gate * _dot(p_ref[...].astype(BF16), wp_ref[...])


def mlp_ple(x, g, wu, wd, gp, wg, p_all, layer, wp, *, tm=1024, tf=1024):
    t, d = x.shape
    ff = wu.shape[1]
    return pl.pallas_call(
        _mlp_ple_kernel, grid=(t // tm, ff // tf),
        in_specs=[pl.BlockSpec((tm, d), lambda i, f: (i, 0)),
                  pl.BlockSpec((1, d), lambda i, f: (0, 0)),
                  pl.BlockSpec((d, tf), lambda i, f: (0, f)),
                  pl.BlockSpec((tf, d), lambda i, f: (f, 0)),
                  pl.BlockSpec((1, d), lambda i, f: (0, 0)),
                  pl.BlockSpec((d, d), lambda i, f: (0, 0)),
                  pl.BlockSpec((None, tm, PLE_DIM), lambda i, f: (layer, i, 0)),
                  pl.BlockSpec((PLE_DIM, d), lambda i, f: (0, 0))],
        out_specs=pl.BlockSpec((tm, d), lambda i, f: (i, 0)),
        out_shape=jax.ShapeDtypeStruct((t, d), F32),
        scratch_shapes=[pltpu.VMEM((tm, d), BF16), pltpu.VMEM((tm, d), F32)],
        compiler_params=_cparams("parallel", "arbitrary"),
    )(x, g.reshape(1, d), wu, wd, gp.reshape(1, d), wg, p_all, wp)


def _pack_w_in(w):
    offs = np.concatenate([[0], np.cumsum(IN_WIDTHS)]).tolist()
    (sb_q, sb_k, sb_v, cq, ckv, kr, nsa_q, kc, vc, ks, vs, kw, vw, ng,
     fox_q, fox_k, fox_v, ff, gate) = [w[:, offs[n]:offs[n + 1]] for n in range(len(IN_WIDTHS))]
    d = w.shape[0]
    z = lambda n: jnp.zeros((d, n), w.dtype)
    misc = jnp.concatenate([ff, ng, z(MISC_KR - MISC_G - 3 * N_HEADS), kr,
                            z(LANES - MISC_KR - MLA_ROPE)], axis=1)
    w_f32_part = jnp.concatenate([cq, ckv, ks, kw, vs, vw, kc, vc, misc, nsa_q], axis=1)
    w_bf16_part = jnp.concatenate([gate, sb_q, sb_k, sb_v, fox_q, fox_k, fox_v], axis=1)
    return w_f32_part.astype(BF16), w_bf16_part.astype(BF16)


def _head_slots(w, width):
    k = w.shape[0]
    w = w.reshape(k, N_HEADS, width)
    return jnp.pad(w, ((0, 0), (0, 0), (0, LANES - width))).reshape(k, N_HEADS * LANES)


def _rope_tables(s):
    half = MLA_ROPE // 2
    inv = jnp.exp(-math.log(ROPE_THETA) * jnp.arange(half, dtype=F32) / half)
    ang = jnp.arange(s, dtype=F32)[:, None] * inv[None, :]
    cos, sin = jnp.cos(ang), jnp.sin(ang)
    ones = jnp.ones((s, MLA_NOPE), F32)
    zeros = lambda n: jnp.zeros((s, n), F32)
    tail = LANES - MLA_QK
    cos_t = jnp.concatenate([ones, cos, cos, jnp.ones((s, tail), F32)], axis=1)
    sa_t = jnp.concatenate([zeros(MLA_NOPE), -sin, zeros(half), zeros(tail)], axis=1)
    sb_t = jnp.concatenate([zeros(MLA_NOPE), zeros(half), sin, zeros(tail)], axis=1)
    return cos_t, sa_t, sb_t


def _pad_lanes(v, left=0):
    v = v.reshape(1, -1)
    return jnp.pad(v, ((0, 0), (left, LANES - left - v.shape[1])))


def _pad_to_lanes(w):
    return jnp.pad(w, ((0, 0), (0, LANES - w.shape[1])))


def kernel(x, p, rel_bias, norm_mix_g, w_in, mla_cq_norm_g, mla_ckv_norm_g, mla_w_uq, mla_w_ukv,
           mla_qn_g, mla_kn_g, nsa_pe_k, nsa_pe_v, nsa_w1_k, nsa_w2_k, nsa_w1_v, nsa_w2_v,
           nsa_qn_g, nsa_kn_g, fox_f_bias, fox_qn_g, fox_kn_g, w_branch, w_o, norm_mlp_g,
           w_mlp_up, w_mlp_down, norm_ple_g, w_ple_gate, w_ple_proj):
    b, s, d = x.shape
    t = b * s
    xf = x.reshape(t, d)
    p_all = p.reshape(DEPTH, t, PLE_DIM)
    band_slc, band_win, cmpb = rel_bias_tables(rel_bias.astype(F32), s)
    cos_t, sa_t, sb_t = _rope_tables(s)
    n_cmp_in = NSA_CMP_STRIDE * HEAD_DIM

    for i in range(DEPTH):
        w_a, w_b = _pack_w_in(w_in[i])
        ya, h = norm_proj(xf, norm_mix_g[i], w_a)
        yb = matmul_bf16(h, w_b)
        y3 = ya.reshape(b, s, N_F32)
        yb3 = yb.reshape(b, s, N_BF16)

        y_sb = sb_attention(yb3)

        wukv = mla_w_ukv[i].reshape(MLA_KV_LORA, N_HEADS, MLA_NOPE + MLA_V)
        q_m, k_m, vt_m = mla_prep(
            y3, mla_cq_norm_g[i].reshape(1, -1), mla_ckv_norm_g[i].reshape(1, -1),
            _head_slots(mla_w_uq[i], MLA_QK).astype(BF16),
            _head_slots(wukv[:, :, :MLA_NOPE].reshape(MLA_KV_LORA, -1), MLA_NOPE).astype(BF16),
            wukv[:, :, MLA_NOPE:].reshape(MLA_KV_LORA, -1).T.astype(BF16),
            _pad_lanes(mla_qn_g[i]), _pad_lanes(mla_kn_g[i]), cos_t, sa_t, sb_t)
        y_mla = mla_attention(q_m, k_m, vt_m)

        kin = y3[:, :, COL_KCVC:COL_KCVC + HEAD_DIM].reshape(b, s // NSA_CMP_STRIDE, n_cmp_in)
        vin = y3[:, :, COL_KCVC + HEAD_DIM:COL_KCVC + 2 * HEAD_DIM].reshape(
            b, s // NSA_CMP_STRIDE, n_cmp_in)
        pad_sq = lambda w: jnp.pad(_pad_to_lanes(w), ((0, LANES - w.shape[0]), (0, 0)))
        kc, vct = nsa_compress(
            kin, vin, nsa_pe_k[i].reshape(1, -1), nsa_pe_v[i].reshape(1, -1),
            _pad_to_lanes(nsa_w1_k[i]).astype(BF16), _pad_to_lanes(nsa_w1_v[i]).astype(BF16),
            pad_sq(nsa_w2_k[i]).astype(BF16), pad_sq(nsa_w2_v[i].T).astype(BF16),
            _pad_lanes(nsa_kn_g[i, 0]))
        y_nsa = nsa_attention(
            y3, kc, vct, band_slc, band_win, cmpb,
            jnp.tile(nsa_qn_g[i], N_HEADS).reshape(1, -1),
            jnp.concatenate([nsa_kn_g[i, 1], nsa_kn_g[i, 2]]).reshape(1, -1))

        cum = fox_cum(y3, _pad_lanes(fox_f_bias[i], MISC_F))
        y_fox = fox_attention(yb3, cum, jnp.tile(fox_qn_g[i], 2).reshape(1, -1),
                              jnp.tile(fox_kn_g[i], 2).reshape(1, -1))

        ys = [a.reshape(t, MIX_W) for a in (y_sb, y_mla, y_nsa, y_fox)]
        xf = merge_branches(ys, yb, w_branch[i].astype(BF16), w_o[i].astype(BF16), xf)
        xf = mlp_ple(xf, norm_mlp_g[i], w_mlp_up[i].astype(BF16), w_mlp_down[i].astype(BF16),
                     norm_ple_g[i], w_ple_gate[i].astype(BF16), p_all, i,
                     w_ple_proj[i].astype(BF16))
    return xf.reshape(b, s, d)
```

```python
import functools
import math

import numpy as np
import jax
import jax.numpy as jnp
from jax import lax
from jax.experimental import pallas as pl
from jax.experimental.pallas import tpu as pltpu

F32 = jnp.float32
BF16 = jnp.bfloat16

D_MODEL = 1024
DEPTH = 4
HEAD_DIM = 64
N_HEADS = 4
MIX_W = N_HEADS * HEAD_DIM
N_BRANCH = 4
EPS = 1e-6
FORCE = 1e9
MLA_Q_LORA = 384
MLA_KV_LORA = 128
MLA_NOPE = 64
MLA_ROPE = 32
MLA_V = 64
MLA_QK = MLA_NOPE + MLA_ROPE
ROPE_THETA = 10000.0
NSA_CMP_LEN = 32
NSA_CMP_STRIDE = 16
NSA_SEL_LEN = 64
NSA_TOP_N = 16
NSA_WINDOW = 512
REL_BUCKETS = 32
REL_MAX_DIST = 128
D_FF = 4 * D_MODEL
PLE_DIM = 256

LANES = 128
MASKED = -1e30
VMEM_LIMIT = 56 * 1024 * 1024

IN_WIDTHS = ((MIX_W,) * 3
             + (MLA_Q_LORA, MLA_KV_LORA, MLA_ROPE)
             + (MIX_W,) + (HEAD_DIM,) * 6 + (3 * N_HEADS,)
             + (MIX_W,) * 3 + (N_HEADS,)
             + (N_BRANCH * D_MODEL,))

COL_CQ = 0
COL_CKV = 384
COL_K2 = 512
COL_V2 = 640
COL_KCVC = 768
COL_MISC = 896
COL_NSAQ = 1024
N_F32 = 1280
COL_GATE = 0
COL_SB = 4096
COL_FOX = 4864
N_BF16 = 5632
MISC_F = 0
MISC_G = 4
MISC_KR = 32

NSA_PAD_LANE = HEAD_DIM
NSA_SEL_LANE0 = HEAD_DIM + 1
NSA_FAR_TILE = 512


def _cparams(*sem):
    return pltpu.CompilerParams(dimension_semantics=sem, vmem_limit_bytes=VMEM_LIMIT)


def _nt_dot(a, b):
    return lax.dot_general(a, b, (((1,), (1,)), ((), ())), preferred_element_type=F32)


def _dot(a, b):
    return jnp.dot(a, b, preferred_element_type=F32)


def _split3(x):
    hi = x.astype(BF16)
    r1 = x - hi.astype(F32)
    mid = r1.astype(BF16)
    lo = (r1 - mid.astype(F32)).astype(BF16)
    return hi, mid, lo


def _softplus(z):
    return jnp.maximum(z, 0.0) + jnp.log(1.0 + jnp.exp(-jnp.abs(z)))


def _sigmoid(z):
    return 1.0 / (1.0 + jnp.exp(-z))


def _rms_rows(x, g):
    r = lax.rsqrt(jnp.mean(x * x, axis=-1, keepdims=True) + EPS)
    return x * r * g


def _pair_rms(x, g, lo):
    x2 = x * x
    s0 = jnp.sum(jnp.where(lo, x2, 0.0), axis=-1, keepdims=True)
    s1 = jnp.sum(jnp.where(lo, 0.0, x2), axis=-1, keepdims=True)
    r = jnp.where(lo, lax.rsqrt(s0 / HEAD_DIM + EPS), lax.rsqrt(s1 / HEAD_DIM + EPS))
    return x * r * g


def _lane_col(x, lane, idx):
    return jnp.sum(jnp.where(lane == idx, x, 0.0), axis=-1, keepdims=True)


def _first_t(s, vt):
    m = jnp.max(s, axis=0, keepdims=True)
    p = jnp.exp(s - m)
    return m, jnp.sum(p, axis=0, keepdims=True), _dot(vt, p.astype(BF16))


def _update_t(carry, s, vt, mask):
    m, l, acc = carry
    if mask is not None:
        s = jnp.where(mask, s, MASKED)
    m_new = jnp.maximum(m, jnp.max(s, axis=0, keepdims=True))
    p = jnp.exp(s - m_new)
    if mask is not None:
        p = jnp.where(mask, p, 0.0)
    alpha = jnp.exp(m - m_new)
    l = alpha * l + jnp.sum(p, axis=0, keepdims=True)
    acc = alpha * acc + _dot(vt, p.astype(BF16))
    return m_new, l, acc


def _init_t(tq):
    return (jnp.full((1, tq), MASKED, F32), jnp.zeros((1, tq), F32),
            jnp.zeros((HEAD_DIM, tq), F32))


def _causal_flash_static(c, tq, tk, score_fn, vt_fn):
    assert tk % tq == 0
    heads = range(N_HEADS)
    diff = (lax.broadcasted_iota(jnp.int32, (tk, tq), 0)
            - lax.broadcasted_iota(jnp.int32, (tk, tq), 1))
    n_full = (c * tq) // tk
    cur = [score_fn(h, 0) for h in heads]
    state = [_init_t(tq) for _ in heads]
    for kt in range(n_full + 1):
        nxt = [score_fn(h, (kt + 1) * tk) for h in heads] if kt < n_full else None
        mask = (diff <= c * tq - kt * tk) if kt == n_full else None
        state = [_update_t(state[h], cur[h], vt_fn(h, kt * tk), mask) for h in heads]
        cur = nxt
    return jnp.concatenate([acc / l for (_, l, acc) in state], axis=0)


def _augment(x, col, lane, h, key_side):
    live = (lane < HEAD_DIM) if h == 0 else (lane >= HEAD_DIM)
    a0 = HEAD_DIM if h == 0 else 0
    hi, mid, lo3 = (t.astype(F32) for t in _split3(col))
    c0, o0 = (a0, a0 + 3) if key_side else (a0 + 3, a0)
    aug = jnp.where(lane == c0, hi,
                    jnp.where(lane == c0 + 1, mid,
                              jnp.where(lane == c0 + 2, lo3,
                                        jnp.where((lane >= o0) & (lane < o0 + 3), 1.0, 0.0))))
    return jnp.where(live, x, aug)


def _norm_proj_kernel(x_ref, g_ref, w_ref, o_ref, h_ref):
    h = _rms_rows(x_ref[...], g_ref[...]).astype(BF16)
    h_ref[...] = h
    o_ref[...] = _dot(h, w_ref[...])


def norm_proj(x, g, w_all, layer, *, tm=1024):
    t, d = x.shape
    n = w_all.shape[2]
    return pl.pallas_call(
        _norm_proj_kernel,
        grid=(t // tm,),
        in_specs=[pl.BlockSpec((tm, d), lambda i: (i, 0)),
                  pl.BlockSpec((1, d), lambda i: (0, 0)),
                  pl.BlockSpec((None, d, n), lambda i: (layer, 0, 0))],
        out_specs=[pl.BlockSpec((tm, n), lambda i: (i, 0)),
                   pl.BlockSpec((tm, d), lambda i: (i, 0))],
        out_shape=[jax.ShapeDtypeStruct((t, n), F32), jax.ShapeDtypeStruct((t, d), BF16)],
        compiler_params=_cparams("parallel"),
    )(x, g.reshape(1, d), w_all)


def _matmul_kernel(h_ref, w_ref, o_ref):
    o_ref[...] = _dot(h_ref[...], w_ref[...]).astype(o_ref.dtype)


def matmul_bf16(h, w_all, layer, *, tm=2048, tn=512):
    t, d = h.shape
    n = w_all.shape[2]
    return pl.pallas_call(
        _matmul_kernel,
        grid=(t // tm, n // tn),
        in_specs=[pl.BlockSpec((tm, d), lambda i, j: (i, 0)),
                  pl.BlockSpec((None, d, tn), lambda i, j: (layer, 0, j))],
        out_specs=pl.BlockSpec((tm, tn), lambda i, j: (i, j)),
        out_shape=jax.ShapeDtypeStruct((t, n), BF16),
        compiler_params=_cparams("parallel", "parallel"),
    )(h, w_all)


def _sb_kernel(q_ref, k_ref, v_ref, o_ref, vt_ref, *, tq, tk):
    i = pl.program_id(1)

    @pl.when(i == 0)
    def _():
        vt_ref[...] = v_ref[0].astype(F32).T.astype(BF16)

    sub = LANES
    n_sub = tk // sub
    heads = range(N_HEADS)
    lane = lax.broadcasted_iota(jnp.int32, (tq, LANES), 1)
    lo = lane < HEAD_DIM
    qh = []
    for p in range(N_HEADS // 2):
        q = q_ref[0, :, p * LANES:(p + 1) * LANES].astype(F32) * (HEAD_DIM ** -0.5)
        qh += [jnp.where(lo, q, 0.0).astype(BF16), jnp.where(lo, 0.0, q).astype(BF16)]
    rr = lax.broadcasted_iota(jnp.int32, (sub, sub), 0)
    cc = lax.broadcasted_iota(jnp.int32, (sub, sub), 1)
    upper = jnp.where(cc >= rr, 1.0, 0.0).astype(BF16)
    upper2 = jnp.concatenate([upper, upper], axis=1)
    diff = (lax.broadcasted_iota(jnp.int32, (tk, tq), 0)
            - lax.broadcasted_iota(jnp.int32, (tk, tq), 1))

    def logits_at(kt):
        return tuple(_nt_dot(k_ref[0, pl.ds(kt * tk, tk), (h // 2) * LANES:(h // 2 + 1) * LANES],
                             qh[h]) for h in heads)

    def tile(c, kt, zs, carry, masked):
        off = kt * tk
        past = (diff < c * tq - off) if masked else None
        out = []
        for h in heads:
            run, acc = carry[h]
            z = zs[h]
            sp = _softplus(z)
            spm = jnp.where(past, sp, 0.0) if masked else sp
            ws = [None] * n_sub
            for j in reversed(range(n_sub)):
                sl = slice(j * sub, (j + 1) * sub)
                spj = spm[sl]
                hi = spj.astype(BF16)
                mid = (spj - hi.astype(F32)).astype(BF16)
                tail = _dot(upper2, jnp.concatenate([hi, mid], axis=0))
                w = jnp.exp(z[sl] - tail - run)
                if masked:
                    w = jnp.where(past[sl], w, 0.0)
                ws[j] = w.astype(BF16)
                run = run + tail[0:1, :]
            vt = vt_ref[h * HEAD_DIM:(h + 1) * HEAD_DIM, pl.ds(off, tk)]
            out.append((run, acc + _dot(vt, jnp.concatenate(ws, axis=0))))
        return tuple(out)

    zero = (jnp.zeros((1, tq), F32), jnp.zeros((HEAD_DIM, tq), F32))
    for c in range(k_ref.shape[1] // tq):
        @pl.when(i == c)
        def _(c=c):
            n_full = (c * tq) // tk
            zs = logits_at(n_full)
            carry = (zero,) * N_HEADS
            for kt in range(n_full, -1, -1):
                zs_next = logits_at(kt - 1) if kt > 0 else None
                carry = tile(c, kt, zs, carry, kt == n_full)
                zs = zs_next
            o_t = jnp.concatenate([acc for _, acc in carry], axis=0)
            o_ref[0] = o_t.T.astype(o_ref.dtype)


def sb_attention(y3, *, tq=256, tk=256):
    b, s, _ = y3.shape
    qb = COL_SB // MIX_W
    return pl.pallas_call(
        functools.partial(_sb_kernel, tq=tq, tk=tk),
        grid=(b, s // tq),
        in_specs=[pl.BlockSpec((1, tq, MIX_W), lambda bi, i: (bi, i, qb)),
                  pl.BlockSpec((1, s, MIX_W), lambda bi, i: (bi, 0, qb + 1)),
                  pl.BlockSpec((1, s, MIX_W), lambda bi, i: (bi, 0, qb + 2))],
        out_specs=pl.BlockSpec((1, tq, MIX_W), lambda bi, i: (bi, i, 0)),
        out_shape=jax.ShapeDtypeStruct((b, s, MIX_W), BF16),
        scratch_shapes=[pltpu.VMEM((MIX_W, s), BF16)],
        compiler_params=_cparams("parallel", "arbitrary"),
    )(y3, y3, y3)


def _fox_cum_kernel(misc_ref, fb_ref, cum_ref, *, s):
    rr = lax.broadcasted_iota(jnp.int32, (LANES, LANES), 0)
    cc = lax.broadcasted_iota(jnp.int32, (LANES, LANES), 1)
    lower = jnp.where(cc <= rr, 1.0, 0.0).astype(BF16)

    def body(n, carry):
        off = pl.multiple_of(n * LANES, LANES)
        x = misc_ref[0, pl.ds(off, LANES), :] + fb_ref[...]
        log_f = jnp.minimum(x, 0.0) - jnp.log(1.0 + jnp.exp(-jnp.abs(x)))
        hi, mid, lo3 = _split3(log_f)
        c = _dot(lower, hi) + _dot(lower, mid) + _dot(lower, lo3) + carry
        cum_ref[0, pl.ds(off, LANES), :] = c
        return c[LANES - 1:LANES, :]

    lax.fori_loop(0, s // LANES, body, jnp.zeros((1, LANES), F32))


def fox_cum(y3, fbias_row):
    b, s, _ = y3.shape
    return pl.pallas_call(
        functools.partial(_fox_cum_kernel, s=s),
        grid=(b,),
        in_specs=[pl.BlockSpec((1, s, LANES), lambda bi: (bi, 0, COL_MISC // LANES)),
                  pl.BlockSpec((1, LANES), lambda bi: (0, 0))],
        out_specs=pl.BlockSpec((1, s, LANES), lambda bi: (bi, 0, 0)),
        out_shape=jax.ShapeDtypeStruct((b, s, LANES), F32),
        compiler_params=_cparams("parallel"),
    )(y3, fbias_row)


def _fox_kernel(q_ref, k_ref, v_ref, cum_ref, gq_ref, gk_ref, o_ref, ka_ref, vt_ref, *, tq, tk):
    i = pl.program_id(1)
    s = k_ref.shape[1]
    pairs = range(N_HEADS // 2)

    @pl.when(i == 0)
    def _():
        lane_s = lax.broadcasted_iota(jnp.int32, (s, LANES), 1)
        cum = cum_ref[0]
        for p in pairs:
            cols = slice(p * LANES, (p + 1) * LANES)
            kn = _pair_rms(k_ref[0, :, cols].astype(F32), gk_ref[:, cols], lane_s < HEAD_DIM)
            for h in range(2):
                ck = _lane_col(cum, lane_s, MISC_F + 2 * p + h)
                ka_ref[:, (2 * p + h) * LANES:(2 * p + h + 1) * LANES] = _augment(
                    kn, -ck, lane_s, h, True).astype(BF16)
        vt_ref[...] = v_ref[0].astype(F32).T.astype(BF16)

    lane = lax.broadcasted_iota(jnp.int32, (tq, LANES), 1)
    cum_q = cum_ref[0, pl.ds(pl.multiple_of(i * tq, tq), tq), :]
    qa = []
    for p in pairs:
        cols = slice(p * LANES, (p + 1) * LANES)
        qn = _pair_rms(q_ref[0, :, cols].astype(F32), gq_ref[:, cols], lane < HEAD_DIM)
        qn = qn * (HEAD_DIM ** -0.5)
        qa += [_augment(qn, _lane_col(cum_q, lane, MISC_F + 2 * p + h), lane, h, False).astype(BF16)
               for h in range(2)]

    def scores(h, off):
        return _nt_dot(ka_ref[pl.ds(off, tk), h * LANES:(h + 1) * LANES], qa[h])

    def values_t(h, off):
        return vt_ref[h * HEAD_DIM:(h + 1) * HEAD_DIM, pl.ds(off, tk)]

    for c in range(s // tq):
        @pl.when(i == c)
        def _(c=c):
            o_ref[0] = _causal_flash_static(c, tq, tk, scores, values_t).T.astype(o_ref.dtype)


def fox_attention(y3, cum, gq, gk, *, tq=256, tk=256):
    b, s, _ = y3.shape
    qb = COL_FOX // MIX_W
    return pl.pallas_call(
        functools.partial(_fox_kernel, tq=tq, tk=tk),
        grid=(b, s // tq),
        in_specs=[pl.BlockSpec((1, tq, MIX_W), lambda bi, i: (bi, i, qb)),
                  pl.BlockSpec((1, s, MIX_W), lambda bi, i: (bi, 0, qb + 1)),
                  pl.BlockSpec((1, s, MIX_W), lambda bi, i: (bi, 0, qb + 2)),
                  pl.BlockSpec((1, s, LANES), lambda bi, i: (bi, 0, 0)),
                  pl.BlockSpec((1, MIX_W), lambda bi, i: (0, 0)),
                  pl.BlockSpec((1, MIX_W), lambda bi, i: (0, 0))],
        out_specs=pl.BlockSpec((1, tq, MIX_W), lambda bi, i: (bi, i, 0)),
        out_shape=jax.ShapeDtypeStruct((b, s, MIX_W), BF16),
        scratch_shapes=[pltpu.VMEM((s, N_HEADS * LANES), BF16), pltpu.VMEM((MIX_W, s), BF16)],
        compiler_params=_cparams("parallel", "arbitrary"),
    )(y3, y3, y3, cum, gq, gk)


def _mla_prep_kernel(cq_ref, ckv_ref, misc_ref, gcq_ref, gckv_ref, wuq_ref, wuk_ref, wuv_ref,
                     qg_ref, kg_ref, cos_ref, sa_ref, sb_ref, q_out, k_out, v_out, *, ts):
    hq = _rms_rows(cq_ref[0], gcq_ref[...]).astype(BF16)
    hkv = _rms_rows(ckv_ref[0], gckv_ref[...]).astype(BF16)
    q = _dot(hq, wuq_ref[...])
    kn = _dot(hkv, wuk_ref[...])
    v_out[0] = _nt_dot(wuv_ref[...], hkv).astype(v_out.dtype)

    lane = lax.broadcasted_iota(jnp.int32, (ts, LANES), 1)
    misc = misc_ref[0]
    kr = pltpu.roll(jnp.where((lane >= MISC_KR) & (lane < MISC_KR + MLA_ROPE), misc, 0.0),
                    MLA_NOPE - MISC_KR, 1)
    cos, sin = cos_ref[...], sa_ref[...] + sb_ref[...]
    half = MLA_ROPE // 2

    kk = lax.broadcasted_iota(jnp.int32, (2 * LANES, LANES), 0) & (LANES - 1)
    ll = lax.broadcasted_iota(jnp.int32, (2 * LANES, LANES), 1)
    ones2 = jnp.ones((2 * LANES, LANES), BF16)
    first = (ll >= MLA_NOPE) & (ll < MLA_NOPE + half)
    second = (ll >= MLA_NOPE + half) & (ll < MLA_QK)
    swap2 = jnp.where((first & (kk == ll + half)) | (second & (kk == ll - half)),
                      1.0, 0.0).astype(BF16)

    def split2(x):
        hi = x.astype(BF16)
        return jnp.concatenate([hi, (x - hi.astype(F32)).astype(BF16)], axis=1)

    def norm_rope(t, g):
        t = t * lax.rsqrt(_dot(split2(t * t), ones2) / MLA_QK + EPS) * g
        return t * cos + _dot(split2(t), swap2) * sin

    for h in range(N_HEADS):
        sl = slice(h * LANES, (h + 1) * LANES)
        q_out[0, :, sl] = (norm_rope(q[:, sl], qg_ref[...]) * (MLA_QK ** -0.5)).astype(q_out.dtype)
        k_out[0, :, sl] = norm_rope(kn[:, sl] + kr, kg_ref[...]).astype(k_out.dtype)


def mla_prep(y3, gcq, gckv, wuq, wuk, wuv, qg, kg, cos, sa, sb, *, ts=512):
    b, s, _ = y3.shape
    const = lambda shape: pl.BlockSpec(shape, lambda bi, i: (0,) * len(shape))
    return pl.pallas_call(
        functools.partial(_mla_prep_kernel, ts=ts),
        grid=(b, s // ts),
        in_specs=[pl.BlockSpec((1, ts, MLA_Q_LORA), lambda bi, i: (bi, i, COL_CQ // MLA_Q_LORA)),
                  pl.BlockSpec((1, ts, LANES), lambda bi, i: (bi, i, COL_CKV // LANES)),
                  pl.BlockSpec((1, ts, LANES), lambda bi, i: (bi, i, COL_MISC // LANES)),
                  const((1, MLA_Q_LORA)), const((1, MLA_KV_LORA)),
                  const((MLA_Q_LORA, N_HEADS * LANES)), const((MLA_KV_LORA, N_HEADS * LANES)),
                  const((MIX_W, MLA_KV_LORA)), const((1, LANES)), const((1, LANES)),
                  pl.BlockSpec((ts, LANES), lambda bi, i: (i, 0)),
                  pl.BlockSpec((ts, LANES), lambda bi, i: (i, 0)),
                  pl.BlockSpec((ts, LANES), lambda bi, i: (i, 0))],
        out_specs=[pl.BlockSpec((1, ts, N_HEADS * LANES), lambda bi, i: (bi, i, 0)),
                   pl.BlockSpec((1, ts, N_HEADS * LANES), lambda bi, i: (bi, i, 0)),
                   pl.BlockSpec((1, MIX_W, ts), lambda bi, i: (bi, 0, i))],
        out_shape=[jax.ShapeDtypeStruct((b, s, N_HEADS * LANES), BF16),
                   jax.ShapeDtypeStruct((b, s, N_HEADS * LANES), BF16),
                   jax.ShapeDtypeStruct((b, MIX_W, s), BF16)],
        compiler_params=_cparams("parallel", "parallel"),
    )(y3, y3, y3, gcq, gckv, wuq, wuk, wuv, qg, kg, cos, sa, sb)


def _mla_kernel(q_ref, k_ref, vt_ref, o_ref, *, tq, tk):
    i = pl.program_id(1)
    qh = [q_ref[0, :, h * LANES:(h + 1) * LANES] for h in range(N_HEADS)]

    def scores(h, off):
        return _nt_dot(k_ref[0, pl.ds(off, tk), h * LANES:(h + 1) * LANES], qh[h])

    def values_t(h, off):
        return vt_ref[0, h * HEAD_DIM:(h + 1) * HEAD_DIM, pl.ds(off, tk)]

    for c in range(k_ref.shape[1] // tq):
        @pl.when(i == c)
        def _(c=c):
            o_ref[0] = _causal_flash_static(c, tq, tk, scores, values_t).T.astype(o_ref.dtype)


def mla_attention(q, k, vt, *, tq=256, tk=256):
    b, s, _ = q.shape
    return pl.pallas_call(
        functools.partial(_mla_kernel, tq=tq, tk=tk),
        grid=(b, s // tq),
        in_specs=[pl.BlockSpec((1, tq, N_HEADS * LANES), lambda bi, i: (bi, i, 0)),
                  pl.BlockSpec((1, s, N_HEADS * LANES), lambda bi, i: (bi, 0, 0)),
                  pl.BlockSpec((1, MIX_W, s), lambda bi, i: (bi, 0, 0))],
        out_specs=pl.BlockSpec((1, tq, MIX_W), lambda bi, i: (bi, i, 0)),
        out_shape=jax.ShapeDtypeStruct((b, s, MIX_W), BF16),
        compiler_params=_cparams("parallel", "arbitrary"),
    )(q, k, vt)


def _rel_bias_tile(dist, tab_ref, h):
    max_exact = REL_BUCKETS // 2
    d = jnp.maximum(dist, 0)
    large = max_exact + (jnp.log(jnp.maximum(d, 1).astype(F32) / max_exact)
                         / math.log(REL_MAX_DIST / max_exact)
                         * (REL_BUCKETS - max_exact)).astype(jnp.int32)
    large = jnp.minimum(large, REL_BUCKETS - 1)
    bucket = jnp.where(d < max_exact, d, large)
    out = jnp.zeros(dist.shape, F32)
    for bkt in range(REL_BUCKETS):
        out = jnp.where(bucket == bkt, tab_ref[bkt, h], out)
    return out


def _band_bias_kernel(tab_ref, slc_ref, win_ref):
    r = pl.program_id(0)
    kj = r * LANES + lax.broadcasted_iota(jnp.int32, (LANES, LANES), 0)
    qi = lax.broadcasted_iota(jnp.int32, (LANES, LANES), 1)
    dist = qi + NSA_WINDOW - kj
    for h in range(N_HEADS):
        delta = _rel_bias_tile(dist, tab_ref, h) - tab_ref[REL_BUCKETS - 1, h]
        slc_ref[h] = jnp.where(dist >= 0, delta, MASKED)
        win_ref[h] = jnp.where((dist >= 0) & (dist < NSA_WINDOW), delta, MASKED)


def _cmp_bias_kernel(tab_ref, o_ref):
    i = pl.program_id(0)
    c = lax.broadcasted_iota(jnp.int32, (LANES, LANES), 0)
    s = i * LANES + lax.broadcasted_iota(jnp.int32, (LANES, LANES), 1)
    dist = s - (c * NSA_CMP_STRIDE + NSA_CMP_LEN - 1)
    for h in range(N_HEADS):
        o_ref[h] = jnp.where(dist >= 0, _rel_bias_tile(dist, tab_ref, h), MASKED)


def rel_bias_tables(rel_bias, s):
    assert REL_BUCKETS == 32 and REL_MAX_DIST == 128 and NSA_WINDOW >= 113
    band = NSA_WINDOW + LANES
    smem = pl.BlockSpec(memory_space=pltpu.SMEM)
    band_spec = pl.BlockSpec((N_HEADS, LANES, LANES), lambda r: (0, r, 0))
    band_shape = jax.ShapeDtypeStruct((N_HEADS, band, LANES), F32)
    band_slc, band_win = pl.pallas_call(
        _band_bias_kernel, grid=(band // LANES,), in_specs=[smem],
        out_specs=[band_spec, band_spec], out_shape=[band_shape, band_shape],
    )(rel_bias)
    cmpb = pl.pallas_call(
        _cmp_bias_kernel, grid=(s // LANES,), in_specs=[smem],
        out_specs=pl.BlockSpec((N_HEADS, LANES, LANES), lambda i: (0, 0, i)),
        out_shape=jax.ShapeDtypeStruct((N_HEADS, LANES, s), F32),
    )(rel_bias)
    return band_slc, band_win, cmpb


def _nsa_compress_kernel(kv_ref, pea_ref, peb_ref, w1ka_ref, w1kb_ref, w1va_ref, w1vb_ref,
                         w2k_ref, w2v_ref, gk_ref, kc_ref, vct_ref):
    n_blk = kv_ref.shape[1] // NSA_CMP_STRIDE
    views = [kv_ref[0, pl.ds(l, n_blk, stride=NSA_CMP_STRIDE), :] for l in range(NSA_CMP_STRIDE)]
    first = jnp.concatenate([(views[l] + pea_ref[l:l + 1, :]).astype(BF16)
                             for l in range(NSA_CMP_STRIDE)], axis=1)
    second = jnp.concatenate([(views[l] + peb_ref[l:l + 1, :]).astype(BF16)
                              for l in range(NSA_CMP_STRIDE)], axis=1)

    def hidden(wa_ref, wb_ref):
        pre = _dot(first, wa_ref[...]) + pltpu.roll(_dot(second, wb_ref[...]), n_blk - 1, 0)
        return (pre * _sigmoid(pre)).astype(BF16)

    kc = _dot(hidden(w1ka_ref, w1kb_ref), w2k_ref[...])
    kc = kc * lax.rsqrt(jnp.sum(kc * kc, axis=-1, keepdims=True) / HEAD_DIM + EPS) * gk_ref[...]
    kc_ref[0] = kc.astype(kc_ref.dtype)
    vct_ref[0] = _nt_dot(w2v_ref[...], hidden(w1va_ref, w1vb_ref)).astype(vct_ref.dtype)


def nsa_compress(y3, pea, peb, w1ka, w1kb, w1va, w1vb, w2k, w2v, gk):
    b, s, _ = y3.shape
    assert s // NSA_CMP_STRIDE == LANES
    const = lambda a: pl.BlockSpec(a.shape, lambda bi: (0,) * a.ndim)
    out = pl.BlockSpec((1, LANES, LANES), lambda bi: (bi, 0, 0))
    consts = (pea, peb, w1ka, w1kb, w1va, w1vb, w2k, w2v, gk)
    return pl.pallas_call(
        _nsa_compress_kernel, grid=(b,),
        in_specs=[pl.BlockSpec((1, s, LANES), lambda bi: (bi, 0, COL_KCVC // LANES))]
        + [const(a) for a in consts],
        out_specs=[out, out],
        out_shape=[jax.ShapeDtypeStruct((b, LANES, LANES), BF16)] * 2,
        compiler_params=_cparams("parallel"),
    )(y3, *consts)


def _compress_weights(w1, lane0):
    w = w1.reshape(NSA_CMP_LEN, HEAD_DIM, HEAD_DIM)
    w = jnp.pad(w, ((0, 0), (lane0, LANES - HEAD_DIM - lane0), (0, LANES - HEAD_DIM)))
    w = w.reshape(2, NSA_CMP_STRIDE * LANES, LANES).astype(BF16)
    return w[0], w[1]


def _nsa_kernel(q_ref, k2_ref, v2_ref, misc_ref, kc_ref, vct_ref, bslc_ref, bwin_ref, cmpb_ref,
                gq_ref, gk2_ref, o_ref, ks_ref, kw_ref, vt_ref, win_ref, stage_ref):
    tq = LANES
    n_blk = q_ref.shape[1] // tq
    s = k2_ref.shape[1]
    pad = NSA_WINDOW
    band = NSA_WINDOW + tq
    n_sel = s // NSA_SEL_LEN
    assert NSA_SEL_LANE0 + n_sel <= LANES and pad % NSA_FAR_TILE == 0
    ip = pl.program_id(1)

    @pl.when(ip == 0)
    def _():
        lane_s = lax.broadcasted_iota(jnp.int32, (s, LANES), 1)
        row_s = lax.broadcasted_iota(jnp.int32, (s, LANES), 0)
        kn = _pair_rms(k2_ref[0], gk2_ref[...], lane_s < HEAD_DIM)
        sel_lane = NSA_SEL_LANE0 + (row_s >> 6)
        ks_ref[pad:, :] = jnp.where(lane_s < HEAD_DIM, kn,
                                    jnp.where(lane_s == sel_lane, 1.0, 0.0)).astype(BF16)
        kw_ref[pad:, :] = jnp.where(lane_s < HEAD_DIM, pltpu.roll(kn, HEAD_DIM, 1),
                                    0.0).astype(BF16)
        lane_p = lax.broadcasted_iota(jnp.int32, (pad, LANES), 1)
        before = jnp.where(lane_p == NSA_PAD_LANE, MASKED, 0.0).astype(BF16)
        ks_ref[:pad, :] = before
        kw_ref[:pad, :] = before
        vt_ref[:, pad:] = v2_ref[0].T.astype(BF16)
        vt_ref[:, :pad] = jnp.zeros((LANES, pad), BF16)

    lane = lax.broadcasted_iota(jnp.int32, (tq, LANES), 1)
    lo = lane < HEAD_DIM
    scale = HEAD_DIM ** -0.5
    kc = kc_ref[0]
    vct = vct_ref[0, :HEAD_DIM, :]
    jj = lax.broadcasted_iota(jnp.int32, (LANES, LANES), 0)
    c0 = lax.broadcasted_iota(jnp.int32, (LANES, LANES), 1) * NSA_CMP_STRIDE
    j0 = jj * NSA_SEL_LEN
    overlap = jnp.where((c0 < j0 + NSA_SEL_LEN) & (c0 + NSA_CMP_LEN > j0), 1.0, 0.0).astype(BF16)
    jj32 = lax.broadcasted_iota(jnp.int32, (n_sel, tq), 0)
    in_sel = (lane >= NSA_SEL_LANE0) & (lane < NSA_SEL_LANE0 + n_sel)
    blocks = [dict() for _ in range(n_blk)]

    def prepare(u):
        blk = blocks[u]
        i = ip * n_blk + u
        q = q_ref[0, u * tq:(u + 1) * tq, :]
        q_base = []
        for pair in range(2):
            cols = slice(pair * LANES, (pair + 1) * LANES)
            pn = _pair_rms(q[:, cols], gq_ref[:, cols], lo) * scale
            for head in (jnp.where(lo, pn, 0.0), pltpu.roll(jnp.where(lo, 0.0, pn), HEAD_DIM, 1)):
                q_base.append(jnp.where(lane == NSA_PAD_LANE, 1.0, head))
        q_plain = [x.astype(BF16) for x in q_base]
        b0 = pl.multiple_of(i * tq, tq)
        blk.update(i=i, q_base=q_base, q_plain=q_plain,
                   ks_band=ks_ref[pl.ds(b0, band), :],
                   vs_band=vt_ref[:HEAD_DIM, pl.ds(b0, band)],
                   vw_band=vt_ref[HEAD_DIM:, pl.ds(b0, band)])
        blk['s_cmp'] = [_nt_dot(kc, q_plain[h]) + cmpb_ref[h, :, u * tq:(u + 1) * tq]
                        for h in range(N_HEADS)]
        kw_band = kw_ref[pl.ds(b0, band), :]
        for h in range(N_HEADS):
            win_ref[u, h] = _nt_dot(kw_band, q_plain[h])

    def window(u, h):
        _, l_w, a_w = _first_t(win_ref[u, h] + bwin_ref[h], blocks[u]['vw_band'])
        return a_w / l_w

    def compressed(u):
        blk = blocks[u]
        o_cmp = []
        p_sum = jnp.zeros((LANES, tq), F32)
        for h in range(N_HEADS):
            s_c = blk['s_cmp'][h]
            e_c = jnp.where(s_c > 0.5 * MASKED,
                            jnp.exp(s_c - jnp.max(s_c, axis=0, keepdims=True)), 0.0)
            den = jnp.sum(e_c, axis=0, keepdims=True)
            p_c = e_c / jnp.where(den > 0.0, den, 1.0)
            o_cmp.append(_dot(vct, p_c.astype(BF16)))
            p_sum = p_sum + p_c
        hi, mid, lo3 = _split3(p_sum)
        blk['o_cmp'] = o_cmp
        blk['imp'] = (_dot(overlap, hi) + _dot(overlap, mid) + _dot(overlap, lo3))[0:n_sel]

    def select(u):
        blk = blocks[u]
        i = blk['i']
        cur = (i * tq + lax.broadcasted_iota(jnp.int32, (n_sel, tq), 1)) >> 6
        forced = (jj32 == 0) | (jj32 == cur) | (jj32 == cur - 1)
        imp = jnp.where(forced, FORCE, blk['imp'])
        imp = jnp.where(jj32 <= cur, imp, -FORCE)
        cnt = jnp.zeros((n_sel, tq), F32)
        for jp in range(n_sel):
            other = imp[jp:jp + 1, :]
            beats = (other > imp) | ((other == imp) & (jj32 > jp))
            cnt = cnt + jnp.where(beats, 1.0, 0.0)
        sel_neg = jnp.where(cnt < float(NSA_TOP_N), 0.0, MASKED)
        sel_neg = jnp.concatenate([sel_neg, jnp.zeros((LANES - n_sel, tq), F32)], axis=0).T
        sel_neg = pltpu.roll(sel_neg, NSA_SEL_LANE0, 1)
        band_block0 = NSA_SEL_LANE0 + ((i * tq - pad) >> 6)
        blk['q_band'] = [jnp.where(in_sel, sel_neg, x).astype(BF16) for x in blk['q_base']]
        blk['q_far'] = [jnp.where(in_sel, jnp.where(lane >= band_block0, MASKED, sel_neg),
                                  x).astype(BF16) for x in blk['q_base']]

    def stage_selected(u, h):
        stage_ref[u, h % 2] = _nt_dot(blocks[u]['ks_band'], blocks[u]['q_band'][h])

    def selected_band(u):
        blk = blocks[u]
        slc = []
        for h in range(N_HEADS):
            slc.append(_first_t(stage_ref[u, h % 2] + bslc_ref[h], blk['vs_band']))
            if h + 2 < N_HEADS:
                stage_selected(u, h + 2)
        blk['slc'] = slc

    def far_tile(u, kt):
        blk = blocks[u]
        off = pad + kt * NSA_FAR_TILE
        k_far = ks_ref[pl.ds(off, NSA_FAR_TILE), :]
        v_far = vt_ref[:HEAD_DIM, pl.ds(off, NSA_FAR_TILE)]
        q_far = blk['q_far']
        sc = [_nt_dot(k_far, q_far[0]), _nt_dot(k_far, q_far[1])]
        out = []
        for h in range(N_HEADS):
            if h + 2 < N_HEADS:
                sc.append(_nt_dot(k_far, q_far[h + 2]))
            out.append(_update_t(blk['slc'][h], sc[h], v_far, None))
        blk['slc'] = out

    def finish(u):
        blk = blocks[u]
        rows = slice(u * tq, (u + 1) * tq)
        g_t = _sigmoid(misc_ref[0, rows, :]).T
        heads = []
        for h in range(N_HEADS):
            _, l_s, a_s = blk['slc'][h]
            row = lambda n: g_t[MISC_G + n * N_HEADS + h:MISC_G + n * N_HEADS + h + 1, :]
            heads.append(row(0) * blk['o_cmp'][h] + row(1) * (a_s / l_s) + row(2) * blk['o_win'][h])
        o_ref[0, rows, :] = jnp.concatenate(heads, axis=0).T.astype(o_ref.dtype)

    both = range(n_blk)
    for u in both:
        prepare(u)
    for u in both:
        compressed(u)
    for u in both:
        blocks[u]['o_win'] = [window(u, 0), window(u, 1)]
    for u in both:
        select(u)
    for u in both:
        stage_selected(u, 0)
        stage_selected(u, 1)
    for u in both:
        blocks[u]['o_win'] += [window(u, 2), window(u, 3)]
    for u in both:
        selected_band(u)

    for c in range(s // (n_blk * tq)):
        @pl.when(ip == c)
        def _(c=c):
            saved = [blocks[u]['slc'] for u in both]
            n_far = [(max((c * n_blk + u) * tq - pad, 0) + NSA_FAR_TILE - 1) // NSA_FAR_TILE
                     for u in both]
            for kt in range(max(n_far)):
                for u in both:
                    if kt < n_far[u]:
                        far_tile(u, kt)
            for u in both:
                finish(u)
                blocks[u]['slc'] = saved[u]


def nsa_attention(y3, kc, vct, band_slc, band_win, cmpb, gq, gk2):
    b, s, _ = y3.shape
    n_blk = 2
    tq = n_blk * LANES
    band = NSA_WINDOW + LANES
    return pl.pallas_call(
        _nsa_kernel,
        grid=(b, s // tq),
        in_specs=[pl.BlockSpec((1, tq, MIX_W), lambda bi, i: (bi, i, COL_NSAQ // MIX_W)),
                  pl.BlockSpec((1, s, LANES), lambda bi, i: (bi, 0, COL_K2 // LANES)),
                  pl.BlockSpec((1, s, LANES), lambda bi, i: (bi, 0, COL_V2 // LANES)),
                  pl.BlockSpec((1, tq, LANES), lambda bi, i: (bi, i, COL_MISC // LANES)),
                  pl.BlockSpec((1, LANES, LANES), lambda bi, i: (bi, 0, 0)),
                  pl.BlockSpec((1, LANES, LANES), lambda bi, i: (bi, 0, 0)),
                  pl.BlockSpec(band_slc.shape, lambda bi, i: (0, 0, 0)),
                  pl.BlockSpec(band_win.shape, lambda bi, i: (0, 0, 0)),
                  pl.BlockSpec((N_HEADS, LANES, tq), lambda bi, i: (0, 0, i)),
                  pl.BlockSpec((1, MIX_W), lambda bi, i: (0, 0)),
                  pl.BlockSpec((1, LANES), lambda bi, i: (0, 0))],
        out_specs=pl.BlockSpec((1, tq, MIX_W), lambda bi, i: (bi, i, 0)),
        out_shape=jax.ShapeDtypeStruct((b, s, MIX_W), BF16),
        scratch_shapes=[pltpu.VMEM((s + NSA_WINDOW, LANES), BF16),
                        pltpu.VMEM((s + NSA_WINDOW, LANES), BF16),
                        pltpu.VMEM((LANES, s + NSA_WINDOW), BF16),
                        pltpu.VMEM((n_blk, N_HEADS, band, LANES), F32),
                        pltpu.VMEM((n_blk, 2, band, LANES), F32)],
        compiler_params=_cparams("parallel", "arbitrary"),
    )(y3, y3, y3, y3, kc, vct, band_slc, band_win, cmpb, gq, gk2)


def _merge_kernel(ysb_ref, ymla_ref, ynsa_ref, yfox_ref, g0_ref, g1_ref, g2_ref, g3_ref,
                  wb_ref, wo_ref, x_ref, o_ref):
    u = None
    for n, (y_ref, g_ref) in enumerate(((ysb_ref, g0_ref), (ymla_ref, g1_ref),
                                        (ynsa_ref, g2_ref), (yfox_ref, g3_ref))):
        term = _sigmoid(g_ref[...].astype(F32)) * _dot(y_ref[...], wb_ref[n])
        u = term if u is None else u + term
    o_ref[...] = x_ref[...] + _dot(u.astype(BF16), wo_ref[...])


def merge_branches(ys, y, wb_all, wo_all, layer, x, *, tm=512):
    t, d = x.shape
    gate_blk = COL_GATE // d
    yspec = pl.BlockSpec((tm, MIX_W), lambda i: (i, 0))
    gspecs = [pl.BlockSpec((tm, d), lambda i, n=n: (i, gate_blk + n)) for n in range(N_BRANCH)]
    wb, wo = wb_all, wo_all
    return pl.pallas_call(
        _merge_kernel, grid=(t // tm,),
        in_specs=[yspec] * 4 + gspecs + [
            pl.BlockSpec((None,) + wb.shape[1:], lambda i: (layer, 0, 0, 0)),
            pl.BlockSpec((None, d, d), lambda i: (layer, 0, 0)),
            pl.BlockSpec((tm, d), lambda i: (i, 0))],
        out_specs=pl.BlockSpec((tm, d), lambda i: (i, 0)),
        out_shape=jax.ShapeDtypeStruct((t, d), F32),
        compiler_params=_cparams("parallel"),
    )(*ys, y, y, y, y, wb, wo, x)


def _mlp_ple_kernel(x_ref, g_ref, wu_ref, wd_ref, gp_ref, wg_ref, p_ref, wp_ref, o_ref,
                    h_ref, acc_ref):
    f = pl.program_id(1)

    @pl.when(f == 0)
    def _():
        h_ref[...] = _rms_rows(x_ref[...], g_ref[...]).astype(BF16)
        acc_ref[...] = jnp.zeros_like(acc_ref)

    a = jnp.maximum(_dot(h_ref[...], wu_ref[...]), 0.0)
    acc_ref[...] += _dot((a * a).astype(BF16), wd_ref[...])

    @pl.when(f == pl.num_programs(1) - 1)
    def _():
        x1 = x_ref[...] + acc_ref[...]
        gate = _sigmoid(_dot(_rms_rows(x1, gp_ref[...]).astype(BF16), wg_ref[...]))
        o_ref[...] = x1 + gate * _dot(p_ref[...].astype(BF16), wp_ref[...])


def mlp_ple(x, g, wu, wd, gp, wg, p_all, wp, layer, *, tm=1024, tf=1024):
    t, d = x.shape
    ff = wu.shape[2]
    return pl.pallas_call(
        _mlp_ple_kernel, grid=(t // tm, ff // tf),
        in_specs=[pl.BlockSpec((tm, d), lambda i, f: (i, 0)),
                  pl.BlockSpec((1, d), lambda i, f: (0, 0)),
                  pl.BlockSpec((None, d, tf), lambda i, f: (layer, 0, f)),
                  pl.BlockSpec((None, tf, d), lambda i, f: (layer, f, 0)),
                  pl.BlockSpec((1, d), lambda i, f: (0, 0)),
                  pl.BlockSpec((None, d, d), lambda i, f: (layer, 0, 0)),
                  pl.BlockSpec((None, tm, PLE_DIM), lambda i, f: (layer, i, 0)),
                  pl.BlockSpec((None, PLE_DIM, d), lambda i, f: (layer, 0, 0))],
        out_specs=pl.BlockSpec((tm, d), lambda i, f: (i, 0)),
        out_shape=jax.ShapeDtypeStruct((t, d), F32),
        scratch_shapes=[pltpu.VMEM((tm, d), BF16), pltpu.VMEM((tm, d), F32)],
        compiler_params=_cparams("parallel", "arbitrary"),
    )(x, g.reshape(1, d), wu, wd, gp.reshape(1, d), wg, p_all, wp)


def _pack_w_in(w):
    offs = np.concatenate([[0], np.cumsum(IN_WIDTHS)]).tolist()
    (sb_q, sb_k, sb_v, cq, ckv, kr, nsa_q, kc, vc, ks, vs, kw, vw, ng,
     fox_q, fox_k, fox_v, ff, gate) = [w[..., offs[n]:offs[n + 1]] for n in range(len(IN_WIDTHS))]
    z = lambda n: jnp.zeros(w.shape[:-1] + (n,), w.dtype)
    misc = jnp.concatenate([ff, ng, z(MISC_KR - MISC_G - 3 * N_HEADS), kr,
                            z(LANES - MISC_KR - MLA_ROPE)], axis=-1)
    w_f32_part = jnp.concatenate([cq, ckv, ks, kw, vs, vw, kc, vc, misc, nsa_q], axis=-1)
    w_bf16_part = jnp.concatenate([gate, sb_q, sb_k, sb_v, fox_q, fox_k, fox_v], axis=-1)
    return w_f32_part.astype(BF16), w_bf16_part.astype(BF16)


def _head_slots(w, width):
    k = w.shape[0]
    w = w.reshape(k, N_HEADS, width)
    return jnp.pad(w, ((0, 0), (0, 0), (0, LANES - width))).reshape(k, N_HEADS * LANES)


def _rope_tables(s):
    half = MLA_ROPE // 2
    inv = jnp.exp(-math.log(ROPE_THETA) * jnp.arange(half, dtype=F32) / half)
    ang = jnp.arange(s, dtype=F32)[:, None] * inv[None, :]
    cos, sin = jnp.cos(ang), jnp.sin(ang)
    ones = jnp.ones((s, MLA_NOPE), F32)
    zeros = lambda n: jnp.zeros((s, n), F32)
    tail = LANES - MLA_QK
    cos_t = jnp.concatenate([ones, cos, cos, jnp.ones((s, tail), F32)], axis=1)
    sa_t = jnp.concatenate([zeros(MLA_NOPE), -sin, zeros(half), zeros(tail)], axis=1)
    sb_t = jnp.concatenate([zeros(MLA_NOPE), zeros(half), sin, zeros(tail)], axis=1)
    return cos_t, sa_t, sb_t


def _pad_lanes(v, left=0):
    v = v.reshape(1, -1)
    return jnp.pad(v, ((0, 0), (left, LANES - left - v.shape[1])))


def _pad_to_lanes(w):
    return jnp.pad(w, ((0, 0), (0, LANES - w.shape[1])))


def kernel(x, p, rel_bias, norm_mix_g, w_in, mla_cq_norm_g, mla_ckv_norm_g, mla_w_uq, mla_w_ukv,
           mla_qn_g, mla_kn_g, nsa_pe_k, nsa_pe_v, nsa_w1_k, nsa_w2_k, nsa_w1_v, nsa_w2_v,
           nsa_qn_g, nsa_kn_g, fox_f_bias, fox_qn_g, fox_kn_g, w_branch, w_o, norm_mlp_g,
           w_mlp_up, w_mlp_down, norm_ple_g, w_ple_gate, w_ple_proj):
    b, s, d = x.shape
    t = b * s
    xf = x.reshape(t, d)
    p_all = p.reshape(DEPTH, t, PLE_DIM)
    band_slc, band_win, cmpb = rel_bias_tables(rel_bias.astype(F32), s)
    cos_t, sa_t, sb_t = _rope_tables(s)
    n_cmp_in = NSA_CMP_STRIDE * HEAD_DIM
    w_a, w_b = _pack_w_in(w_in)
    wb_all, wo_all = w_branch.astype(BF16), w_o.astype(BF16)
    wu_all, wd_all = w_mlp_up.astype(BF16), w_mlp_down.astype(BF16)
    wg_all, wp_all = w_ple_gate.astype(BF16), w_ple_proj.astype(BF16)

    for i in range(DEPTH):
        ya, h = norm_proj(xf, norm_mix_g[i], w_a, i)
        yb = matmul_bf16(h, w_b, i)
        y3 = ya.reshape(b, s, N_F32)
        yb3 = yb.reshape(b, s, N_BF16)

        y_sb = sb_attention(yb3)

        wukv = mla_w_ukv[i].reshape(MLA_KV_LORA, N_HEADS, MLA_NOPE + MLA_V)
        q_m, k_m, vt_m = mla_prep(
            y3, mla_cq_norm_g[i].reshape(1, -1), mla_ckv_norm_g[i].reshape(1, -1),
            _head_slots(mla_w_uq[i], MLA_QK).astype(BF16),
            _head_slots(wukv[:, :, :MLA_NOPE].reshape(MLA_KV_LORA, -1), MLA_NOPE).astype(BF16),
            wukv[:, :, MLA_NOPE:].reshape(MLA_KV_LORA, -1).T.astype(BF16),
            _pad_lanes(mla_qn_g[i]), _pad_lanes(mla_kn_g[i]), cos_t, sa_t, sb_t)
        y_mla = mla_attention(q_m, k_m, vt_m)

        pad_sq = lambda w: jnp.pad(_pad_to_lanes(w), ((0, LANES - w.shape[0]), (0, 0)))
        pe_kv = jnp.concatenate([nsa_pe_k[i], nsa_pe_v[i]], axis=1)
        kc, vct = nsa_compress(
            y3, pe_kv[:NSA_CMP_STRIDE], pe_kv[NSA_CMP_STRIDE:],
            *_compress_weights(nsa_w1_k[i], 0), *_compress_weights(nsa_w1_v[i], HEAD_DIM),
            pad_sq(nsa_w2_k[i]).astype(BF16), pad_sq(nsa_w2_v[i].T).astype(BF16),
            _pad_lanes(nsa_kn_g[i, 0]))
        y_nsa = nsa_attention(
            y3, kc, vct, band_slc, band_win, cmpb,
            jnp.tile(nsa_qn_g[i], N_HEADS).reshape(1, -1),
            jnp.concatenate([nsa_kn_g[i, 1], nsa_kn_g[i, 2]]).reshape(1, -1))

        cum = fox_cum(y3, _pad_lanes(fox_f_bias[i], MISC_F))
        y_fox = fox_attention(yb3, cum, jnp.tile(fox_qn_g[i], N_HEADS).reshape(1, -1),
                              jnp.tile(fox_kn_g[i], N_HEADS).reshape(1, -1))

        ys = [a.reshape(t, MIX_W) for a in (y_sb, y_mla, y_nsa, y_fox)]
        xf = merge_branches(ys, yb, wb_all, wo_all, i, xf)
        xf = mlp_ple(xf, norm_mlp_g[i], wu_all, wd_all, norm_ple_g[i], wg_all, p_all, wp_all, i)
    return xf.reshape(b, s, d)
```

```python
import functools
import math

import numpy as np
import jax
import jax.numpy as jnp
from jax import lax
from jax.experimental import pallas as pl
from jax.experimental.pallas import tpu as pltpu

F32 = jnp.float32
BF16 = jnp.bfloat16

D_MODEL = 1024
DEPTH = 4
HEAD_DIM = 64
N_HEADS = 4
MIX_W = N_HEADS * HEAD_DIM
N_BRANCH = 4
EPS = 1e-6
FORCE = 1e9
MLA_Q_LORA = 384
MLA_KV_LORA = 128
MLA_NOPE = 64
MLA_ROPE = 32
MLA_V = 64
MLA_QK = MLA_NOPE + MLA_ROPE
ROPE_THETA = 10000.0
NSA_CMP_LEN = 32
NSA_CMP_STRIDE = 16
NSA_SEL_LEN = 64
NSA_TOP_N = 16
NSA_WINDOW = 512
REL_BUCKETS = 32
REL_MAX_DIST = 128
D_FF = 4 * D_MODEL
PLE_DIM = 256

LANES = 128
MASKED = -1e30
VMEM_LIMIT = 56 * 1024 * 1024

IN_WIDTHS = ((MIX_W,) * 3
             + (MLA_Q_LORA, MLA_KV_LORA, MLA_ROPE)
             + (MIX_W,) + (HEAD_DIM,) * 6 + (3 * N_HEADS,)
             + (MIX_W,) * 3 + (N_HEADS,)
             + (N_BRANCH * D_MODEL,))

COL_CQ = 0
COL_CKV = 384
COL_K2 = 512
COL_V2 = 640
COL_KCVC = 768
COL_MISC = 896
COL_NSAQ = 1024
N_F32 = 1280
COL_SB = 0
COL_FOX = 768
N_BF16 = 1536
MISC_F = 0
MISC_G = 4
MISC_KR = 32

NSA_PAD_LANE = HEAD_DIM
NSA_SEL_LANE0 = HEAD_DIM + 1
NSA_FAR_TILE = 512


def _cparams(*sem):
    return pltpu.CompilerParams(dimension_semantics=sem, vmem_limit_bytes=VMEM_LIMIT)


def _nt_dot(a, b):
    return lax.dot_general(a, b, (((1,), (1,)), ((), ())), preferred_element_type=F32)


def _dot(a, b):
    return jnp.dot(a, b, preferred_element_type=F32)


def _split3(x):
    hi = x.astype(BF16)
    r1 = x - hi.astype(F32)
    mid = r1.astype(BF16)
    lo = (r1 - mid.astype(F32)).astype(BF16)
    return hi, mid, lo


def _softplus(z):
    return jnp.maximum(z, 0.0) + jnp.log(1.0 + jnp.exp(-jnp.abs(z)))


def _sigmoid(z):
    return 1.0 / (1.0 + jnp.exp(-z))


def _rms_rows(x, g):
    r = lax.rsqrt(jnp.mean(x * x, axis=-1, keepdims=True) + EPS)
    return x * r * g


def _pair_rms(x, g, lo):
    x2 = x * x
    s0 = jnp.sum(jnp.where(lo, x2, 0.0), axis=-1, keepdims=True)
    s1 = jnp.sum(jnp.where(lo, 0.0, x2), axis=-1, keepdims=True)
    r = jnp.where(lo, lax.rsqrt(s0 / HEAD_DIM + EPS), lax.rsqrt(s1 / HEAD_DIM + EPS))
    return x * r * g


def _first_t(s, vt):
    m = jnp.max(s, axis=0, keepdims=True)
    p = jnp.exp(s - m)
    return m, jnp.sum(p, axis=0, keepdims=True), _dot(vt, p.astype(BF16))


def _update_t(carry, s, vt, mask):
    m, l, acc = carry
    if mask is not None:
        s = jnp.where(mask, s, MASKED)
    m_new = jnp.maximum(m, jnp.max(s, axis=0, keepdims=True))
    p = jnp.exp(s - m_new)
    if mask is not None:
        p = jnp.where(mask, p, 0.0)
    alpha = jnp.exp(m - m_new)
    l = alpha * l + jnp.sum(p, axis=0, keepdims=True)
    acc = alpha * acc + _dot(vt, p.astype(BF16))
    return m_new, l, acc


def _init_t(tq):
    return (jnp.full((1, tq), MASKED, F32), jnp.zeros((1, tq), F32),
            jnp.zeros((HEAD_DIM, tq), F32))


def _causal_flash_static(c, tq, tk, score_fn, vt_fn):
    assert tk % tq == 0
    heads = range(N_HEADS)
    diff = (lax.broadcasted_iota(jnp.int32, (tk, tq), 0)
            - lax.broadcasted_iota(jnp.int32, (tk, tq), 1))
    n_full = (c * tq) // tk
    cur = [score_fn(h, 0) for h in heads]
    state = [_init_t(tq) for _ in heads]
    for kt in range(n_full + 1):
        nxt = [score_fn(h, (kt + 1) * tk) for h in heads] if kt < n_full else None
        mask = (diff <= c * tq - kt * tk) if kt == n_full else None
        state = [_update_t(state[h], cur[h], vt_fn(h, kt * tk), mask) for h in heads]
        cur = nxt
    return jnp.concatenate([acc / l for (_, l, acc) in state], axis=0)


def _augment(x, terms, col, lane, h, key_side):
    live = (lane < HEAD_DIM) if h == 0 else (lane >= HEAD_DIM)
    a0 = HEAD_DIM if h == 0 else 0
    c0, o0 = (a0, a0 + 3) if key_side else (a0 + 3, a0)
    src = lax.broadcasted_iota(jnp.int32, (3 * LANES, LANES), 0)
    dst = lax.broadcasted_iota(jnp.int32, (3 * LANES, LANES), 1)
    place = jnp.where(((src & (LANES - 1)) == col) & (dst == c0 + (src >> 7)),
                      -1.0 if key_side else 1.0, 0.0).astype(BF16)
    ones = jnp.where((lane >= o0) & (lane < o0 + 3), 1.0, 0.0)
    return jnp.where(live, x, _dot(terms, place) + ones)


def _norm_proj_kernel(x_ref, g_ref, w_ref, o_ref, h_ref):
    h = _rms_rows(x_ref[...], g_ref[...]).astype(BF16)
    h_ref[...] = h
    o_ref[...] = _dot(h, w_ref[...])


def norm_proj(x, g, w_all, layer, *, tm=1024):
    t, d = x.shape
    n = w_all.shape[2]
    return pl.pallas_call(
        _norm_proj_kernel,
        grid=(t // tm,),
        in_specs=[pl.BlockSpec((tm, d), lambda i: (i, 0)),
                  pl.BlockSpec((1, d), lambda i: (0, 0)),
                  pl.BlockSpec((None, d, n), lambda i: (layer, 0, 0))],
        out_specs=[pl.BlockSpec((tm, n), lambda i: (i, 0)),
                   pl.BlockSpec((tm, d), lambda i: (i, 0))],
        out_shape=[jax.ShapeDtypeStruct((t, n), F32), jax.ShapeDtypeStruct((t, d), BF16)],
        compiler_params=_cparams("parallel"),
    )(x, g.reshape(1, d), w_all)


def _matmul_kernel(h_ref, w_ref, o_ref):
    o_ref[...] = _dot(h_ref[...], w_ref[...]).astype(o_ref.dtype)


def matmul_bf16(h, w_all, layer, *, tm=2048, tn=512):
    t, d = h.shape
    n = w_all.shape[2]
    return pl.pallas_call(
        _matmul_kernel,
        grid=(t // tm, n // tn),
        in_specs=[pl.BlockSpec((tm, d), lambda i, j: (i, 0)),
                  pl.BlockSpec((None, d, tn), lambda i, j: (layer, 0, j))],
        out_specs=pl.BlockSpec((tm, tn), lambda i, j: (i, j)),
        out_shape=jax.ShapeDtypeStruct((t, n), BF16),
        compiler_params=_cparams("parallel", "parallel"),
    )(h, w_all)


def _sb_kernel(q_ref, k_ref, v_ref, o_ref, vt_ref, *, tq, tk):
    i = pl.program_id(1)

    @pl.when(i == 0)
    def _():
        vt_ref[...] = v_ref[0].astype(F32).T.astype(BF16)

    sub = LANES
    n_sub = tk // sub
    heads = range(N_HEADS)
    lane = lax.broadcasted_iota(jnp.int32, (tq, LANES), 1)
    lo = lane < HEAD_DIM
    qh = []
    for p in range(N_HEADS // 2):
        q = q_ref[0, :, p * LANES:(p + 1) * LANES].astype(F32) * (HEAD_DIM ** -0.5)
        qh += [jnp.where(lo, q, 0.0).astype(BF16), jnp.where(lo, 0.0, q).astype(BF16)]
    rr = lax.broadcasted_iota(jnp.int32, (sub, sub), 0)
    cc = lax.broadcasted_iota(jnp.int32, (sub, sub), 1)
    upper = jnp.where(cc >= rr, 1.0, 0.0).astype(BF16)
    upper2 = jnp.concatenate([upper, upper], axis=1)
    diff = (lax.broadcasted_iota(jnp.int32, (tk, tq), 0)
            - lax.broadcasted_iota(jnp.int32, (tk, tq), 1))

    def logits_at(kt):
        return tuple(_nt_dot(k_ref[0, pl.ds(kt * tk, tk), (h // 2) * LANES:(h // 2 + 1) * LANES],
                             qh[h]) for h in heads)

    def tile(c, kt, zs, carry, masked):
        off = kt * tk
        past = (diff < c * tq - off) if masked else None
        out = []
        for h in heads:
            run, acc = carry[h]
            z = zs[h]
            sp = _softplus(z)
            spm = jnp.where(past, sp, 0.0) if masked else sp
            ws = [None] * n_sub
            for j in reversed(range(n_sub)):
                sl = slice(j * sub, (j + 1) * sub)
                spj = spm[sl]
                hi = spj.astype(BF16)
                mid = (spj - hi.astype(F32)).astype(BF16)
                tail = _dot(upper2, jnp.concatenate([hi, mid], axis=0))
                w = jnp.exp(z[sl] - tail - run)
                if masked:
                    w = jnp.where(past[sl], w, 0.0)
                ws[j] = w.astype(BF16)
                run = run + tail[0:1, :]
            vt = vt_ref[h * HEAD_DIM:(h + 1) * HEAD_DIM, pl.ds(off, tk)]
            out.append((run, acc + _dot(vt, jnp.concatenate(ws, axis=0))))
        return tuple(out)

    zero = (jnp.zeros((1, tq), F32), jnp.zeros((HEAD_DIM, tq), F32))
    for c in range(k_ref.shape[1] // tq):
        @pl.when(i == c)
        def _(c=c):
            n_full = (c * tq) // tk
            zs = logits_at(n_full)
            carry = (zero,) * N_HEADS
            for kt in range(n_full, -1, -1):
                zs_next = logits_at(kt - 1) if kt > 0 else None
                carry = tile(c, kt, zs, carry, kt == n_full)
                zs = zs_next
            o_t = jnp.concatenate([acc for _, acc in carry], axis=0)
            o_ref[0] = o_t.T.astype(o_ref.dtype)


def sb_attention(y3, *, tq=256, tk=256):
    b, s, _ = y3.shape
    qb = COL_SB // MIX_W
    return pl.pallas_call(
        functools.partial(_sb_kernel, tq=tq, tk=tk),
        grid=(b, s // tq),
        in_specs=[pl.BlockSpec((1, tq, MIX_W), lambda bi, i: (bi, i, qb)),
                  pl.BlockSpec((1, s, MIX_W), lambda bi, i: (bi, 0, qb + 1)),
                  pl.BlockSpec((1, s, MIX_W), lambda bi, i: (bi, 0, qb + 2))],
        out_specs=pl.BlockSpec((1, tq, MIX_W), lambda bi, i: (bi, i, 0)),
        out_shape=jax.ShapeDtypeStruct((b, s, MIX_W), BF16),
        scratch_shapes=[pltpu.VMEM((MIX_W, s), BF16)],
        compiler_params=_cparams("parallel", "arbitrary"),
    )(y3, y3, y3)


def _fox_cum_kernel(misc_ref, fb_ref, cum_ref, *, s):
    rr = lax.broadcasted_iota(jnp.int32, (LANES, LANES), 0)
    cc = lax.broadcasted_iota(jnp.int32, (LANES, LANES), 1)
    lower = jnp.where(cc <= rr, 1.0, 0.0).astype(BF16)

    def body(n, carry):
        off = pl.multiple_of(n * LANES, LANES)
        x = misc_ref[0, pl.ds(off, LANES), :] + fb_ref[...]
        log_f = jnp.minimum(x, 0.0) - jnp.log(1.0 + jnp.exp(-jnp.abs(x)))
        hi, mid, lo3 = _split3(log_f)
        c = _dot(lower, hi) + _dot(lower, mid) + _dot(lower, lo3) + carry
        cum_ref[0, pl.ds(off, LANES), :] = c
        return c[LANES - 1:LANES, :]

    lax.fori_loop(0, s // LANES, body, jnp.zeros((1, LANES), F32))


def fox_cum(y3, fbias_row):
    b, s, _ = y3.shape
    return pl.pallas_call(
        functools.partial(_fox_cum_kernel, s=s),
        grid=(b,),
        in_specs=[pl.BlockSpec((1, s, LANES), lambda bi: (bi, 0, COL_MISC // LANES)),
                  pl.BlockSpec((1, LANES), lambda bi: (0, 0))],
        out_specs=pl.BlockSpec((1, s, LANES), lambda bi: (bi, 0, 0)),
        out_shape=jax.ShapeDtypeStruct((b, s, LANES), F32),
        compiler_params=_cparams("parallel"),
    )(y3, fbias_row)


def _fox_kernel(q_ref, k_ref, v_ref, cum_ref, gq_ref, gk_ref, o_ref, ka_ref, vt_ref, *, tq, tk):
    i = pl.program_id(1)
    s = k_ref.shape[1]
    pairs = range(N_HEADS // 2)

    @pl.when(i == 0)
    def _():
        lane_s = lax.broadcasted_iota(jnp.int32, (s, LANES), 1)
        terms = jnp.concatenate(_split3(cum_ref[0]), axis=1)
        for p in pairs:
            cols = slice(p * LANES, (p + 1) * LANES)
            kn = _pair_rms(k_ref[0, :, cols].astype(F32), gk_ref[:, cols], lane_s < HEAD_DIM)
            for h in range(2):
                ka_ref[:, (2 * p + h) * LANES:(2 * p + h + 1) * LANES] = _augment(
                    kn, terms, MISC_F + 2 * p + h, lane_s, h, True).astype(BF16)
        vt_ref[...] = v_ref[0].astype(F32).T.astype(BF16)

    lane = lax.broadcasted_iota(jnp.int32, (tq, LANES), 1)
    terms_q = jnp.concatenate(
        _split3(cum_ref[0, pl.ds(pl.multiple_of(i * tq, tq), tq), :]), axis=1)
    qa = []
    for p in pairs:
        cols = slice(p * LANES, (p + 1) * LANES)
        qn = _pair_rms(q_ref[0, :, cols].astype(F32), gq_ref[:, cols], lane < HEAD_DIM)
        qn = qn * (HEAD_DIM ** -0.5)
        qa += [_augment(qn, terms_q, MISC_F + 2 * p + h, lane, h, False).astype(BF16)
               for h in range(2)]

    def scores(h, off):
        return _nt_dot(ka_ref[pl.ds(off, tk), h * LANES:(h + 1) * LANES], qa[h])

    def values_t(h, off):
        return vt_ref[h * HEAD_DIM:(h + 1) * HEAD_DIM, pl.ds(off, tk)]

    for c in range(s // tq):
        @pl.when(i == c)
        def _(c=c):
            o_ref[0] = _causal_flash_static(c, tq, tk, scores, values_t).T.astype(o_ref.dtype)


def fox_attention(y3, cum, gq, gk, *, tq=256, tk=256):
    b, s, _ = y3.shape
    qb = COL_FOX // MIX_W
    return pl.pallas_call(
        functools.partial(_fox_kernel, tq=tq, tk=tk),
        grid=(b, s // tq),
        in_specs=[pl.BlockSpec((1, tq, MIX_W), lambda bi, i: (bi, i, qb)),
                  pl.BlockSpec((1, s, MIX_W), lambda bi, i: (bi, 0, qb + 1)),
                  pl.BlockSpec((1, s, MIX_W), lambda bi, i: (bi, 0, qb + 2)),
                  pl.BlockSpec((1, s, LANES), lambda bi, i: (bi, 0, 0)),
                  pl.BlockSpec((1, MIX_W), lambda bi, i: (0, 0)),
                  pl.BlockSpec((1, MIX_W), lambda bi, i: (0, 0))],
        out_specs=pl.BlockSpec((1, tq, MIX_W), lambda bi, i: (bi, i, 0)),
        out_shape=jax.ShapeDtypeStruct((b, s, MIX_W), BF16),
        scratch_shapes=[pltpu.VMEM((s, N_HEADS * LANES), BF16), pltpu.VMEM((MIX_W, s), BF16)],
        compiler_params=_cparams("parallel", "arbitrary"),
    )(y3, y3, y3, cum, gq, gk)


def _mla_prep_kernel(cq_ref, ckv_ref, misc_ref, gcq_ref, gckv_ref, wuq_ref, wuk_ref, wuv_ref,
                     qg_ref, kg_ref, cos_ref, sa_ref, sb_ref, q_out, k_out, v_out, *, ts):
    hq = _rms_rows(cq_ref[0], gcq_ref[...]).astype(BF16)
    hkv = _rms_rows(ckv_ref[0], gckv_ref[...]).astype(BF16)
    q = _dot(hq, wuq_ref[...])
    kn = _dot(hkv, wuk_ref[...])
    v_out[0] = _nt_dot(wuv_ref[...], hkv).astype(v_out.dtype)

    lane = lax.broadcasted_iota(jnp.int32, (ts, LANES), 1)
    misc = misc_ref[0]
    kr = pltpu.roll(jnp.where((lane >= MISC_KR) & (lane < MISC_KR + MLA_ROPE), misc, 0.0),
                    MLA_NOPE - MISC_KR, 1)
    cos, sin = cos_ref[...], sa_ref[...] + sb_ref[...]
    half = MLA_ROPE // 2

    kk = lax.broadcasted_iota(jnp.int32, (2 * LANES, LANES), 0) & (LANES - 1)
    ll = lax.broadcasted_iota(jnp.int32, (2 * LANES, LANES), 1)
    ones2 = jnp.ones((2 * LANES, LANES), BF16)
    first = (ll >= MLA_NOPE) & (ll < MLA_NOPE + half)
    second = (ll >= MLA_NOPE + half) & (ll < MLA_QK)
    swap2 = jnp.where((first & (kk == ll + half)) | (second & (kk == ll - half)),
                      1.0, 0.0).astype(BF16)

    def split2(x):
        hi = x.astype(BF16)
        return jnp.concatenate([hi, (x - hi.astype(F32)).astype(BF16)], axis=1)

    def norm_rope(t, g):
        t = t * lax.rsqrt(_dot(split2(t * t), ones2) / MLA_QK + EPS) * g
        return t * cos + _dot(split2(t), swap2) * sin

    for h in range(N_HEADS):
        sl = slice(h * LANES, (h + 1) * LANES)
        q_out[0, :, sl] = (norm_rope(q[:, sl], qg_ref[...]) * (MLA_QK ** -0.5)).astype(q_out.dtype)
        k_out[0, :, sl] = norm_rope(kn[:, sl] + kr, kg_ref[...]).astype(k_out.dtype)


def mla_prep(y3, gcq, gckv, wuq, wuk, wuv, qg, kg, cos, sa, sb, *, ts=512):
    b, s, _ = y3.shape
    const = lambda shape: pl.BlockSpec(shape, lambda bi, i: (0,) * len(shape))
    return pl.pallas_call(
        functools.partial(_mla_prep_kernel, ts=ts),
        grid=(b, s // ts),
        in_specs=[pl.BlockSpec((1, ts, MLA_Q_LORA), lambda bi, i: (bi, i, COL_CQ // MLA_Q_LORA)),
                  pl.BlockSpec((1, ts, LANES), lambda bi, i: (bi, i, COL_CKV // LANES)),
                  pl.BlockSpec((1, ts, LANES), lambda bi, i: (bi, i, COL_MISC // LANES)),
                  const((1, MLA_Q_LORA)), const((1, MLA_KV_LORA)),
                  const((MLA_Q_LORA, N_HEADS * LANES)), const((MLA_KV_LORA, N_HEADS * LANES)),
                  const((MIX_W, MLA_KV_LORA)), const((1, LANES)), const((1, LANES)),
                  pl.BlockSpec((ts, LANES), lambda bi, i: (i, 0)),
                  pl.BlockSpec((ts, LANES), lambda bi, i: (i, 0)),
                  pl.BlockSpec((ts, LANES), lambda bi, i: (i, 0))],
        out_specs=[pl.BlockSpec((1, ts, N_HEADS * LANES), lambda bi, i: (bi, i, 0)),
                   pl.BlockSpec((1, ts, N_HEADS * LANES), lambda bi, i: (bi, i, 0)),
                   pl.BlockSpec((1, MIX_W, ts), lambda bi, i: (bi, 0, i))],
        out_shape=[jax.ShapeDtypeStruct((b, s, N_HEADS * LANES), BF16),
                   jax.ShapeDtypeStruct((b, s, N_HEADS * LANES), BF16),
                   jax.ShapeDtypeStruct((b, MIX_W, s), BF16)],
        compiler_params=_cparams("parallel", "parallel"),
    )(y3, y3, y3, gcq, gckv, wuq, wuk, wuv, qg, kg, cos, sa, sb)


def _mla_kernel(q_ref, k_ref, vt_ref, o_ref, *, tq, tk):
    i = pl.program_id(1)
    qh = [q_ref[0, :, h * LANES:(h + 1) * LANES] for h in range(N_HEADS)]

    def scores(h, off):
        return _nt_dot(k_ref[0, pl.ds(off, tk), h * LANES:(h + 1) * LANES], qh[h])

    def values_t(h, off):
        return vt_ref[0, h * HEAD_DIM:(h + 1) * HEAD_DIM, pl.ds(off, tk)]

    for c in range(k_ref.shape[1] // tq):
        @pl.when(i == c)
        def _(c=c):
            o_ref[0] = _causal_flash_static(c, tq, tk, scores, values_t).T.astype(o_ref.dtype)


def mla_attention(q, k, vt, *, tq=256, tk=256):
    b, s, _ = q.shape
    return pl.pallas_call(
        functools.partial(_mla_kernel, tq=tq, tk=tk),
        grid=(b, s // tq),
        in_specs=[pl.BlockSpec((1, tq, N_HEADS * LANES), lambda bi, i: (bi, i, 0)),
                  pl.BlockSpec((1, s, N_HEADS * LANES), lambda bi, i: (bi, 0, 0)),
                  pl.BlockSpec((1, MIX_W, s), lambda bi, i: (bi, 0, 0))],
        out_specs=pl.BlockSpec((1, tq, MIX_W), lambda bi, i: (bi, i, 0)),
        out_shape=jax.ShapeDtypeStruct((b, s, MIX_W), BF16),
        compiler_params=_cparams("parallel", "arbitrary"),
    )(q, k, vt)


def _rel_bias_tile(dist, tab_ref, h):
    max_exact = REL_BUCKETS // 2
    d = jnp.maximum(dist, 0)
    large = max_exact + (jnp.log(jnp.maximum(d, 1).astype(F32) / max_exact)
                         / math.log(REL_MAX_DIST / max_exact)
                         * (REL_BUCKETS - max_exact)).astype(jnp.int32)
    large = jnp.minimum(large, REL_BUCKETS - 1)
    bucket = jnp.where(d < max_exact, d, large)
    out = jnp.zeros(dist.shape, F32)
    for bkt in range(REL_BUCKETS):
        out = jnp.where(bucket == bkt, tab_ref[bkt, h], out)
    return out


def _band_bias_kernel(tab_ref, slc_ref, win_ref):
    r = pl.program_id(0)
    kj = r * LANES + lax.broadcasted_iota(jnp.int32, (LANES, LANES), 0)
    qi = lax.broadcasted_iota(jnp.int32, (LANES, LANES), 1)
    dist = qi + NSA_WINDOW - kj
    for h in range(N_HEADS):
        delta = _rel_bias_tile(dist, tab_ref, h) - tab_ref[REL_BUCKETS - 1, h]
        slc_ref[h] = jnp.where(dist >= 0, delta, MASKED)
        win_ref[h] = jnp.where((dist >= 0) & (dist < NSA_WINDOW), delta, MASKED)


def _cmp_bias_kernel(tab_ref, o_ref):
    i = pl.program_id(0)
    c = lax.broadcasted_iota(jnp.int32, (LANES, LANES), 0)
    s = i * LANES + lax.broadcasted_iota(jnp.int32, (LANES, LANES), 1)
    dist = s - (c * NSA_CMP_STRIDE + NSA_CMP_LEN - 1)
    for h in range(N_HEADS):
        o_ref[h] = jnp.where(dist >= 0, _rel_bias_tile(dist, tab_ref, h), MASKED)


def rel_bias_tables(rel_bias, s):
    assert REL_BUCKETS == 32 and REL_MAX_DIST == 128 and NSA_WINDOW >= 113
    band = NSA_WINDOW + LANES
    smem = pl.BlockSpec(memory_space=pltpu.SMEM)
    band_spec = pl.BlockSpec((N_HEADS, LANES, LANES), lambda r: (0, r, 0))
    band_shape = jax.ShapeDtypeStruct((N_HEADS, band, LANES), F32)
    band_slc, band_win = pl.pallas_call(
        _band_bias_kernel, grid=(band // LANES,), in_specs=[smem],
        out_specs=[band_spec, band_spec], out_shape=[band_shape, band_shape],
    )(rel_bias)
    cmpb = pl.pallas_call(
        _cmp_bias_kernel, grid=(s // LANES,), in_specs=[smem],
        out_specs=pl.BlockSpec((N_HEADS, LANES, LANES), lambda i: (0, 0, i)),
        out_shape=jax.ShapeDtypeStruct((N_HEADS, LANES, s), F32),
    )(rel_bias)
    return band_slc, band_win, cmpb


def _nsa_compress_kernel(kv_ref, pea_ref, peb_ref, w1ka_ref, w1kb_ref, w1va_ref, w1vb_ref,
                         w2k_ref, w2v_ref, gk_ref, kc_ref, vct_ref):
    n_blk = kv_ref.shape[1] // NSA_CMP_STRIDE
    views = [kv_ref[0, pl.ds(l, n_blk, stride=NSA_CMP_STRIDE), :] for l in range(NSA_CMP_STRIDE)]
    first = jnp.concatenate([(views[l] + pea_ref[l:l + 1, :]).astype(BF16)
                             for l in range(NSA_CMP_STRIDE)], axis=1)
    second = jnp.concatenate([(views[l] + peb_ref[l:l + 1, :]).astype(BF16)
                              for l in range(NSA_CMP_STRIDE)], axis=1)

    def hidden(wa_ref, wb_ref):
        pre = _dot(first, wa_ref[...]) + pltpu.roll(_dot(second, wb_ref[...]), n_blk - 1, 0)
        return (pre * _sigmoid(pre)).astype(BF16)

    kc = _dot(hidden(w1ka_ref, w1kb_ref), w2k_ref[...])
    kc = kc * lax.rsqrt(jnp.sum(kc * kc, axis=-1, keepdims=True) / HEAD_DIM + EPS) * gk_ref[...]
    kc_ref[0] = kc.astype(kc_ref.dtype)
    vct_ref[0] = _nt_dot(w2v_ref[...], hidden(w1va_ref, w1vb_ref)).astype(vct_ref.dtype)


def nsa_compress(y3, pea, peb, w1ka, w1kb, w1va, w1vb, w2k, w2v, gk):
    b, s, _ = y3.shape
    assert s // NSA_CMP_STRIDE == LANES
    const = lambda a: pl.BlockSpec(a.shape, lambda bi: (0,) * a.ndim)
    out = pl.BlockSpec((1, LANES, LANES), lambda bi: (bi, 0, 0))
    consts = (pea, peb, w1ka, w1kb, w1va, w1vb, w2k, w2v, gk)
    return pl.pallas_call(
        _nsa_compress_kernel, grid=(b,),
        in_specs=[pl.BlockSpec((1, s, LANES), lambda bi: (bi, 0, COL_KCVC // LANES))]
        + [const(a) for a in consts],
        out_specs=[out, out],
        out_shape=[jax.ShapeDtypeStruct((b, LANES, LANES), BF16)] * 2,
        compiler_params=_cparams("parallel"),
    )(y3, *consts)


def _compress_weights(w1, lane0):
    w = w1.reshape(NSA_CMP_LEN, HEAD_DIM, HEAD_DIM)
    w = jnp.pad(w, ((0, 0), (lane0, LANES - HEAD_DIM - lane0), (0, LANES - HEAD_DIM)))
    w = w.reshape(2, NSA_CMP_STRIDE * LANES, LANES).astype(BF16)
    return w[0], w[1]


def _nsa_kernel(q_ref, k2_ref, v2_ref, misc_ref, kc_ref, vct_ref, bslc_ref, bwin_ref, cmpb_ref,
                gq_ref, gk2_ref, o_ref, ks_ref, kw_ref, vt_ref, win_ref, stage_ref):
    tq = LANES
    n_blk = q_ref.shape[1] // tq
    s = k2_ref.shape[1]
    pad = NSA_WINDOW
    band = NSA_WINDOW + tq
    n_sel = s // NSA_SEL_LEN
    assert NSA_SEL_LANE0 + n_sel <= LANES and pad % NSA_FAR_TILE == 0
    ip = pl.program_id(1)

    @pl.when(ip == 0)
    def _():
        lane_s = lax.broadcasted_iota(jnp.int32, (s, LANES), 1)
        row_s = lax.broadcasted_iota(jnp.int32, (s, LANES), 0)
        kn = _pair_rms(k2_ref[0], gk2_ref[...], lane_s < HEAD_DIM)
        sel_lane = NSA_SEL_LANE0 + (row_s >> 6)
        ks_ref[pad:, :] = jnp.where(lane_s < HEAD_DIM, kn,
                                    jnp.where(lane_s == sel_lane, 1.0, 0.0)).astype(BF16)
        kw_ref[pad:, :] = jnp.where(lane_s < HEAD_DIM, pltpu.roll(kn, HEAD_DIM, 1),
                                    0.0).astype(BF16)
        lane_p = lax.broadcasted_iota(jnp.int32, (pad, LANES), 1)
        before = jnp.where(lane_p == NSA_PAD_LANE, MASKED, 0.0).astype(BF16)
        ks_ref[:pad, :] = before
        kw_ref[:pad, :] = before
        vt_ref[:, pad:] = v2_ref[0].T.astype(BF16)
        vt_ref[:, :pad] = jnp.zeros((LANES, pad), BF16)

    lane = lax.broadcasted_iota(jnp.int32, (tq, LANES), 1)
    lo = lane < HEAD_DIM
    scale = HEAD_DIM ** -0.5
    kc = kc_ref[0]
    vct = vct_ref[0, :HEAD_DIM, :]
    jj = lax.broadcasted_iota(jnp.int32, (LANES, LANES), 0)
    c0 = lax.broadcasted_iota(jnp.int32, (LANES, LANES), 1) * NSA_CMP_STRIDE
    j0 = jj * NSA_SEL_LEN
    overlap = jnp.where((c0 < j0 + NSA_SEL_LEN) & (c0 + NSA_CMP_LEN > j0), 1.0, 0.0).astype(BF16)
    jj32 = lax.broadcasted_iota(jnp.int32, (n_sel, tq), 0)
    in_sel = (lane >= NSA_SEL_LANE0) & (lane < NSA_SEL_LANE0 + n_sel)
    blocks = [dict() for _ in range(n_blk)]

    def prepare(u):
        blk = blocks[u]
        i = ip * n_blk + u
        q = q_ref[0, u * tq:(u + 1) * tq, :]
        q_base = []
        for pair in range(2):
            cols = slice(pair * LANES, (pair + 1) * LANES)
            pn = _pair_rms(q[:, cols], gq_ref[:, cols], lo) * scale
            for head in (jnp.where(lo, pn, 0.0), pltpu.roll(jnp.where(lo, 0.0, pn), HEAD_DIM, 1)):
                q_base.append(jnp.where(lane == NSA_PAD_LANE, 1.0, head))
        q_plain = [x.astype(BF16) for x in q_base]
        b0 = pl.multiple_of(i * tq, tq)
        blk.update(i=i, q_base=q_base, q_plain=q_plain,
                   ks_band=ks_ref[pl.ds(b0, band), :],
                   vs_band=vt_ref[:HEAD_DIM, pl.ds(b0, band)],
                   vw_band=vt_ref[HEAD_DIM:, pl.ds(b0, band)])
        blk['s_cmp'] = [_nt_dot(kc, q_plain[h]) + cmpb_ref[h, :, u * tq:(u + 1) * tq]
                        for h in range(N_HEADS)]
        kw_band = kw_ref[pl.ds(b0, band), :]
        for h in range(N_HEADS):
            win_ref[u, h] = _nt_dot(kw_band, q_plain[h])

    def window(u, h):
        _, l_w, a_w = _first_t(win_ref[u, h] + bwin_ref[h], blocks[u]['vw_band'])
        return a_w / l_w

    def compressed(u):
        blk = blocks[u]
        o_cmp = []
        p_sum = jnp.zeros((LANES, tq), F32)
        for h in range(N_HEADS):
            s_c = blk['s_cmp'][h]
            e_c = jnp.where(s_c > 0.5 * MASKED,
                            jnp.exp(s_c - jnp.max(s_c, axis=0, keepdims=True)), 0.0)
            den = jnp.sum(e_c, axis=0, keepdims=True)
            p_c = e_c / jnp.where(den > 0.0, den, 1.0)
            o_cmp.append(_dot(vct, p_c.astype(BF16)))
            p_sum = p_sum + p_c
        hi, mid, lo3 = _split3(p_sum)
        blk['o_cmp'] = o_cmp
        blk['imp'] = (_dot(overlap, hi) + _dot(overlap, mid) + _dot(overlap, lo3))[0:n_sel]

    def select(u):
        blk = blocks[u]
        i = blk['i']
        cur = (i * tq + lax.broadcasted_iota(jnp.int32, (n_sel, tq), 1)) >> 6
        forced = (jj32 == 0) | (jj32 == cur) | (jj32 == cur - 1)
        imp = jnp.where(forced, FORCE, blk['imp'])
        imp = jnp.where(jj32 <= cur, imp, -FORCE)
        cnt = jnp.zeros((n_sel, tq), F32)
        for jp in range(n_sel):
            other = imp[jp:jp + 1, :]
            beats = (other > imp) | ((other == imp) & (jj32 > jp))
            cnt = cnt + jnp.where(beats, 1.0, 0.0)
        sel_neg = jnp.where(cnt < float(NSA_TOP_N), 0.0, MASKED)
        sel_neg = jnp.concatenate([sel_neg, jnp.zeros((LANES - n_sel, tq), F32)], axis=0).T
        sel_neg = pltpu.roll(sel_neg, NSA_SEL_LANE0, 1)
        band_block0 = NSA_SEL_LANE0 + ((i * tq - pad) >> 6)
        blk['q_band'] = [jnp.where(in_sel, sel_neg, x).astype(BF16) for x in blk['q_base']]
        blk['q_far'] = [jnp.where(in_sel, jnp.where(lane >= band_block0, MASKED, sel_neg),
                                  x).astype(BF16) for x in blk['q_base']]

    def stage_selected(u, h):
        stage_ref[u, h % 2] = _nt_dot(blocks[u]['ks_band'], blocks[u]['q_band'][h])

    def selected_band(u):
        blk = blocks[u]
        slc = []
        for h in range(N_HEADS):
            slc.append(_first_t(stage_ref[u, h % 2] + bslc_ref[h], blk['vs_band']))
            if h + 2 < N_HEADS:
                stage_selected(u, h + 2)
        blk['slc'] = slc

    def far_tile(u, kt):
        blk = blocks[u]
        off = pad + kt * NSA_FAR_TILE
        k_far = ks_ref[pl.ds(off, NSA_FAR_TILE), :]
        v_far = vt_ref[:HEAD_DIM, pl.ds(off, NSA_FAR_TILE)]
        q_far = blk['q_far']
        sc = [_nt_dot(k_far, q_far[0]), _nt_dot(k_far, q_far[1])]
        out = []
        for h in range(N_HEADS):
            if h + 2 < N_HEADS:
                sc.append(_nt_dot(k_far, q_far[h + 2]))
            out.append(_update_t(blk['slc'][h], sc[h], v_far, None))
        blk['slc'] = out

    def finish(u):
        blk = blocks[u]
        rows = slice(u * tq, (u + 1) * tq)
        g_t = _sigmoid(misc_ref[0, rows, :]).T
        heads = []
        for h in range(N_HEADS):
            _, l_s, a_s = blk['slc'][h]
            row = lambda n: g_t[MISC_G + n * N_HEADS + h:MISC_G + n * N_HEADS + h + 1, :]
            heads.append(row(0) * blk['o_cmp'][h] + row(1) * (a_s / l_s) + row(2) * blk['o_win'][h])
        o_ref[0, rows, :] = jnp.concatenate(heads, axis=0).T.astype(o_ref.dtype)

    both = range(n_blk)
    for u in both:
        prepare(u)
    for u in both:
        compressed(u)
    for u in both:
        blocks[u]['o_win'] = [window(u, 0), window(u, 1)]
    for u in both:
        select(u)
    for u in both:
        stage_selected(u, 0)
        stage_selected(u, 1)
    for u in both:
        blocks[u]['o_win'] += [window(u, 2), window(u, 3)]
    for u in both:
        selected_band(u)

    for c in range(s // (n_blk * tq)):
        @pl.when(ip == c)
        def _(c=c):
            saved = [blocks[u]['slc'] for u in both]
            n_far = [(max((c * n_blk + u) * tq - pad, 0) + NSA_FAR_TILE - 1) // NSA_FAR_TILE
                     for u in both]
            for kt in range(max(n_far)):
                for u in both:
                    if kt < n_far[u]:
                        far_tile(u, kt)
            for u in both:
                finish(u)
                blocks[u]['slc'] = saved[u]


def nsa_attention(y3, kc, vct, band_slc, band_win, cmpb, gq, gk2):
    b, s, _ = y3.shape
    n_blk = 2
    tq = n_blk * LANES
    band = NSA_WINDOW + LANES
    return pl.pallas_call(
        _nsa_kernel,
        grid=(b, s // tq),
        in_specs=[pl.BlockSpec((1, tq, MIX_W), lambda bi, i: (bi, i, COL_NSAQ // MIX_W)),
                  pl.BlockSpec((1, s, LANES), lambda bi, i: (bi, 0, COL_K2 // LANES)),
                  pl.BlockSpec((1, s, LANES), lambda bi, i: (bi, 0, COL_V2 // LANES)),
                  pl.BlockSpec((1, tq, LANES), lambda bi, i: (bi, i, COL_MISC // LANES)),
                  pl.BlockSpec((1, LANES, LANES), lambda bi, i: (bi, 0, 0)),
                  pl.BlockSpec((1, LANES, LANES), lambda bi, i: (bi, 0, 0)),
                  pl.BlockSpec(band_slc.shape, lambda bi, i: (0, 0, 0)),
                  pl.BlockSpec(band_win.shape, lambda bi, i: (0, 0, 0)),
                  pl.BlockSpec((N_HEADS, LANES, tq), lambda bi, i: (0, 0, i)),
                  pl.BlockSpec((1, MIX_W), lambda bi, i: (0, 0)),
                  pl.BlockSpec((1, LANES), lambda bi, i: (0, 0))],
        out_specs=pl.BlockSpec((1, tq, MIX_W), lambda bi, i: (bi, i, 0)),
        out_shape=jax.ShapeDtypeStruct((b, s, MIX_W), BF16),
        scratch_shapes=[pltpu.VMEM((s + NSA_WINDOW, LANES), BF16),
                        pltpu.VMEM((s + NSA_WINDOW, LANES), BF16),
                        pltpu.VMEM((LANES, s + NSA_WINDOW), BF16),
                        pltpu.VMEM((n_blk, N_HEADS, band, LANES), F32),
                        pltpu.VMEM((n_blk, 2, band, LANES), F32)],
        compiler_params=_cparams("parallel", "arbitrary"),
    )(y3, y3, y3, y3, kc, vct, band_slc, band_win, cmpb, gq, gk2)


def _merge_kernel(ysb_ref, ymla_ref, ynsa_ref, yfox_ref, h_ref, wgate_ref, wb_ref, wo_ref, x_ref,
                  o_ref):
    d = x_ref.shape[1]
    h = h_ref[...]
    u = None
    for n, y_ref in enumerate((ysb_ref, ymla_ref, ynsa_ref, yfox_ref)):
        gate = _sigmoid(_dot(h, wgate_ref[:, n * d:(n + 1) * d]))
        term = gate * _dot(y_ref[...], wb_ref[n])
        u = term if u is None else u + term
    o_ref[...] = x_ref[...] + _dot(u.astype(BF16), wo_ref[...])


def merge_branches(ys, h, wgate_all, wb_all, wo_all, layer, x, *, tm=512):
    t, d = x.shape
    yspec = pl.BlockSpec((tm, MIX_W), lambda i: (i, 0))
    return pl.pallas_call(
        _merge_kernel, grid=(t // tm,),
        in_specs=[yspec] * 4 + [
            pl.BlockSpec((tm, d), lambda i: (i, 0)),
            pl.BlockSpec((None, d, N_BRANCH * d), lambda i: (layer, 0, 0)),
            pl.BlockSpec((None,) + wb_all.shape[1:], lambda i: (layer, 0, 0, 0)),
            pl.BlockSpec((None, d, d), lambda i: (layer, 0, 0)),
            pl.BlockSpec((tm, d), lambda i: (i, 0))],
        out_specs=pl.BlockSpec((tm, d), lambda i: (i, 0)),
        out_shape=jax.ShapeDtypeStruct((t, d), F32),
        compiler_params=_cparams("parallel"),
    )(*ys, h, wgate_all, wb_all, wo_all, x)


def _mlp_ple_kernel(x_ref, g_ref, wu_ref, wd_ref, gp_ref, wg_ref, p_ref, wp_ref, o_ref,
                    h_ref, acc_ref):
    f = pl.program_id(1)

    @pl.when(f == 0)
    def _():
        h_ref[...] = _rms_rows(x_ref[...], g_ref[...]).astype(BF16)
        acc_ref[...] = jnp.zeros_like(acc_ref)

    a = jnp.maximum(_dot(h_ref[...], wu_ref[...]), 0.0)
    acc_ref[...] += _dot((a * a).astype(BF16), wd_ref[...])

    @pl.when(f == pl.num_programs(1) - 1)
    def _():
        x1 = x_ref[...] + acc_ref[...]
        gate = _sigmoid(_dot(_rms_rows(x1, gp_ref[...]).astype(BF16), wg_ref[...]))
        o_ref[...] = x1 + gate * _dot(p_ref[...].astype(BF16), wp_ref[...])


def mlp_ple(x, g, wu, wd, gp, wg, p_all, wp, layer, *, tm=1024, tf=1024):
    t, d = x.shape
    ff = wu.shape[2]
    return pl.pallas_call(
        _mlp_ple_kernel, grid=(t // tm, ff // tf),
        in_specs=[pl.BlockSpec((tm, d), lambda i, f: (i, 0)),
                  pl.BlockSpec((1, d), lambda i, f: (0, 0)),
                  pl.BlockSpec((None, d, tf), lambda i, f: (layer, 0, f)),
                  pl.BlockSpec((None, tf, d), lambda i, f: (layer, f, 0)),
                  pl.BlockSpec((1, d), lambda i, f: (0, 0)),
                  pl.BlockSpec((None, d, d), lambda i, f: (layer, 0, 0)),
                  pl.BlockSpec((None, tm, PLE_DIM), lambda i, f: (layer, i, 0)),
                  pl.BlockSpec((None, PLE_DIM, d), lambda i, f: (layer, 0, 0))],
        out_specs=pl.BlockSpec((tm, d), lambda i, f: (i, 0)),
        out_shape=jax.ShapeDtypeStruct((t, d), F32),
        scratch_shapes=[pltpu.VMEM((tm, d), BF16), pltpu.VMEM((tm, d), F32)],
        compiler_params=_cparams("parallel", "arbitrary"),
    )(x, g.reshape(1, d), wu, wd, gp.reshape(1, d), wg, p_all, wp)


def _pack_w_in(w):
    offs = np.concatenate([[0], np.cumsum(IN_WIDTHS)]).tolist()
    (sb_q, sb_k, sb_v, cq, ckv, kr, nsa_q, kc, vc, ks, vs, kw, vw, ng,
     fox_q, fox_k, fox_v, ff, gate) = [w[..., offs[n]:offs[n + 1]] for n in range(len(IN_WIDTHS))]
    z = lambda n: jnp.zeros(w.shape[:-1] + (n,), w.dtype)
    misc = jnp.concatenate([ff, ng, z(MISC_KR - MISC_G - 3 * N_HEADS), kr,
                            z(LANES - MISC_KR - MLA_ROPE)], axis=-1)
    w_f32_part = jnp.concatenate([cq, ckv, ks, kw, vs, vw, kc, vc, misc, nsa_q], axis=-1)
    w_bf16_part = jnp.concatenate([sb_q, sb_k, sb_v, fox_q, fox_k, fox_v], axis=-1)
    return w_f32_part.astype(BF16), w_bf16_part.astype(BF16), gate.astype(BF16)


def _head_slots(w, width):
    k = w.shape[0]
    w = w.reshape(k, N_HEADS, width)
    return jnp.pad(w, ((0, 0), (0, 0), (0, LANES - width))).reshape(k, N_HEADS * LANES)


def _rope_tables(s):
    half = MLA_ROPE // 2
    inv = jnp.exp(-math.log(ROPE_THETA) * jnp.arange(half, dtype=F32) / half)
    ang = jnp.arange(s, dtype=F32)[:, None] * inv[None, :]
    cos, sin = jnp.cos(ang), jnp.sin(ang)
    ones = jnp.ones((s, MLA_NOPE), F32)
    zeros = lambda n: jnp.zeros((s, n), F32)
    tail = LANES - MLA_QK
    cos_t = jnp.concatenate([ones, cos, cos, jnp.ones((s, tail), F32)], axis=1)
    sa_t = jnp.concatenate([zeros(MLA_NOPE), -sin, zeros(half), zeros(tail)], axis=1)
    sb_t = jnp.concatenate([zeros(MLA_NOPE), zeros(half), sin, zeros(tail)], axis=1)
    return cos_t, sa_t, sb_t


def _pad_lanes(v, left=0):
    v = v.reshape(1, -1)
    return jnp.pad(v, ((0, 0), (left, LANES - left - v.shape[1])))


def _pad_to_lanes(w):
    return jnp.pad(w, ((0, 0), (0, LANES - w.shape[1])))


def kernel(x, p, rel_bias, norm_mix_g, w_in, mla_cq_norm_g, mla_ckv_norm_g, mla_w_uq, mla_w_ukv,
           mla_qn_g, mla_kn_g, nsa_pe_k, nsa_pe_v, nsa_w1_k, nsa_w2_k, nsa_w1_v, nsa_w2_v,
           nsa_qn_g, nsa_kn_g, fox_f_bias, fox_qn_g, fox_kn_g, w_branch, w_o, norm_mlp_g,
           w_mlp_up, w_mlp_down, norm_ple_g, w_ple_gate, w_ple_proj):
    b, s, d = x.shape
    t = b * s
    xf = x.reshape(t, d)
    p_all = p.reshape(DEPTH, t, PLE_DIM)
    band_slc, band_win, cmpb = rel_bias_tables(rel_bias.astype(F32), s)
    cos_t, sa_t, sb_t = _rope_tables(s)
    n_cmp_in = NSA_CMP_STRIDE * HEAD_DIM
    w_a, w_b, w_gate = _pack_w_in(w_in)
    wb_all, wo_all = w_branch.astype(BF16), w_o.astype(BF16)
    wu_all, wd_all = w_mlp_up.astype(BF16), w_mlp_down.astype(BF16)
    wg_all, wp_all = w_ple_gate.astype(BF16), w_ple_proj.astype(BF16)

    for i in range(DEPTH):
        ya, h = norm_proj(xf, norm_mix_g[i], w_a, i)
        yb = matmul_bf16(h, w_b, i)
        y3 = ya.reshape(b, s, N_F32)
        yb3 = yb.reshape(b, s, N_BF16)

        y_sb = sb_attention(yb3)

        wukv = mla_w_ukv[i].reshape(MLA_KV_LORA, N_HEADS, MLA_NOPE + MLA_V)
        q_m, k_m, vt_m = mla_prep(
            y3, mla_cq_norm_g[i].reshape(1, -1), mla_ckv_norm_g[i].reshape(1, -1),
            _head_slots(mla_w_uq[i], MLA_QK).astype(BF16),
            _head_slots(wukv[:, :, :MLA_NOPE].reshape(MLA_KV_LORA, -1), MLA_NOPE).astype(BF16),
            wukv[:, :, MLA_NOPE:].reshape(MLA_KV_LORA, -1).T.astype(BF16),
            _pad_lanes(mla_qn_g[i]), _pad_lanes(mla_kn_g[i]), cos_t, sa_t, sb_t)
        y_mla = mla_attention(q_m, k_m, vt_m)

        pad_sq = lambda w: jnp.pad(_pad_to_lanes(w), ((0, LANES - w.shape[0]), (0, 0)))
        pe_kv = jnp.concatenate([nsa_pe_k[i], nsa_pe_v[i]], axis=1)
        kc, vct = nsa_compress(
            y3, pe_kv[:NSA_CMP_STRIDE], pe_kv[NSA_CMP_STRIDE:],
            *_compress_weights(nsa_w1_k[i], 0), *_compress_weights(nsa_w1_v[i], HEAD_DIM),
            pad_sq(nsa_w2_k[i]).astype(BF16), pad_sq(nsa_w2_v[i].T).astype(BF16),
            _pad_lanes(nsa_kn_g[i, 0]))
        y_nsa = nsa_attention(
            y3, kc, vct, band_slc, band_win, cmpb,
            jnp.tile(nsa_qn_g[i], N_HEADS).reshape(1, -1),
            jnp.concatenate([nsa_kn_g[i, 1], nsa_kn_g[i, 2]]).reshape(1, -1))

        cum = fox_cum(y3, _pad_lanes(fox_f_bias[i], MISC_F))
        y_fox = fox_attention(yb3, cum, jnp.tile(fox_qn_g[i], N_HEADS).reshape(1, -1),
                              jnp.tile(fox_kn_g[i], N_HEADS).reshape(1, -1))

        ys = [a.reshape(t, MIX_W) for a in (y_sb, y_mla, y_nsa, y_fox)]
        xf = merge_branches(ys, h, w_gate, wb_all, wo_all, i, xf)
        xf = mlp_ple(xf, norm_mlp_g[i], wu_all, wd_all, norm_ple_g[i], wg_all, p_all, wp_all, i)
    return xf.reshape(b, s, d)
```

```python
import functools
import math

import numpy as np
import jax
import jax.numpy as jnp
from jax import lax
from jax.experimental import pallas as pl
from jax.experimental.pallas import tpu as pltpu

F32 = jnp.float32
BF16 = jnp.bfloat16

D_MODEL = 1024
DEPTH = 4
HEAD_DIM = 64
N_HEADS = 4
MIX_W = N_HEADS * HEAD_DIM
N_BRANCH = 4
EPS = 1e-6
FORCE = 1e9
MLA_Q_LORA = 384
MLA_KV_LORA = 128
MLA_NOPE = 64
MLA_ROPE = 32
MLA_V = 64
MLA_QK = MLA_NOPE + MLA_ROPE
ROPE_THETA = 10000.0
NSA_CMP_LEN = 32
NSA_CMP_STRIDE = 16
NSA_SEL_LEN = 64
NSA_TOP_N = 16
NSA_WINDOW = 512
REL_BUCKETS = 32
REL_MAX_DIST = 128
D_FF = 4 * D_MODEL
PLE_DIM = 256

LANES = 128
MASKED = -1e30
VMEM_LIMIT = 56 * 1024 * 1024

IN_WIDTHS = ((MIX_W,) * 3
             + (MLA_Q_LORA, MLA_KV_LORA, MLA_ROPE)
             + (MIX_W,) + (HEAD_DIM,) * 6 + (3 * N_HEADS,)
             + (MIX_W,) * 3 + (N_HEADS,)
             + (N_BRANCH * D_MODEL,))

COL_CQ = 0
COL_CKV = 384
COL_K2 = 512
COL_V2 = 640
COL_KCVC = 768
COL_MISC = 896
COL_NSAQ = 1024
N_F32 = 1280
COL_SB = 0
COL_FOX = 768
N_BF16 = 1536
MISC_F = 0
MISC_G = 4
MISC_KR = 32

NSA_PAD_LANE = HEAD_DIM
NSA_SEL_LANE0 = HEAD_DIM + 1
NSA_FAR_TILE = 512


def _cparams(*sem):
    return pltpu.CompilerParams(dimension_semantics=sem, vmem_limit_bytes=VMEM_LIMIT)


def _nt_dot(a, b):
    return lax.dot_general(a, b, (((1,), (1,)), ((), ())), preferred_element_type=F32)


def _dot(a, b):
    return jnp.dot(a, b, preferred_element_type=F32)


def _split3(x):
    hi = x.astype(BF16)
    r1 = x - hi.astype(F32)
    mid = r1.astype(BF16)
    lo = (r1 - mid.astype(F32)).astype(BF16)
    return hi, mid, lo


def _softplus(z):
    return jnp.maximum(z, 0.0) + jnp.log(1.0 + jnp.exp(-jnp.abs(z)))


def _sigmoid(z):
    return 1.0 / (1.0 + jnp.exp(-z))


def _rms_rows(x, g):
    r = lax.rsqrt(jnp.mean(x * x, axis=-1, keepdims=True) + EPS)
    return x * r * g


def _pair_rms(x, g, lo):
    x2 = x * x
    s0 = jnp.sum(jnp.where(lo, x2, 0.0), axis=-1, keepdims=True)
    s1 = jnp.sum(jnp.where(lo, 0.0, x2), axis=-1, keepdims=True)
    r = jnp.where(lo, lax.rsqrt(s0 / HEAD_DIM + EPS), lax.rsqrt(s1 / HEAD_DIM + EPS))
    return x * r * g


LOG2E = math.log2(math.e)


def _first_t(s, vt):
    m = jnp.max(s, axis=0, keepdims=True)
    p = jnp.exp2(s - m)
    return m, jnp.sum(p, axis=0, keepdims=True), _dot(vt, p.astype(BF16))


def _update_t(carry, s, vt, mask):
    m, l, acc = carry
    if mask is not None:
        s = jnp.where(mask, s, MASKED)
    m_new = jnp.maximum(m, jnp.max(s, axis=0, keepdims=True))
    p = jnp.exp2(s - m_new)
    if mask is not None:
        p = jnp.where(mask, p, 0.0)
    alpha = jnp.exp2(m - m_new)
    l = alpha * l + jnp.sum(p, axis=0, keepdims=True)
    acc = alpha * acc + _dot(vt, p.astype(BF16))
    return m_new, l, acc


def _init_t(tq):
    return (jnp.full((1, tq), MASKED, F32), jnp.zeros((1, tq), F32),
            jnp.zeros((HEAD_DIM, tq), F32))


def _causal_flash_static(c, tq, tk, score_fn, vt_fn):
    assert tk % tq == 0
    heads = range(N_HEADS)
    diff = (lax.broadcasted_iota(jnp.int32, (tk, tq), 0)
            - lax.broadcasted_iota(jnp.int32, (tk, tq), 1))
    n_full = (c * tq) // tk
    cur = [score_fn(h, 0) for h in heads]
    state = [_init_t(tq) for _ in heads]
    for kt in range(n_full + 1):
        nxt = [score_fn(h, (kt + 1) * tk) for h in heads] if kt < n_full else None
        mask = (diff <= c * tq - kt * tk) if kt == n_full else None
        state = [_update_t(state[h], cur[h], vt_fn(h, kt * tk), mask) for h in heads]
        cur = nxt
    return jnp.concatenate([acc / l for (_, l, acc) in state], axis=0)


def _augment(x, terms, col, lane, h, key_side):
    live = (lane < HEAD_DIM) if h == 0 else (lane >= HEAD_DIM)
    a0 = HEAD_DIM if h == 0 else 0
    c0, o0 = (a0, a0 + 3) if key_side else (a0 + 3, a0)
    src = lax.broadcasted_iota(jnp.int32, (3 * LANES, LANES), 0)
    dst = lax.broadcasted_iota(jnp.int32, (3 * LANES, LANES), 1)
    place = jnp.where(((src & (LANES - 1)) == col) & (dst == c0 + (src >> 7)),
                      -1.0 if key_side else 1.0, 0.0).astype(BF16)
    ones = jnp.where((lane >= o0) & (lane < o0 + 3), 1.0, 0.0)
    return jnp.where(live, x, _dot(terms, place) + ones)


def _norm_proj_kernel(x_ref, g_ref, w_ref, o_ref, h_ref):
    h = _rms_rows(x_ref[...], g_ref[...]).astype(BF16)
    h_ref[...] = h
    o_ref[...] = _dot(h, w_ref[...])


def norm_proj(x, g, w_all, layer, *, tm=1024):
    t, d = x.shape
    n = w_all.shape[2]
    return pl.pallas_call(
        _norm_proj_kernel,
        grid=(t // tm,),
        in_specs=[pl.BlockSpec((tm, d), lambda i: (i, 0)),
                  pl.BlockSpec((1, d), lambda i: (0, 0)),
                  pl.BlockSpec((None, d, n), lambda i: (layer, 0, 0))],
        out_specs=[pl.BlockSpec((tm, n), lambda i: (i, 0)),
                   pl.BlockSpec((tm, d), lambda i: (i, 0))],
        out_shape=[jax.ShapeDtypeStruct((t, n), F32), jax.ShapeDtypeStruct((t, d), BF16)],
        compiler_params=_cparams("parallel"),
    )(x, g.reshape(1, d), w_all)


def _matmul_kernel(h_ref, w_ref, o_ref):
    o_ref[...] = _dot(h_ref[...], w_ref[...]).astype(o_ref.dtype)


def matmul_bf16(h, w_all, layer, *, tm=2048, tn=512):
    t, d = h.shape
    n = w_all.shape[2]
    return pl.pallas_call(
        _matmul_kernel,
        grid=(t // tm, n // tn),
        in_specs=[pl.BlockSpec((tm, d), lambda i, j: (i, 0)),
                  pl.BlockSpec((None, d, tn), lambda i, j: (layer, 0, j))],
        out_specs=pl.BlockSpec((tm, tn), lambda i, j: (i, j)),
        out_shape=jax.ShapeDtypeStruct((t, n), BF16),
        compiler_params=_cparams("parallel", "parallel"),
    )(h, w_all)


def _sb_kernel(q_ref, k_ref, v_ref, o_ref, vt_ref, *, tq, tk):
    i = pl.program_id(1)

    @pl.when(i == 0)
    def _():
        vt_ref[...] = v_ref[0].astype(F32).T.astype(BF16)

    sub = LANES
    n_sub = tk // sub
    heads = range(N_HEADS)
    lane = lax.broadcasted_iota(jnp.int32, (tq, LANES), 1)
    lo = lane < HEAD_DIM
    qh = []
    for p in range(N_HEADS // 2):
        q = q_ref[0, :, p * LANES:(p + 1) * LANES].astype(F32) * (HEAD_DIM ** -0.5)
        qh += [jnp.where(lo, q, 0.0).astype(BF16), jnp.where(lo, 0.0, q).astype(BF16)]
    rr = lax.broadcasted_iota(jnp.int32, (sub, sub), 0)
    cc = lax.broadcasted_iota(jnp.int32, (sub, sub), 1)
    upper = jnp.where(cc >= rr, 1.0, 0.0).astype(BF16)
    upper2 = jnp.concatenate([upper, upper], axis=1)
    diff = (lax.broadcasted_iota(jnp.int32, (tk, tq), 0)
            - lax.broadcasted_iota(jnp.int32, (tk, tq), 1))

    def logits_at(kt):
        return tuple(_nt_dot(k_ref[0, pl.ds(kt * tk, tk), (h // 2) * LANES:(h // 2 + 1) * LANES],
                             qh[h]) for h in heads)

    def tile(c, kt, zs, carry, masked):
        off = kt * tk
        past = (diff < c * tq - off) if masked else None
        out = []
        for h in heads:
            run, acc = carry[h]
            z = zs[h]
            sp = _softplus(z)
            spm = jnp.where(past, sp, 0.0) if masked else sp
            ws = [None] * n_sub
            for j in reversed(range(n_sub)):
                sl = slice(j * sub, (j + 1) * sub)
                spj = spm[sl]
                hi = spj.astype(BF16)
                mid = (spj - hi.astype(F32)).astype(BF16)
                tail = _dot(upper2, jnp.concatenate([hi, mid], axis=0))
                w = jnp.exp(z[sl] - tail - run)
                if masked:
                    w = jnp.where(past[sl], w, 0.0)
                ws[j] = w.astype(BF16)
                run = run + tail[0:1, :]
            vt = vt_ref[h * HEAD_DIM:(h + 1) * HEAD_DIM, pl.ds(off, tk)]
            out.append((run, acc + _dot(vt, jnp.concatenate(ws, axis=0))))
        return tuple(out)

    zero = (jnp.zeros((1, tq), F32), jnp.zeros((HEAD_DIM, tq), F32))
    for c in range(k_ref.shape[1] // tq):
        @pl.when(i == c)
        def _(c=c):
            n_full = (c * tq) // tk
            zs = logits_at(n_full)
            carry = (zero,) * N_HEADS
            for kt in range(n_full, -1, -1):
                zs_next = logits_at(kt - 1) if kt > 0 else None
                carry = tile(c, kt, zs, carry, kt == n_full)
                zs = zs_next
            o_t = jnp.concatenate([acc for _, acc in carry], axis=0)
            o_ref[0] = o_t.T.astype(o_ref.dtype)


def sb_attention(y3, *, tq=512, tk=512):
    b, s, _ = y3.shape
    qb = COL_SB // MIX_W
    return pl.pallas_call(
        functools.partial(_sb_kernel, tq=tq, tk=tk),
        grid=(b, s // tq),
        in_specs=[pl.BlockSpec((1, tq, MIX_W), lambda bi, i: (bi, i, qb)),
                  pl.BlockSpec((1, s, MIX_W), lambda bi, i: (bi, 0, qb + 1)),
                  pl.BlockSpec((1, s, MIX_W), lambda bi, i: (bi, 0, qb + 2))],
        out_specs=pl.BlockSpec((1, tq, MIX_W), lambda bi, i: (bi, i, 0)),
        out_shape=jax.ShapeDtypeStruct((b, s, MIX_W), BF16),
        scratch_shapes=[pltpu.VMEM((MIX_W, s), BF16)],
        compiler_params=_cparams("parallel", "arbitrary"),
    )(y3, y3, y3)


def _fox_cum_kernel(misc_ref, fb_ref, cum_ref, *, s):
    rr = lax.broadcasted_iota(jnp.int32, (LANES, LANES), 0)
    cc = lax.broadcasted_iota(jnp.int32, (LANES, LANES), 1)
    lower = jnp.where(cc <= rr, 1.0, 0.0).astype(BF16)

    def body(n, carry):
        off = pl.multiple_of(n * LANES, LANES)
        x = misc_ref[0, pl.ds(off, LANES), :] + fb_ref[...]
        log_f = jnp.minimum(x, 0.0) - jnp.log(1.0 + jnp.exp(-jnp.abs(x)))
        hi, mid, lo3 = _split3(log_f)
        c = _dot(lower, hi) + _dot(lower, mid) + _dot(lower, lo3) + carry
        cum_ref[0, pl.ds(off, LANES), :] = c
        return c[LANES - 1:LANES, :]

    lax.fori_loop(0, s // LANES, body, jnp.zeros((1, LANES), F32))


def fox_cum(y3, fbias_row):
    b, s, _ = y3.shape
    return pl.pallas_call(
        functools.partial(_fox_cum_kernel, s=s),
        grid=(b,),
        in_specs=[pl.BlockSpec((1, s, LANES), lambda bi: (bi, 0, COL_MISC // LANES)),
                  pl.BlockSpec((1, LANES), lambda bi: (0, 0))],
        out_specs=pl.BlockSpec((1, s, LANES), lambda bi: (bi, 0, 0)),
        out_shape=jax.ShapeDtypeStruct((b, s, LANES), F32),
        compiler_params=_cparams("parallel"),
    )(y3, fbias_row)


def _fox_kernel(q_ref, k_ref, v_ref, cum_ref, gq_ref, gk_ref, o_ref, ka_ref, vt_ref, *, tq, tk):
    i = pl.program_id(1)
    s = k_ref.shape[1]
    pairs = range(N_HEADS // 2)

    @pl.when(i == 0)
    def _():
        lane_s = lax.broadcasted_iota(jnp.int32, (s, LANES), 1)
        terms = jnp.concatenate(_split3(cum_ref[0] * LOG2E), axis=1)
        for p in pairs:
            cols = slice(p * LANES, (p + 1) * LANES)
            kn = _pair_rms(k_ref[0, :, cols].astype(F32), gk_ref[:, cols], lane_s < HEAD_DIM)
            for h in range(2):
                ka_ref[:, (2 * p + h) * LANES:(2 * p + h + 1) * LANES] = _augment(
                    kn, terms, MISC_F + 2 * p + h, lane_s, h, True).astype(BF16)
        vt_ref[...] = v_ref[0].astype(F32).T.astype(BF16)

    lane = lax.broadcasted_iota(jnp.int32, (tq, LANES), 1)
    terms_q = jnp.concatenate(
        _split3(cum_ref[0, pl.ds(pl.multiple_of(i * tq, tq), tq), :] * LOG2E), axis=1)
    qa = []
    for p in pairs:
        cols = slice(p * LANES, (p + 1) * LANES)
        qn = _pair_rms(q_ref[0, :, cols].astype(F32), gq_ref[:, cols], lane < HEAD_DIM)
        qn = qn * (HEAD_DIM ** -0.5 * LOG2E)
        qa += [_augment(qn, terms_q, MISC_F + 2 * p + h, lane, h, False).astype(BF16)
               for h in range(2)]

    def scores(h, off):
        return _nt_dot(ka_ref[pl.ds(off, tk), h * LANES:(h + 1) * LANES], qa[h])

    def values_t(h, off):
        return vt_ref[h * HEAD_DIM:(h + 1) * HEAD_DIM, pl.ds(off, tk)]

    for c in range(s // tq):
        @pl.when(i == c)
        def _(c=c):
            o_ref[0] = _causal_flash_static(c, tq, tk, scores, values_t).T.astype(o_ref.dtype)


def fox_attention(y3, cum, gq, gk, *, tq=512, tk=512):
    b, s, _ = y3.shape
    qb = COL_FOX // MIX_W
    return pl.pallas_call(
        functools.partial(_fox_kernel, tq=tq, tk=tk),
        grid=(b, s // tq),
        in_specs=[pl.BlockSpec((1, tq, MIX_W), lambda bi, i: (bi, i, qb)),
                  pl.BlockSpec((1, s, MIX_W), lambda bi, i: (bi, 0, qb + 1)),
                  pl.BlockSpec((1, s, MIX_W), lambda bi, i: (bi, 0, qb + 2)),
                  pl.BlockSpec((1, s, LANES), lambda bi, i: (bi, 0, 0)),
                  pl.BlockSpec((1, MIX_W), lambda bi, i: (0, 0)),
                  pl.BlockSpec((1, MIX_W), lambda bi, i: (0, 0))],
        out_specs=pl.BlockSpec((1, tq, MIX_W), lambda bi, i: (bi, i, 0)),
        out_shape=jax.ShapeDtypeStruct((b, s, MIX_W), BF16),
        scratch_shapes=[pltpu.VMEM((s, N_HEADS * LANES), BF16), pltpu.VMEM((MIX_W, s), BF16)],
        compiler_params=_cparams("parallel", "arbitrary"),
    )(y3, y3, y3, cum, gq, gk)


def _mla_prep_kernel(cq_ref, ckv_ref, misc_ref, gcq_ref, gckv_ref, wuq_ref, wuk_ref, wuv_ref,
                     qg_ref, kg_ref, cos_ref, sa_ref, sb_ref, q_out, k_out, v_out, *, ts):
    hq = _rms_rows(cq_ref[0], gcq_ref[...]).astype(BF16)
    hkv = _rms_rows(ckv_ref[0], gckv_ref[...]).astype(BF16)
    q = _dot(hq, wuq_ref[...])
    kn = _dot(hkv, wuk_ref[...])
    v_out[0] = _nt_dot(wuv_ref[...], hkv).astype(v_out.dtype)

    lane = lax.broadcasted_iota(jnp.int32, (ts, LANES), 1)
    misc = misc_ref[0]
    kr = pltpu.roll(jnp.where((lane >= MISC_KR) & (lane < MISC_KR + MLA_ROPE), misc, 0.0),
                    MLA_NOPE - MISC_KR, 1)
    cos, sin = cos_ref[...], sa_ref[...] + sb_ref[...]
    half = MLA_ROPE // 2

    kk = lax.broadcasted_iota(jnp.int32, (2 * LANES, LANES), 0) & (LANES - 1)
    ll = lax.broadcasted_iota(jnp.int32, (2 * LANES, LANES), 1)
    ones2 = jnp.ones((2 * LANES, LANES), BF16)
    first = (ll >= MLA_NOPE) & (ll < MLA_NOPE + half)
    second = (ll >= MLA_NOPE + half) & (ll < MLA_QK)
    swap2 = jnp.where((first & (kk == ll + half)) | (second & (kk == ll - half)),
                      1.0, 0.0).astype(BF16)

    def split2(x):
        hi = x.astype(BF16)
        return jnp.concatenate([hi, (x - hi.astype(F32)).astype(BF16)], axis=1)

    def norm_rope(t, g):
        t = t * lax.rsqrt(_dot(split2(t * t), ones2) / MLA_QK + EPS) * g
        return t * cos + _dot(split2(t), swap2) * sin

    for h in range(N_HEADS):
        sl = slice(h * LANES, (h + 1) * LANES)
        q_out[0, :, sl] = (norm_rope(q[:, sl], qg_ref[...])
                           * (MLA_QK ** -0.5 * LOG2E)).astype(q_out.dtype)
        k_out[0, :, sl] = norm_rope(kn[:, sl] + kr, kg_ref[...]).astype(k_out.dtype)


def mla_prep(y3, gcq, gckv, wuq, wuk, wuv, qg, kg, cos, sa, sb, *, ts=512):
    b, s, _ = y3.shape
    const = lambda shape: pl.BlockSpec(shape, lambda bi, i: (0,) * len(shape))
    return pl.pallas_call(
        functools.partial(_mla_prep_kernel, ts=ts),
        grid=(b, s // ts),
        in_specs=[pl.BlockSpec((1, ts, MLA_Q_LORA), lambda bi, i: (bi, i, COL_CQ // MLA_Q_LORA)),
                  pl.BlockSpec((1, ts, LANES), lambda bi, i: (bi, i, COL_CKV // LANES)),
                  pl.BlockSpec((1, ts, LANES), lambda bi, i: (bi, i, COL_MISC // LANES)),
                  const((1, MLA_Q_LORA)), const((1, MLA_KV_LORA)),
                  const((MLA_Q_LORA, N_HEADS * LANES)), const((MLA_KV_LORA, N_HEADS * LANES)),
                  const((MIX_W, MLA_KV_LORA)), const((1, LANES)), const((1, LANES)),
                  pl.BlockSpec((ts, LANES), lambda bi, i: (i, 0)),
                  pl.BlockSpec((ts, LANES), lambda bi, i: (i, 0)),
                  pl.BlockSpec((ts, LANES), lambda bi, i: (i, 0))],
        out_specs=[pl.BlockSpec((1, ts, N_HEADS * LANES), lambda bi, i: (bi, i, 0)),
                   pl.BlockSpec((1, ts, N_HEADS * LANES), lambda bi, i: (bi, i, 0)),
                   pl.BlockSpec((1, MIX_W, ts), lambda bi, i: (bi, 0, i))],
        out_shape=[jax.ShapeDtypeStruct((b, s, N_HEADS * LANES), BF16),
                   jax.ShapeDtypeStruct((b, s, N_HEADS * LANES), BF16),
                   jax.ShapeDtypeStruct((b, MIX_W, s), BF16)],
        compiler_params=_cparams("parallel", "parallel"),
    )(y3, y3, y3, gcq, gckv, wuq, wuk, wuv, qg, kg, cos, sa, sb)


def _mla_kernel(q_ref, k_ref, vt_ref, o_ref, *, tq, tk):
    i = pl.program_id(1)
    qh = [q_ref[0, :, h * LANES:(h + 1) * LANES] for h in range(N_HEADS)]

    def scores(h, off):
        return _nt_dot(k_ref[0, pl.ds(off, tk), h * LANES:(h + 1) * LANES], qh[h])

    def values_t(h, off):
        return vt_ref[0, h * HEAD_DIM:(h + 1) * HEAD_DIM, pl.ds(off, tk)]

    for c in range(k_ref.shape[1] // tq):
        @pl.when(i == c)
        def _(c=c):
            o_ref[0] = _causal_flash_static(c, tq, tk, scores, values_t).T.astype(o_ref.dtype)


def mla_attention(q, k, vt, *, tq=512, tk=512):
    b, s, _ = q.shape
    return pl.pallas_call(
        functools.partial(_mla_kernel, tq=tq, tk=tk),
        grid=(b, s // tq),
        in_specs=[pl.BlockSpec((1, tq, N_HEADS * LANES), lambda bi, i: (bi, i, 0)),
                  pl.BlockSpec((1, s, N_HEADS * LANES), lambda bi, i: (bi, 0, 0)),
                  pl.BlockSpec((1, MIX_W, s), lambda bi, i: (bi, 0, 0))],
        out_specs=pl.BlockSpec((1, tq, MIX_W), lambda bi, i: (bi, i, 0)),
        out_shape=jax.ShapeDtypeStruct((b, s, MIX_W), BF16),
        compiler_params=_cparams("parallel", "arbitrary"),
    )(q, k, vt)


def _rel_bias_tile(dist, tab_ref, h):
    max_exact = REL_BUCKETS // 2
    d = jnp.maximum(dist, 0)
    large = max_exact + (jnp.log(jnp.maximum(d, 1).astype(F32) / max_exact)
                         / math.log(REL_MAX_DIST / max_exact)
                         * (REL_BUCKETS - max_exact)).astype(jnp.int32)
    large = jnp.minimum(large, REL_BUCKETS - 1)
    bucket = jnp.where(d < max_exact, d, large)
    out = jnp.zeros(dist.shape, F32)
    for bkt in range(REL_BUCKETS):
        out = jnp.where(bucket == bkt, tab_ref[bkt, h], out)
    return out


def _band_bias_kernel(tab_ref, slc_ref, win_ref):
    r = pl.program_id(0)
    kj = r * LANES + lax.broadcasted_iota(jnp.int32, (LANES, LANES), 0)
    qi = lax.broadcasted_iota(jnp.int32, (LANES, LANES), 1)
    dist = qi + NSA_WINDOW - kj
    for h in range(N_HEADS):
        delta = (_rel_bias_tile(dist, tab_ref, h) - tab_ref[REL_BUCKETS - 1, h]) * LOG2E
        slc_ref[h] = jnp.where(dist >= 0, delta, MASKED)
        win_ref[h] = jnp.where((dist >= 0) & (dist < NSA_WINDOW), delta, MASKED)


def _cmp_bias_kernel(tab_ref, o_ref):
    i = pl.program_id(0)
    c = lax.broadcasted_iota(jnp.int32, (LANES, LANES), 0)
    s = i * LANES + lax.broadcasted_iota(jnp.int32, (LANES, LANES), 1)
    dist = s - (c * NSA_CMP_STRIDE + NSA_CMP_LEN - 1)
    for h in range(N_HEADS):
        o_ref[h] = jnp.where(dist >= 0, _rel_bias_tile(dist, tab_ref, h) * LOG2E, MASKED)


def rel_bias_tables(rel_bias, s):
    assert REL_BUCKETS == 32 and REL_MAX_DIST == 128 and NSA_WINDOW >= 113
    band = NSA_WINDOW + LANES
    smem = pl.BlockSpec(memory_space=pltpu.SMEM)
    band_spec = pl.BlockSpec((N_HEADS, LANES, LANES), lambda r: (0, r, 0))
    band_shape = jax.ShapeDtypeStruct((N_HEADS, band, LANES), F32)
    band_slc, band_win = pl.pallas_call(
        _band_bias_kernel, grid=(band // LANES,), in_specs=[smem],
        out_specs=[band_spec, band_spec], out_shape=[band_shape, band_shape],
    )(rel_bias)
    cmpb = pl.pallas_call(
        _cmp_bias_kernel, grid=(s // LANES,), in_specs=[smem],
        out_specs=pl.BlockSpec((N_HEADS, LANES, LANES), lambda i: (0, 0, i)),
        out_shape=jax.ShapeDtypeStruct((N_HEADS, LANES, s), F32),
    )(rel_bias)
    return band_slc, band_win, cmpb


def _nsa_compress_kernel(kv_ref, pea_ref, peb_ref, w1ka_ref, w1kb_ref, w1va_ref, w1vb_ref,
                         w2k_ref, w2v_ref, gk_ref, kc_ref, vct_ref):
    n_blk = kv_ref.shape[1] // NSA_CMP_STRIDE
    views = [kv_ref[0, pl.ds(l, n_blk, stride=NSA_CMP_STRIDE), :] for l in range(NSA_CMP_STRIDE)]
    first = jnp.concatenate([(views[l] + pea_ref[l:l + 1, :]).astype(BF16)
                             for l in range(NSA_CMP_STRIDE)], axis=1)
    second = jnp.concatenate([(views[l] + peb_ref[l:l + 1, :]).astype(BF16)
                              for l in range(NSA_CMP_STRIDE)], axis=1)

    def hidden(wa_ref, wb_ref):
        pre = _dot(first, wa_ref[...]) + pltpu.roll(_dot(second, wb_ref[...]), n_blk - 1, 0)
        return (pre * _sigmoid(pre)).astype(BF16)

    kc = _dot(hidden(w1ka_ref, w1kb_ref), w2k_ref[...])
    kc = kc * lax.rsqrt(jnp.sum(kc * kc, axis=-1, keepdims=True) / HEAD_DIM + EPS) * gk_ref[...]
    kc_ref[0] = kc.astype(kc_ref.dtype)
    vct_ref[0] = _nt_dot(w2v_ref[...], hidden(w1va_ref, w1vb_ref)).astype(vct_ref.dtype)


def nsa_compress(y3, pea, peb, w1ka, w1kb, w1va, w1vb, w2k, w2v, gk):
    b, s, _ = y3.shape
    assert s // NSA_CMP_STRIDE == LANES
    const = lambda a: pl.BlockSpec(a.shape, lambda bi: (0,) * a.ndim)
    out = pl.BlockSpec((1, LANES, LANES), lambda bi: (bi, 0, 0))
    consts = (pea, peb, w1ka, w1kb, w1va, w1vb, w2k, w2v, gk)
    return pl.pallas_call(
        _nsa_compress_kernel, grid=(b,),
        in_specs=[pl.BlockSpec((1, s, LANES), lambda bi: (bi, 0, COL_KCVC // LANES))]
        + [const(a) for a in consts],
        out_specs=[out, out],
        out_shape=[jax.ShapeDtypeStruct((b, LANES, LANES), BF16)] * 2,
        compiler_params=_cparams("parallel"),
    )(y3, *consts)


def _compress_weights(w1, lane0):
    w = w1.reshape(NSA_CMP_LEN, HEAD_DIM, HEAD_DIM)
    w = jnp.pad(w, ((0, 0), (lane0, LANES - HEAD_DIM - lane0), (0, LANES - HEAD_DIM)))
    w = w.reshape(2, NSA_CMP_STRIDE * LANES, LANES).astype(BF16)
    return w[0], w[1]


def _nsa_kernel(q_ref, k2_ref, v2_ref, misc_ref, kc_ref, vct_ref, bslc_ref, bwin_ref, cmpb_ref,
                gq_ref, gk2_ref, o_ref, ks_ref, kw_ref, vt_ref, win_ref, stage_ref):
    tq = LANES
    n_blk = q_ref.shape[1] // tq
    s = k2_ref.shape[1]
    pad = NSA_WINDOW
    band = NSA_WINDOW + tq
    n_sel = s // NSA_SEL_LEN
    assert NSA_SEL_LANE0 + n_sel <= LANES and pad % NSA_FAR_TILE == 0
    ip = pl.program_id(1)

    @pl.when(ip == 0)
    def _():
        lane_s = lax.broadcasted_iota(jnp.int32, (s, LANES), 1)
        row_s = lax.broadcasted_iota(jnp.int32, (s, LANES), 0)
        kn = _pair_rms(k2_ref[0], gk2_ref[...], lane_s < HEAD_DIM)
        sel_lane = NSA_SEL_LANE0 + (row_s >> 6)
        ks_ref[pad:, :] = jnp.where(lane_s < HEAD_DIM, kn,
                                    jnp.where(lane_s == sel_lane, 1.0, 0.0)).astype(BF16)
        kw_ref[pad:, :] = jnp.where(lane_s < HEAD_DIM, pltpu.roll(kn, HEAD_DIM, 1),
                                    0.0).astype(BF16)
        lane_p = lax.broadcasted_iota(jnp.int32, (pad, LANES), 1)
        before = jnp.where(lane_p == NSA_PAD_LANE, MASKED, 0.0).astype(BF16)
        ks_ref[:pad, :] = before
        kw_ref[:pad, :] = before
        vt_ref[:, pad:] = v2_ref[0].T.astype(BF16)
        vt_ref[:, :pad] = jnp.zeros((LANES, pad), BF16)

    lane = lax.broadcasted_iota(jnp.int32, (tq, LANES), 1)
    lo = lane < HEAD_DIM
    scale = HEAD_DIM ** -0.5 * LOG2E
    kc = kc_ref[0]
    vct = vct_ref[0, :HEAD_DIM, :]
    jj = lax.broadcasted_iota(jnp.int32, (LANES, LANES), 0)
    c0 = lax.broadcasted_iota(jnp.int32, (LANES, LANES), 1) * NSA_CMP_STRIDE
    j0 = jj * NSA_SEL_LEN
    overlap = jnp.where((c0 < j0 + NSA_SEL_LEN) & (c0 + NSA_CMP_LEN > j0), 1.0, 0.0).astype(BF16)
    jj32 = lax.broadcasted_iota(jnp.int32, (n_sel, tq), 0)
    in_sel = (lane >= NSA_SEL_LANE0) & (lane < NSA_SEL_LANE0 + n_sel)
    blocks = [dict() for _ in range(n_blk)]

    def prepare(u):
        blk = blocks[u]
        i = ip * n_blk + u
        q = q_ref[0, u * tq:(u + 1) * tq, :]
        q_base = []
        for pair in range(2):
            cols = slice(pair * LANES, (pair + 1) * LANES)
            pn = _pair_rms(q[:, cols], gq_ref[:, cols], lo) * scale
            for head in (jnp.where(lo, pn, 0.0), pltpu.roll(jnp.where(lo, 0.0, pn), HEAD_DIM, 1)):
                q_base.append(jnp.where(lane == NSA_PAD_LANE, 1.0, head))
        q_plain = [x.astype(BF16) for x in q_base]
        b0 = pl.multiple_of(i * tq, tq)
        blk.update(i=i, q_base=q_base, q_plain=q_plain,
                   ks_band=ks_ref[pl.ds(b0, band), :],
                   vs_band=vt_ref[:HEAD_DIM, pl.ds(b0, band)],
                   vw_band=vt_ref[HEAD_DIM:, pl.ds(b0, band)])
        blk['s_cmp'] = [_nt_dot(kc, q_plain[h]) + cmpb_ref[h, :, u * tq:(u + 1) * tq]
                        for h in range(N_HEADS)]
        kw_band = kw_ref[pl.ds(b0, band), :]
        for h in range(N_HEADS):
            win_ref[u, h] = _nt_dot(kw_band, q_plain[h])

    def window(u, h):
        _, l_w, a_w = _first_t(win_ref[u, h] + bwin_ref[h], blocks[u]['vw_band'])
        return a_w / l_w

    def compressed(u):
        blk = blocks[u]
        o_cmp = []
        p_sum = jnp.zeros((LANES, tq), F32)
        for h in range(N_HEADS):
            s_c = blk['s_cmp'][h]
            e_c = jnp.where(s_c > 0.5 * MASKED,
                            jnp.exp2(s_c - jnp.max(s_c, axis=0, keepdims=True)), 0.0)
            den = jnp.sum(e_c, axis=0, keepdims=True)
            p_c = e_c / jnp.where(den > 0.0, den, 1.0)
            o_cmp.append(_dot(vct, p_c.astype(BF16)))
            p_sum = p_sum + p_c
        hi, mid, lo3 = _split3(p_sum)
        blk['o_cmp'] = o_cmp
        blk['imp'] = (_dot(overlap, hi) + _dot(overlap, mid) + _dot(overlap, lo3))[0:n_sel]

    def select(u):
        blk = blocks[u]
        i = blk['i']
        cur = (i * tq + lax.broadcasted_iota(jnp.int32, (n_sel, tq), 1)) >> 6
        forced = (jj32 == 0) | (jj32 == cur) | (jj32 == cur - 1)
        imp = jnp.where(forced, FORCE, blk['imp'])
        imp = jnp.where(jj32 <= cur, imp, -FORCE)
        cnt = jnp.zeros((n_sel, tq), F32)
        for jp in range(n_sel):
            other = imp[jp:jp + 1, :]
            beats = (other > imp) | ((other == imp) & (jj32 > jp))
            cnt = cnt + jnp.where(beats, 1.0, 0.0)
        sel_neg = jnp.where(cnt < float(NSA_TOP_N), 0.0, MASKED)
        sel_neg = jnp.concatenate([sel_neg, jnp.zeros((LANES - n_sel, tq), F32)], axis=0).T
        sel_neg = pltpu.roll(sel_neg, NSA_SEL_LANE0, 1)
        band_block0 = NSA_SEL_LANE0 + ((i * tq - pad) >> 6)
        blk['q_band'] = [jnp.where(in_sel, sel_neg, x).astype(BF16) for x in blk['q_base']]
        blk['q_far'] = [jnp.where(in_sel, jnp.where(lane >= band_block0, MASKED, sel_neg),
                                  x).astype(BF16) for x in blk['q_base']]

    def stage_selected(u, h):
        stage_ref[u, h % 2] = _nt_dot(blocks[u]['ks_band'], blocks[u]['q_band'][h])

    def selected_band(u):
        blk = blocks[u]
        slc = []
        for h in range(N_HEADS):
            slc.append(_first_t(stage_ref[u, h % 2] + bslc_ref[h], blk['vs_band']))
            if h + 2 < N_HEADS:
                stage_selected(u, h + 2)
        blk['slc'] = slc

    def far_tile(u, kt):
        blk = blocks[u]
        off = pad + kt * NSA_FAR_TILE
        k_far = ks_ref[pl.ds(off, NSA_FAR_TILE), :]
        v_far = vt_ref[:HEAD_DIM, pl.ds(off, NSA_FAR_TILE)]
        q_far = blk['q_far']
        sc = [_nt_dot(k_far, q_far[0]), _nt_dot(k_far, q_far[1])]
        out = []
        for h in range(N_HEADS):
            if h + 2 < N_HEADS:
                sc.append(_nt_dot(k_far, q_far[h + 2]))
            out.append(_update_t(blk['slc'][h], sc[h], v_far, None))
        blk['slc'] = out

    def finish(u):
        blk = blocks[u]
        rows = slice(u * tq, (u + 1) * tq)
        g_t = _sigmoid(misc_ref[0, rows, :]).T
        heads = []
        for h in range(N_HEADS):
            _, l_s, a_s = blk['slc'][h]
            row = lambda n: g_t[MISC_G + n * N_HEADS + h:MISC_G + n * N_HEADS + h + 1, :]
            heads.append(row(0) * blk['o_cmp'][h] + row(1) * (a_s / l_s) + row(2) * blk['o_win'][h])
        o_ref[0, rows, :] = jnp.concatenate(heads, axis=0).T.astype(o_ref.dtype)

    both = range(n_blk)
    for u in both:
        prepare(u)
    for u in both:
        compressed(u)
    for u in both:
        blocks[u]['o_win'] = [window(u, 0), window(u, 1)]
    for u in both:
        select(u)
    for u in both:
        stage_selected(u, 0)
        stage_selected(u, 1)
    for u in both:
        blocks[u]['o_win'] += [window(u, 2), window(u, 3)]
    for u in both:
        selected_band(u)

    for c in range(s // (n_blk * tq)):
        @pl.when(ip == c)
        def _(c=c):
            saved = [blocks[u]['slc'] for u in both]
            n_far = [(max((c * n_blk + u) * tq - pad, 0) + NSA_FAR_TILE - 1) // NSA_FAR_TILE
                     for u in both]
            for kt in range(max(n_far)):
                for u in both:
                    if kt < n_far[u]:
                        far_tile(u, kt)
            for u in both:
                finish(u)
                blocks[u]['slc'] = saved[u]


def nsa_attention(y3, kc, vct, band_slc, band_win, cmpb, gq, gk2):
    b, s, _ = y3.shape
    n_blk = 2
    tq = n_blk * LANES
    band = NSA_WINDOW + LANES
    return pl.pallas_call(
        _nsa_kernel,
        grid=(b, s // tq),
        in_specs=[pl.BlockSpec((1, tq, MIX_W), lambda bi, i: (bi, i, COL_NSAQ // MIX_W)),
                  pl.BlockSpec((1, s, LANES), lambda bi, i: (bi, 0, COL_K2 // LANES)),
                  pl.BlockSpec((1, s, LANES), lambda bi, i: (bi, 0, COL_V2 // LANES)),
                  pl.BlockSpec((1, tq, LANES), lambda bi, i: (bi, i, COL_MISC // LANES)),
                  pl.BlockSpec((1, LANES, LANES), lambda bi, i: (bi, 0, 0)),
                  pl.BlockSpec((1, LANES, LANES), lambda bi, i: (bi, 0, 0)),
                  pl.BlockSpec(band_slc.shape, lambda bi, i: (0, 0, 0)),
                  pl.BlockSpec(band_win.shape, lambda bi, i: (0, 0, 0)),
                  pl.BlockSpec((N_HEADS, LANES, tq), lambda bi, i: (0, 0, i)),
                  pl.BlockSpec((1, MIX_W), lambda bi, i: (0, 0)),
                  pl.BlockSpec((1, LANES), lambda bi, i: (0, 0))],
        out_specs=pl.BlockSpec((1, tq, MIX_W), lambda bi, i: (bi, i, 0)),
        out_shape=jax.ShapeDtypeStruct((b, s, MIX_W), BF16),
        scratch_shapes=[pltpu.VMEM((s + NSA_WINDOW, LANES), BF16),
                        pltpu.VMEM((s + NSA_WINDOW, LANES), BF16),
                        pltpu.VMEM((LANES, s + NSA_WINDOW), BF16),
                        pltpu.VMEM((n_blk, N_HEADS, band, LANES), F32),
                        pltpu.VMEM((n_blk, 2, band, LANES), F32)],
        compiler_params=_cparams("parallel", "arbitrary"),
    )(y3, y3, y3, y3, kc, vct, band_slc, band_win, cmpb, gq, gk2)


def _merge_kernel(ysb_ref, ymla_ref, ynsa_ref, yfox_ref, h_ref, wgate_ref, wb_ref, wo_ref, x_ref,
                  o_ref):
    d = x_ref.shape[1]
    h = h_ref[...]
    u = None
    for n, y_ref in enumerate((ysb_ref, ymla_ref, ynsa_ref, yfox_ref)):
        gate = _sigmoid(_dot(h, wgate_ref[:, n * d:(n + 1) * d]))
        term = gate * _dot(y_ref[...], wb_ref[n])
        u = term if u is None else u + term
    o_ref[...] = x_ref[...] + _dot(u.astype(BF16), wo_ref[...])


def merge_branches(ys, h, wgate_all, wb_all, wo_all, layer, x, *, tm=512):
    t, d = x.shape
    yspec = pl.BlockSpec((tm, MIX_W), lambda i: (i, 0))
    return pl.pallas_call(
        _merge_kernel, grid=(t // tm,),
        in_specs=[yspec] * 4 + [
            pl.BlockSpec((tm, d), lambda i: (i, 0)),
            pl.BlockSpec((None, d, N_BRANCH * d), lambda i: (layer, 0, 0)),
            pl.BlockSpec((None,) + wb_all.shape[1:], lambda i: (layer, 0, 0, 0)),
            pl.BlockSpec((None, d, d), lambda i: (layer, 0, 0)),
            pl.BlockSpec((tm, d), lambda i: (i, 0))],
        out_specs=pl.BlockSpec((tm, d), lambda i: (i, 0)),
        out_shape=jax.ShapeDtypeStruct((t, d), F32),
        compiler_params=_cparams("parallel"),
    )(*ys, h, wgate_all, wb_all, wo_all, x)


def _mlp_ple_kernel(x_ref, g_ref, wu_ref, wd_ref, gp_ref, wg_ref, p_ref, wp_ref, o_ref,
                    h_ref, acc_ref):
    f = pl.program_id(1)

    @pl.when(f == 0)
    def _():
        h_ref[...] = _rms_rows(x_ref[...], g_ref[...]).astype(BF16)
        acc_ref[...] = jnp.zeros_like(acc_ref)

    a = jnp.maximum(_dot(h_ref[...], wu_ref[...]), 0.0)
    acc_ref[...] += _dot((a * a).astype(BF16), wd_ref[...])

    @pl.when(f == pl.num_programs(1) - 1)
    def _():
        x1 = x_ref[...] + acc_ref[...]
        gate = _sigmoid(_dot(_rms_rows(x1, gp_ref[...]).astype(BF16), wg_ref[...]))
        o_ref[...] = x1 + gate * _dot(p_ref[...].astype(BF16), wp_ref[...])


def mlp_ple(x, g, wu, wd, gp, wg, p_all, wp, layer, *, tm=1024, tf=1024):
    t, d = x.shape
    ff = wu.shape[2]
    return pl.pallas_call(
        _mlp_ple_kernel, grid=(t // tm, ff // tf),
        in_specs=[pl.BlockSpec((tm, d), lambda i, f: (i, 0)),
                  pl.BlockSpec((1, d), lambda i, f: (0, 0)),
                  pl.BlockSpec((None, d, tf), lambda i, f: (layer, 0, f)),
                  pl.BlockSpec((None, tf, d), lambda i, f: (layer, f, 0)),
                  pl.BlockSpec((1, d), lambda i, f: (0, 0)),
                  pl.BlockSpec((None, d, d), lambda i, f: (layer, 0, 0)),
                  pl.BlockSpec((None, tm, PLE_DIM), lambda i, f: (layer, i, 0)),
                  pl.BlockSpec((None, PLE_DIM, d), lambda i, f: (layer, 0, 0))],
        out_specs=pl.BlockSpec((tm, d), lambda i, f: (i, 0)),
        out_shape=jax.ShapeDtypeStruct((t, d), F32),
        scratch_shapes=[pltpu.VMEM((tm, d), BF16), pltpu.VMEM((tm, d), F32)],
        compiler_params=_cparams("parallel", "arbitrary"),
    )(x, g.reshape(1, d), wu, wd, gp.reshape(1, d), wg, p_all, wp)


def _pack_w_in(w):
    offs = np.concatenate([[0], np.cumsum(IN_WIDTHS)]).tolist()
    (sb_q, sb_k, sb_v, cq, ckv, kr, nsa_q, kc, vc, ks, vs, kw, vw, ng,
     fox_q, fox_k, fox_v, ff, gate) = [w[..., offs[n]:offs[n + 1]] for n in range(len(IN_WIDTHS))]
    z = lambda n: jnp.zeros(w.shape[:-1] + (n,), w.dtype)
    misc = jnp.concatenate([ff, ng, z(MISC_KR - MISC_G - 3 * N_HEADS), kr,
                            z(LANES - MISC_KR - MLA_ROPE)], axis=-1)
    w_f32_part = jnp.concatenate([cq, ckv, ks, kw, vs, vw, kc, vc, misc, nsa_q], axis=-1)
    w_bf16_part = jnp.concatenate([sb_q, sb_k, sb_v, fox_q, fox_k, fox_v], axis=-1)
    return w_f32_part.astype(BF16), w_bf16_part.astype(BF16), gate.astype(BF16)


def _head_slots(w, width):
    k = w.shape[0]
    w = w.reshape(k, N_HEADS, width)
    return jnp.pad(w, ((0, 0), (0, 0), (0, LANES - width))).reshape(k, N_HEADS * LANES)


def _rope_tables(s):
    half = MLA_ROPE // 2
    inv = jnp.exp(-math.log(ROPE_THETA) * jnp.arange(half, dtype=F32) / half)
    ang = jnp.arange(s, dtype=F32)[:, None] * inv[None, :]
    cos, sin = jnp.cos(ang), jnp.sin(ang)
    ones = jnp.ones((s, MLA_NOPE), F32)
    zeros = lambda n: jnp.zeros((s, n), F32)
    tail = LANES - MLA_QK
    cos_t = jnp.concatenate([ones, cos, cos, jnp.ones((s, tail), F32)], axis=1)
    sa_t = jnp.concatenate([zeros(MLA_NOPE), -sin, zeros(half), zeros(tail)], axis=1)
    sb_t = jnp.concatenate([zeros(MLA_NOPE), zeros(half), sin, zeros(tail)], axis=1)
    return cos_t, sa_t, sb_t


def _pad_lanes(v, left=0):
    v = v.reshape(1, -1)
    return jnp.pad(v, ((0, 0), (left, LANES - left - v.shape[1])))


def _pad_to_lanes(w):
    return jnp.pad(w, ((0, 0), (0, LANES - w.shape[1])))


def kernel(x, p, rel_bias, norm_mix_g, w_in, mla_cq_norm_g, mla_ckv_norm_g, mla_w_uq, mla_w_ukv,
           mla_qn_g, mla_kn_g, nsa_pe_k, nsa_pe_v, nsa_w1_k, nsa_w2_k, nsa_w1_v, nsa_w2_v,
           nsa_qn_g, nsa_kn_g, fox_f_bias, fox_qn_g, fox_kn_g, w_branch, w_o, norm_mlp_g,
           w_mlp_up, w_mlp_down, norm_ple_g, w_ple_gate, w_ple_proj):
    b, s, d = x.shape
    t = b * s
    xf = x.reshape(t, d)
    p_all = p.reshape(DEPTH, t, PLE_DIM)
    band_slc, band_win, cmpb = rel_bias_tables(rel_bias.astype(F32), s)
    cos_t, sa_t, sb_t = _rope_tables(s)
    n_cmp_in = NSA_CMP_STRIDE * HEAD_DIM
    w_a, w_b, w_gate = _pack_w_in(w_in)
    wb_all, wo_all = w_branch.astype(BF16), w_o.astype(BF16)
    wu_all, wd_all = w_mlp_up.astype(BF16), w_mlp_down.astype(BF16)
    wg_all, wp_all = w_ple_gate.astype(BF16), w_ple_proj.astype(BF16)

    for i in range(DEPTH):
        ya, h = norm_proj(xf, norm_mix_g[i], w_a, i)
        yb = matmul_bf16(h, w_b, i)
        y3 = ya.reshape(b, s, N_F32)
        yb3 = yb.reshape(b, s, N_BF16)

        y_sb = sb_attention(yb3)

        wukv = mla_w_ukv[i].reshape(MLA_KV_LORA, N_HEADS, MLA_NOPE + MLA_V)
        q_m, k_m, vt_m = mla_prep(
            y3, mla_cq_norm_g[i].reshape(1, -1), mla_ckv_norm_g[i].reshape(1, -1),
            _head_slots(mla_w_uq[i], MLA_QK).astype(BF16),
            _head_slots(wukv[:, :, :MLA_NOPE].reshape(MLA_KV_LORA, -1), MLA_NOPE).astype(BF16),
            wukv[:, :, MLA_NOPE:].reshape(MLA_KV_LORA, -1).T.astype(BF16),
            _pad_lanes(mla_qn_g[i]), _pad_lanes(mla_kn_g[i]), cos_t, sa_t, sb_t)
        y_mla = mla_attention(q_m, k_m, vt_m)

        pad_sq = lambda w: jnp.pad(_pad_to_lanes(w), ((0, LANES - w.shape[0]), (0, 0)))
        pe_kv = jnp.concatenate([nsa_pe_k[i], nsa_pe_v[i]], axis=1)
        kc, vct = nsa_compress(
            y3, pe_kv[:NSA_CMP_STRIDE], pe_kv[NSA_CMP_STRIDE:],
            *_compress_weights(nsa_w1_k[i], 0), *_compress_weights(nsa_w1_v[i], HEAD_DIM),
            pad_sq(nsa_w2_k[i]).astype(BF16), pad_sq(nsa_w2_v[i].T).astype(BF16),
            _pad_lanes(nsa_kn_g[i, 0]))
        y_nsa = nsa_attention(
            y3, kc, vct, band_slc, band_win, cmpb,
            jnp.tile(nsa_qn_g[i], N_HEADS).reshape(1, -1),
            jnp.concatenate([nsa_kn_g[i, 1], nsa_kn_g[i, 2]]).reshape(1, -1))

        cum = fox_cum(y3, _pad_lanes(fox_f_bias[i], MISC_F))
        y_fox = fox_attention(yb3, cum, jnp.tile(fox_qn_g[i], N_HEADS).reshape(1, -1),
                              jnp.tile(fox_kn_g[i], N_HEADS).reshape(1, -1))

        ys = [a.reshape(t, MIX_W) for a in (y_sb, y_mla, y_nsa, y_fox)]
        xf = merge_branches(ys, h, w_gate, wb_all, wo_all, i, xf)
        xf = mlp_ple(xf, norm_mlp_g[i], wu_all, wd_all, norm_ple_g[i], wg_all, p_all, wp_all, i)
    return xf.reshape(b, s, d)
```

```python
import functools
import math

import numpy as np
import jax
import jax.numpy as jnp
from jax import lax
from jax.experimental import pallas as pl
from jax.experimental.pallas import tpu as pltpu

F32 = jnp.float32
BF16 = jnp.bfloat16

D_MODEL = 1024
DEPTH = 4
HEAD_DIM = 64
N_HEADS = 4
MIX_W = N_HEADS * HEAD_DIM
N_BRANCH = 4
EPS = 1e-6
FORCE = 1e9
MLA_Q_LORA = 384
MLA_KV_LORA = 128
MLA_NOPE = 64
MLA_ROPE = 32
MLA_V = 64
MLA_QK = MLA_NOPE + MLA_ROPE
ROPE_THETA = 10000.0
NSA_CMP_LEN = 32
NSA_CMP_STRIDE = 16
NSA_SEL_LEN = 64
NSA_TOP_N = 16
NSA_WINDOW = 512
REL_BUCKETS = 32
REL_MAX_DIST = 128
D_FF = 4 * D_MODEL
PLE_DIM = 256

LANES = 128
MASKED = -1e30
VMEM_LIMIT = 56 * 1024 * 1024

IN_WIDTHS = ((MIX_W,) * 3
             + (MLA_Q_LORA, MLA_KV_LORA, MLA_ROPE)
             + (MIX_W,) + (HEAD_DIM,) * 6 + (3 * N_HEADS,)
             + (MIX_W,) * 3 + (N_HEADS,)
             + (N_BRANCH * D_MODEL,))

COL_CQ = 0
COL_CKV = 384
COL_K2 = 512
COL_V2 = 640
COL_KCVC = 768
COL_MISC = 896
COL_NSAQ = 1024
N_F32 = 1280
COL_SB = 0
COL_FOX = 768
N_BF16 = 1536
MISC_F = 0
MISC_G = 4
MISC_KR = 32

NSA_PAD_LANE = HEAD_DIM
NSA_SEL_LANE0 = HEAD_DIM + 1
NSA_FAR_TILE = 512


def _cparams(*sem):
    return pltpu.CompilerParams(dimension_semantics=sem, vmem_limit_bytes=VMEM_LIMIT)


def _nt_dot(a, b):
    return lax.dot_general(a, b, (((1,), (1,)), ((), ())), preferred_element_type=F32)


def _dot(a, b):
    return jnp.dot(a, b, preferred_element_type=F32)


def _split3(x):
    hi = x.astype(BF16)
    r1 = x - hi.astype(F32)
    mid = r1.astype(BF16)
    lo = (r1 - mid.astype(F32)).astype(BF16)
    return hi, mid, lo


def _softplus(z):
    return jnp.maximum(z, 0.0) + jnp.log(1.0 + jnp.exp(-jnp.abs(z)))


def _sigmoid(z):
    return 1.0 / (1.0 + jnp.exp(-z))


def _rms_rows(x, g):
    r = lax.rsqrt(jnp.mean(x * x, axis=-1, keepdims=True) + EPS)
    return x * r * g


def _pair_rms(x, g, lo):
    x2 = x * x
    s0 = jnp.sum(jnp.where(lo, x2, 0.0), axis=-1, keepdims=True)
    s1 = jnp.sum(jnp.where(lo, 0.0, x2), axis=-1, keepdims=True)
    r = jnp.where(lo, lax.rsqrt(s0 / HEAD_DIM + EPS), lax.rsqrt(s1 / HEAD_DIM + EPS))
    return x * r * g


LOG2E = math.log2(math.e)


def _first_t(s, vt):
    m = jnp.max(s, axis=0, keepdims=True)
    p = jnp.exp2(s - m)
    return m, jnp.sum(p, axis=0, keepdims=True), _dot(vt, p.astype(BF16))


def _update_t(carry, s, vt, mask):
    m, l, acc = carry
    if mask is not None:
        s = jnp.where(mask, s, MASKED)
    m_new = jnp.maximum(m, jnp.max(s, axis=0, keepdims=True))
    p = jnp.exp2(s - m_new)
    if mask is not None:
        p = jnp.where(mask, p, 0.0)
    alpha = jnp.exp2(m - m_new)
    l = alpha * l + jnp.sum(p, axis=0, keepdims=True)
    acc = alpha * acc + _dot(vt, p.astype(BF16))
    return m_new, l, acc


def _init_t(tq):
    return (jnp.full((1, tq), MASKED, F32), jnp.zeros((1, tq), F32),
            jnp.zeros((HEAD_DIM, tq), F32))


def _causal_flash_static(c, tq, tk, score_fn, vt_fn):
    assert tk % tq == 0
    heads = range(N_HEADS)
    diff = (lax.broadcasted_iota(jnp.int32, (tk, tq), 0)
            - lax.broadcasted_iota(jnp.int32, (tk, tq), 1))
    n_full = (c * tq) // tk
    cur = [score_fn(h, 0) for h in heads]
    state = [_init_t(tq) for _ in heads]
    for kt in range(n_full + 1):
        nxt = [score_fn(h, (kt + 1) * tk) for h in heads] if kt < n_full else None
        mask = (diff <= c * tq - kt * tk) if kt == n_full else None
        state = [_update_t(state[h], cur[h], vt_fn(h, kt * tk), mask) for h in heads]
        cur = nxt
    return jnp.concatenate([acc / l for (_, l, acc) in state], axis=0)


def _augment(x, terms, col, lane, h, key_side):
    live = (lane < HEAD_DIM) if h == 0 else (lane >= HEAD_DIM)
    a0 = HEAD_DIM if h == 0 else 0
    c0, o0 = (a0, a0 + 3) if key_side else (a0 + 3, a0)
    src = lax.broadcasted_iota(jnp.int32, (3 * LANES, LANES), 0)
    dst = lax.broadcasted_iota(jnp.int32, (3 * LANES, LANES), 1)
    place = jnp.where(((src & (LANES - 1)) == col) & (dst == c0 + (src >> 7)),
                      -1.0 if key_side else 1.0, 0.0).astype(BF16)
    ones = jnp.where((lane >= o0) & (lane < o0 + 3), 1.0, 0.0)
    return jnp.where(live, x, _dot(terms, place) + ones)


def _norm_proj_kernel(x_ref, g_ref, w_ref, o_ref, h_ref):
    h = _rms_rows(x_ref[...], g_ref[...]).astype(BF16)
    h_ref[...] = h
    o_ref[...] = _dot(h, w_ref[...])


def norm_proj(x, g, w_all, layer, *, tm=1024):
    t, d = x.shape
    n = w_all.shape[2]
    return pl.pallas_call(
        _norm_proj_kernel,
        grid=(t // tm,),
        in_specs=[pl.BlockSpec((tm, d), lambda i: (i, 0)),
                  pl.BlockSpec((1, d), lambda i: (0, 0)),
                  pl.BlockSpec((None, d, n), lambda i: (layer, 0, 0))],
        out_specs=[pl.BlockSpec((tm, n), lambda i: (i, 0)),
                   pl.BlockSpec((tm, d), lambda i: (i, 0))],
        out_shape=[jax.ShapeDtypeStruct((t, n), F32), jax.ShapeDtypeStruct((t, d), BF16)],
        compiler_params=_cparams("parallel"),
    )(x, g.reshape(1, d), w_all)


def _matmul_kernel(h_ref, w_ref, o_ref):
    o_ref[...] = _dot(h_ref[...], w_ref[...]).astype(o_ref.dtype)


def matmul_bf16(h, w_all, layer, *, tm=2048, tn=512):
    t, d = h.shape
    n = w_all.shape[2]
    return pl.pallas_call(
        _matmul_kernel,
        grid=(t // tm, n // tn),
        in_specs=[pl.BlockSpec((tm, d), lambda i, j: (i, 0)),
                  pl.BlockSpec((None, d, tn), lambda i, j: (layer, 0, j))],
        out_specs=pl.BlockSpec((tm, tn), lambda i, j: (i, j)),
        out_shape=jax.ShapeDtypeStruct((t, n), BF16),
        compiler_params=_cparams("parallel", "parallel"),
    )(h, w_all)


def _sb_kernel(q_ref, k_ref, v_ref, o_ref, vt_ref, *, tq, tk):
    i = pl.program_id(1)

    @pl.when(i == 0)
    def _():
        vt_ref[...] = v_ref[0].astype(F32).T.astype(BF16)

    sub = LANES
    n_sub = tk // sub
    heads = range(N_HEADS)
    lane = lax.broadcasted_iota(jnp.int32, (tq, LANES), 1)
    lo = lane < HEAD_DIM
    qh = []
    for p in range(N_HEADS // 2):
        q = q_ref[0, :, p * LANES:(p + 1) * LANES].astype(F32) * (HEAD_DIM ** -0.5)
        qh += [jnp.where(lo, q, 0.0).astype(BF16), jnp.where(lo, 0.0, q).astype(BF16)]
    rr = lax.broadcasted_iota(jnp.int32, (sub, sub), 0)
    cc = lax.broadcasted_iota(jnp.int32, (sub, sub), 1)
    upper = jnp.where(cc >= rr, 1.0, 0.0).astype(BF16)
    upper2 = jnp.concatenate([upper, upper], axis=1)
    diff = (lax.broadcasted_iota(jnp.int32, (tk, tq), 0)
            - lax.broadcasted_iota(jnp.int32, (tk, tq), 1))

    def logits_at(kt):
        return tuple(_nt_dot(k_ref[0, pl.ds(kt * tk, tk), (h // 2) * LANES:(h // 2 + 1) * LANES],
                             qh[h]) for h in heads)

    def tile(c, kt, zs, carry, masked):
        off = kt * tk
        past = (diff < c * tq - off) if masked else None
        out = []
        for h in heads:
            run, acc = carry[h]
            z = zs[h]
            sp = _softplus(z)
            spm = jnp.where(past, sp, 0.0) if masked else sp
            ws = [None] * n_sub
            for j in reversed(range(n_sub)):
                sl = slice(j * sub, (j + 1) * sub)
                spj = spm[sl]
                hi = spj.astype(BF16)
                mid = (spj - hi.astype(F32)).astype(BF16)
                tail = _dot(upper2, jnp.concatenate([hi, mid], axis=0))
                w = jnp.exp(z[sl] - tail - run)
                if masked:
                    w = jnp.where(past[sl], w, 0.0)
                ws[j] = w.astype(BF16)
                run = run + tail[0:1, :]
            vt = vt_ref[h * HEAD_DIM:(h + 1) * HEAD_DIM, pl.ds(off, tk)]
            out.append((run, acc + _dot(vt, jnp.concatenate(ws, axis=0))))
        return tuple(out)

    zero = (jnp.zeros((1, tq), F32), jnp.zeros((HEAD_DIM, tq), F32))
    for c in range(k_ref.shape[1] // tq):
        @pl.when(i == c)
        def _(c=c):
            n_full = (c * tq) // tk
            zs = logits_at(n_full)
            carry = (zero,) * N_HEADS
            for kt in range(n_full, -1, -1):
                zs_next = logits_at(kt - 1) if kt > 0 else None
                carry = tile(c, kt, zs, carry, kt == n_full)
                zs = zs_next
            o_t = jnp.concatenate([acc for _, acc in carry], axis=0)
            o_ref[0] = o_t.T.astype(o_ref.dtype)


def sb_attention(y3, *, tq=512, tk=512):
    b, s, _ = y3.shape
    qb = COL_SB // MIX_W
    return pl.pallas_call(
        functools.partial(_sb_kernel, tq=tq, tk=tk),
        grid=(b, s // tq),
        in_specs=[pl.BlockSpec((1, tq, MIX_W), lambda bi, i: (bi, i, qb)),
                  pl.BlockSpec((1, s, MIX_W), lambda bi, i: (bi, 0, qb + 1)),
                  pl.BlockSpec((1, s, MIX_W), lambda bi, i: (bi, 0, qb + 2))],
        out_specs=pl.BlockSpec((1, tq, MIX_W), lambda bi, i: (bi, i, 0)),
        out_shape=jax.ShapeDtypeStruct((b, s, MIX_W), BF16),
        scratch_shapes=[pltpu.VMEM((MIX_W, s), BF16)],
        compiler_params=_cparams("parallel", "arbitrary"),
    )(y3, y3, y3)


def _fox_cum_kernel(misc_ref, fb_ref, cum_ref, *, s):
    rr = lax.broadcasted_iota(jnp.int32, (LANES, LANES), 0)
    cc = lax.broadcasted_iota(jnp.int32, (LANES, LANES), 1)
    lower = jnp.where(cc <= rr, 1.0, 0.0).astype(BF16)

    def body(n, carry):
        off = pl.multiple_of(n * LANES, LANES)
        x = misc_ref[0, pl.ds(off, LANES), :] + fb_ref[...]
        log_f = jnp.minimum(x, 0.0) - jnp.log(1.0 + jnp.exp(-jnp.abs(x)))
        hi, mid, lo3 = _split3(log_f)
        c = _dot(lower, hi) + _dot(lower, mid) + _dot(lower, lo3) + carry
        cum_ref[0, pl.ds(off, LANES), :] = c
        return c[LANES - 1:LANES, :]

    lax.fori_loop(0, s // LANES, body, jnp.zeros((1, LANES), F32))


def fox_cum(y3, fbias_row):
    b, s, _ = y3.shape
    return pl.pallas_call(
        functools.partial(_fox_cum_kernel, s=s),
        grid=(b,),
        in_specs=[pl.BlockSpec((1, s, LANES), lambda bi: (bi, 0, COL_MISC // LANES)),
                  pl.BlockSpec((1, LANES), lambda bi: (0, 0))],
        out_specs=pl.BlockSpec((1, s, LANES), lambda bi: (bi, 0, 0)),
        out_shape=jax.ShapeDtypeStruct((b, s, LANES), F32),
        compiler_params=_cparams("parallel"),
    )(y3, fbias_row)


def _fox_kernel(q_ref, k_ref, v_ref, cum_ref, gq_ref, gk_ref, o_ref, ka_ref, vt_ref, *, tq, tk):
    i = pl.program_id(1)
    s = k_ref.shape[1]
    pairs = range(N_HEADS // 2)

    @pl.when(i == 0)
    def _():
        lane_s = lax.broadcasted_iota(jnp.int32, (s, LANES), 1)
        terms = jnp.concatenate(_split3(cum_ref[0] * LOG2E), axis=1)
        for p in pairs:
            cols = slice(p * LANES, (p + 1) * LANES)
            kn = _pair_rms(k_ref[0, :, cols].astype(F32), gk_ref[:, cols], lane_s < HEAD_DIM)
            for h in range(2):
                ka_ref[:, (2 * p + h) * LANES:(2 * p + h + 1) * LANES] = _augment(
                    kn, terms, MISC_F + 2 * p + h, lane_s, h, True).astype(BF16)
        vt_ref[...] = v_ref[0].astype(F32).T.astype(BF16)

    lane = lax.broadcasted_iota(jnp.int32, (tq, LANES), 1)
    terms_q = jnp.concatenate(
        _split3(cum_ref[0, pl.ds(pl.multiple_of(i * tq, tq), tq), :] * LOG2E), axis=1)
    qa = []
    for p in pairs:
        cols = slice(p * LANES, (p + 1) * LANES)
        qn = _pair_rms(q_ref[0, :, cols].astype(F32), gq_ref[:, cols], lane < HEAD_DIM)
        qn = qn * (HEAD_DIM ** -0.5 * LOG2E)
        qa += [_augment(qn, terms_q, MISC_F + 2 * p + h, lane, h, False).astype(BF16)
               for h in range(2)]

    def scores(h, off):
        return _nt_dot(ka_ref[pl.ds(off, tk), h * LANES:(h + 1) * LANES], qa[h])

    def values_t(h, off):
        return vt_ref[h * HEAD_DIM:(h + 1) * HEAD_DIM, pl.ds(off, tk)]

    for c in range(s // tq):
        @pl.when(i == c)
        def _(c=c):
            o_ref[0] = _causal_flash_static(c, tq, tk, scores, values_t).T.astype(o_ref.dtype)


def fox_attention(y3, cum, gq, gk, *, tq=512, tk=512):
    b, s, _ = y3.shape
    qb = COL_FOX // MIX_W
    return pl.pallas_call(
        functools.partial(_fox_kernel, tq=tq, tk=tk),
        grid=(b, s // tq),
        in_specs=[pl.BlockSpec((1, tq, MIX_W), lambda bi, i: (bi, i, qb)),
                  pl.BlockSpec((1, s, MIX_W), lambda bi, i: (bi, 0, qb + 1)),
                  pl.BlockSpec((1, s, MIX_W), lambda bi, i: (bi, 0, qb + 2)),
                  pl.BlockSpec((1, s, LANES), lambda bi, i: (bi, 0, 0)),
                  pl.BlockSpec((1, MIX_W), lambda bi, i: (0, 0)),
                  pl.BlockSpec((1, MIX_W), lambda bi, i: (0, 0))],
        out_specs=pl.BlockSpec((1, tq, MIX_W), lambda bi, i: (bi, i, 0)),
        out_shape=jax.ShapeDtypeStruct((b, s, MIX_W), BF16),
        scratch_shapes=[pltpu.VMEM((s, N_HEADS * LANES), BF16), pltpu.VMEM((MIX_W, s), BF16)],
        compiler_params=_cparams("parallel", "arbitrary"),
    )(y3, y3, y3, cum, gq, gk)


def _mla_prep_kernel(cq_ref, ckv_ref, misc_ref, gcq_ref, gckv_ref, wuq_ref, wuk_ref, wuv_ref,
                     qg_ref, kg_ref, cos_ref, sa_ref, sb_ref, q_out, k_out, v_out, *, ts):
    hq = _rms_rows(cq_ref[0], gcq_ref[...]).astype(BF16)
    hkv = _rms_rows(ckv_ref[0], gckv_ref[...]).astype(BF16)
    q = _dot(hq, wuq_ref[...])
    kn = _dot(hkv, wuk_ref[...])
    v_out[0] = _nt_dot(wuv_ref[...], hkv).astype(v_out.dtype)

    lane = lax.broadcasted_iota(jnp.int32, (ts, LANES), 1)
    misc = misc_ref[0]
    kr = pltpu.roll(jnp.where((lane >= MISC_KR) & (lane < MISC_KR + MLA_ROPE), misc, 0.0),
                    MLA_NOPE - MISC_KR, 1)
    cos, sin = cos_ref[...], sa_ref[...] + sb_ref[...]
    half = MLA_ROPE // 2

    kk = lax.broadcasted_iota(jnp.int32, (2 * LANES, LANES), 0) & (LANES - 1)
    ll = lax.broadcasted_iota(jnp.int32, (2 * LANES, LANES), 1)
    ones2 = jnp.ones((2 * LANES, LANES), BF16)
    first = (ll >= MLA_NOPE) & (ll < MLA_NOPE + half)
    second = (ll >= MLA_NOPE + half) & (ll < MLA_QK)
    swap2 = jnp.where((first & (kk == ll + half)) | (second & (kk == ll - half)),
                      1.0, 0.0).astype(BF16)

    def split2(x):
        hi = x.astype(BF16)
        return jnp.concatenate([hi, (x - hi.astype(F32)).astype(BF16)], axis=1)

    def norm_rope(t, g):
        t = t * lax.rsqrt(_dot(split2(t * t), ones2) / MLA_QK + EPS) * g
        return t * cos + _dot(split2(t), swap2) * sin

    for h in range(N_HEADS):
        sl = slice(h * LANES, (h + 1) * LANES)
        q_out[0, :, sl] = (norm_rope(q[:, sl], qg_ref[...])
                           * (MLA_QK ** -0.5 * LOG2E)).astype(q_out.dtype)
        k_out[0, :, sl] = norm_rope(kn[:, sl] + kr, kg_ref[...]).astype(k_out.dtype)


def mla_prep(y3, gcq, gckv, wuq, wuk, wuv, qg, kg, cos, sa, sb, *, ts=512):
    b, s, _ = y3.shape
    const = lambda shape: pl.BlockSpec(shape, lambda bi, i: (0,) * len(shape))
    return pl.pallas_call(
        functools.partial(_mla_prep_kernel, ts=ts),
        grid=(b, s // ts),
        in_specs=[pl.BlockSpec((1, ts, MLA_Q_LORA), lambda bi, i: (bi, i, COL_CQ // MLA_Q_LORA)),
                  pl.BlockSpec((1, ts, LANES), lambda bi, i: (bi, i, COL_CKV // LANES)),
                  pl.BlockSpec((1, ts, LANES), lambda bi, i: (bi, i, COL_MISC // LANES)),
                  const((1, MLA_Q_LORA)), const((1, MLA_KV_LORA)),
                  const((MLA_Q_LORA, N_HEADS * LANES)), const((MLA_KV_LORA, N_HEADS * LANES)),
                  const((MIX_W, MLA_KV_LORA)), const((1, LANES)), const((1, LANES)),
                  pl.BlockSpec((ts, LANES), lambda bi, i: (i, 0)),
                  pl.BlockSpec((ts, LANES), lambda bi, i: (i, 0)),
                  pl.BlockSpec((ts, LANES), lambda bi, i: (i, 0))],
        out_specs=[pl.BlockSpec((1, ts, N_HEADS * LANES), lambda bi, i: (bi, i, 0)),
                   pl.BlockSpec((1, ts, N_HEADS * LANES), lambda bi, i: (bi, i, 0)),
                   pl.BlockSpec((1, MIX_W, ts), lambda bi, i: (bi, 0, i))],
        out_shape=[jax.ShapeDtypeStruct((b, s, N_HEADS * LANES), BF16),
                   jax.ShapeDtypeStruct((b, s, N_HEADS * LANES), BF16),
                   jax.ShapeDtypeStruct((b, MIX_W, s), BF16)],
        compiler_params=_cparams("parallel", "parallel"),
    )(y3, y3, y3, gcq, gckv, wuq, wuk, wuv, qg, kg, cos, sa, sb)


def _mla_kernel(q_ref, k_ref, vt_ref, o_ref, *, tq, tk):
    i = pl.program_id(1)
    qh = [q_ref[0, :, h * LANES:(h + 1) * LANES] for h in range(N_HEADS)]

    def scores(h, off):
        return _nt_dot(k_ref[0, pl.ds(off, tk), h * LANES:(h + 1) * LANES], qh[h])

    def values_t(h, off):
        return vt_ref[0, h * HEAD_DIM:(h + 1) * HEAD_DIM, pl.ds(off, tk)]

    for c in range(k_ref.shape[1] // tq):
        @pl.when(i == c)
        def _(c=c):
            o_ref[0] = _causal_flash_static(c, tq, tk, scores, values_t).T.astype(o_ref.dtype)


def mla_attention(q, k, vt, *, tq=512, tk=512):
    b, s, _ = q.shape
    return pl.pallas_call(
        functools.partial(_mla_kernel, tq=tq, tk=tk),
        grid=(b, s // tq),
        in_specs=[pl.BlockSpec((1, tq, N_HEADS * LANES), lambda bi, i: (bi, i, 0)),
                  pl.BlockSpec((1, s, N_HEADS * LANES), lambda bi, i: (bi, 0, 0)),
                  pl.BlockSpec((1, MIX_W, s), lambda bi, i: (bi, 0, 0))],
        out_specs=pl.BlockSpec((1, tq, MIX_W), lambda bi, i: (bi, i, 0)),
        out_shape=jax.ShapeDtypeStruct((b, s, MIX_W), BF16),
        compiler_params=_cparams("parallel", "arbitrary"),
    )(q, k, vt)


def _rel_bias_tile(dist, tab_ref, h):
    max_exact = REL_BUCKETS // 2
    d = jnp.maximum(dist, 0)
    large = max_exact + (jnp.log(jnp.maximum(d, 1).astype(F32) / max_exact)
                         / math.log(REL_MAX_DIST / max_exact)
                         * (REL_BUCKETS - max_exact)).astype(jnp.int32)
    large = jnp.minimum(large, REL_BUCKETS - 1)
    bucket = jnp.where(d < max_exact, d, large)
    out = jnp.zeros(dist.shape, F32)
    for bkt in range(REL_BUCKETS):
        out = jnp.where(bucket == bkt, tab_ref[bkt, h], out)
    return out


def _band_bias_kernel(tab_ref, slc_ref, win_ref):
    r = pl.program_id(0)
    kj = r * LANES + lax.broadcasted_iota(jnp.int32, (LANES, LANES), 0)
    qi = lax.broadcasted_iota(jnp.int32, (LANES, LANES), 1)
    dist = qi + NSA_WINDOW - kj
    for h in range(N_HEADS):
        delta = (_rel_bias_tile(dist, tab_ref, h) - tab_ref[REL_BUCKETS - 1, h]) * LOG2E
        slc_ref[h] = jnp.where(dist >= 0, delta, MASKED)
        win_ref[h] = jnp.where((dist >= 0) & (dist < NSA_WINDOW), delta, MASKED)


def _cmp_bias_kernel(tab_ref, o_ref):
    i = pl.program_id(0)
    c = lax.broadcasted_iota(jnp.int32, (LANES, LANES), 0)
    s = i * LANES + lax.broadcasted_iota(jnp.int32, (LANES, LANES), 1)
    dist = s - (c * NSA_CMP_STRIDE + NSA_CMP_LEN - 1)
    for h in range(N_HEADS):
        o_ref[h] = jnp.where(dist >= 0, _rel_bias_tile(dist, tab_ref, h) * LOG2E, MASKED)


def rel_bias_tables(rel_bias, s):
    assert REL_BUCKETS == 32 and REL_MAX_DIST == 128 and NSA_WINDOW >= 113
    band = NSA_WINDOW + LANES
    smem = pl.BlockSpec(memory_space=pltpu.SMEM)
    band_spec = pl.BlockSpec((N_HEADS, LANES, LANES), lambda r: (0, r, 0))
    band_shape = jax.ShapeDtypeStruct((N_HEADS, band, LANES), F32)
    band_slc, band_win = pl.pallas_call(
        _band_bias_kernel, grid=(band // LANES,), in_specs=[smem],
        out_specs=[band_spec, band_spec], out_shape=[band_shape, band_shape],
    )(rel_bias)
    cmpb = pl.pallas_call(
        _cmp_bias_kernel, grid=(s // LANES,), in_specs=[smem],
        out_specs=pl.BlockSpec((N_HEADS, LANES, LANES), lambda i: (0, 0, i)),
        out_shape=jax.ShapeDtypeStruct((N_HEADS, LANES, s), F32),
    )(rel_bias)
    return band_slc, band_win, cmpb


def _nsa_compress_kernel(kv_ref, pea_ref, peb_ref, w1ka_ref, w1kb_ref, w1va_ref, w1vb_ref,
                         w2k_ref, w2v_ref, gk_ref, kc_ref, vct_ref):
    n_blk = kv_ref.shape[1] // NSA_CMP_STRIDE
    views = [kv_ref[0, pl.ds(l, n_blk, stride=NSA_CMP_STRIDE), :] for l in range(NSA_CMP_STRIDE)]
    first = jnp.concatenate([(views[l] + pea_ref[l:l + 1, :]).astype(BF16)
                             for l in range(NSA_CMP_STRIDE)], axis=1)
    second = jnp.concatenate([(views[l] + peb_ref[l:l + 1, :]).astype(BF16)
                              for l in range(NSA_CMP_STRIDE)], axis=1)

    def hidden(wa_ref, wb_ref):
        pre = _dot(first, wa_ref[...]) + pltpu.roll(_dot(second, wb_ref[...]), n_blk - 1, 0)
        return (pre * _sigmoid(pre)).astype(BF16)

    kc = _dot(hidden(w1ka_ref, w1kb_ref), w2k_ref[...])
    kc = kc * lax.rsqrt(jnp.sum(kc * kc, axis=-1, keepdims=True) / HEAD_DIM + EPS) * gk_ref[...]
    kc_ref[0] = kc.astype(kc_ref.dtype)
    vct_ref[0] = _nt_dot(w2v_ref[...], hidden(w1va_ref, w1vb_ref)).astype(vct_ref.dtype)


def nsa_compress(y3, pea, peb, w1ka, w1kb, w1va, w1vb, w2k, w2v, gk):
    b, s, _ = y3.shape
    assert s // NSA_CMP_STRIDE == LANES
    const = lambda a: pl.BlockSpec(a.shape, lambda bi: (0,) * a.ndim)
    out = pl.BlockSpec((1, LANES, LANES), lambda bi: (bi, 0, 0))
    consts = (pea, peb, w1ka, w1kb, w1va, w1vb, w2k, w2v, gk)
    return pl.pallas_call(
        _nsa_compress_kernel, grid=(b,),
        in_specs=[pl.BlockSpec((1, s, LANES), lambda bi: (bi, 0, COL_KCVC // LANES))]
        + [const(a) for a in consts],
        out_specs=[out, out],
        out_shape=[jax.ShapeDtypeStruct((b, LANES, LANES), BF16)] * 2,
        compiler_params=_cparams("parallel"),
    )(y3, *consts)


def _compress_weights(w1, lane0):
    w = w1.reshape(NSA_CMP_LEN, HEAD_DIM, HEAD_DIM)
    w = jnp.pad(w, ((0, 0), (lane0, LANES - HEAD_DIM - lane0), (0, LANES - HEAD_DIM)))
    w = w.reshape(2, NSA_CMP_STRIDE * LANES, LANES).astype(BF16)
    return w[0], w[1]


def _nsa_kernel(q_ref, k2_ref, v2_ref, misc_ref, kc_ref, vct_ref, bslc_ref, bwin_ref, cmpb_ref,
                gq_ref, gk2_ref, o_ref, ks_ref, kw_ref, vt_ref, win_ref, stage_ref):
    tq = LANES
    n_blk = q_ref.shape[1] // tq
    s = k2_ref.shape[1]
    pad = NSA_WINDOW
    band = NSA_WINDOW + tq
    n_sel = s // NSA_SEL_LEN
    assert NSA_SEL_LANE0 + n_sel <= LANES and pad % NSA_FAR_TILE == 0
    ip = pl.program_id(1)

    @pl.when(ip == 0)
    def _():
        lane_s = lax.broadcasted_iota(jnp.int32, (s, LANES), 1)
        row_s = lax.broadcasted_iota(jnp.int32, (s, LANES), 0)
        kn = _pair_rms(k2_ref[0], gk2_ref[...], lane_s < HEAD_DIM)
        sel_lane = NSA_SEL_LANE0 + (row_s >> 6)
        ks_ref[pad:, :] = jnp.where(lane_s < HEAD_DIM, kn,
                                    jnp.where(lane_s == sel_lane, 1.0, 0.0)).astype(BF16)
        kw_ref[pad:, :] = jnp.where(lane_s < HEAD_DIM, pltpu.roll(kn, HEAD_DIM, 1),
                                    0.0).astype(BF16)
        lane_p = lax.broadcasted_iota(jnp.int32, (pad, LANES), 1)
        before = jnp.where(lane_p == NSA_PAD_LANE, MASKED, 0.0).astype(BF16)
        ks_ref[:pad, :] = before
        kw_ref[:pad, :] = before
        vt_ref[:, pad:] = v2_ref[0].T.astype(BF16)
        vt_ref[:, :pad] = jnp.zeros((LANES, pad), BF16)

    lane = lax.broadcasted_iota(jnp.int32, (tq, LANES), 1)
    lo = lane < HEAD_DIM
    scale = HEAD_DIM ** -0.5 * LOG2E
    kc = kc_ref[0]
    vct = vct_ref[0, :HEAD_DIM, :]
    jj = lax.broadcasted_iota(jnp.int32, (LANES, LANES), 0)
    c0 = lax.broadcasted_iota(jnp.int32, (LANES, LANES), 1) * NSA_CMP_STRIDE
    j0 = jj * NSA_SEL_LEN
    overlap = jnp.where((c0 < j0 + NSA_SEL_LEN) & (c0 + NSA_CMP_LEN > j0), 1.0, 0.0).astype(BF16)
    jj32 = lax.broadcasted_iota(jnp.int32, (n_sel, tq), 0)
    in_sel = (lane >= NSA_SEL_LANE0) & (lane < NSA_SEL_LANE0 + n_sel)
    blocks = [dict() for _ in range(n_blk)]

    def prepare(u):
        blk = blocks[u]
        i = ip * n_blk + u
        q = q_ref[0, u * tq:(u + 1) * tq, :]
        q_base = []
        for pair in range(2):
            cols = slice(pair * LANES, (pair + 1) * LANES)
            pn = _pair_rms(q[:, cols], gq_ref[:, cols], lo) * scale
            for head in (jnp.where(lo, pn, 0.0), pltpu.roll(jnp.where(lo, 0.0, pn), HEAD_DIM, 1)):
                q_base.append(jnp.where(lane == NSA_PAD_LANE, 1.0, head))
        q_plain = [x.astype(BF16) for x in q_base]
        b0 = pl.multiple_of(i * tq, tq)
        blk.update(i=i, q_base=q_base, q_plain=q_plain,
                   ks_band=ks_ref[pl.ds(b0, band), :],
                   vs_band=vt_ref[:HEAD_DIM, pl.ds(b0, band)],
                   vw_band=vt_ref[HEAD_DIM:, pl.ds(b0, band)])
        blk['s_cmp'] = [_nt_dot(kc, q_plain[h]) + cmpb_ref[h, :, u * tq:(u + 1) * tq]
                        for h in range(N_HEADS)]
        kw_band = kw_ref[pl.ds(b0, band), :]
        for h in range(N_HEADS):
            win_ref[u, h] = _nt_dot(kw_band, q_plain[h])

    def window(u, h):
        _, l_w, a_w = _first_t(win_ref[u, h] + bwin_ref[h], blocks[u]['vw_band'])
        return a_w / l_w

    def compressed(u):
        blk = blocks[u]
        o_cmp = []
        p_sum = jnp.zeros((LANES, tq), F32)
        for h in range(N_HEADS):
            s_c = blk['s_cmp'][h]
            e_c = jnp.where(s_c > 0.5 * MASKED,
                            jnp.exp2(s_c - jnp.max(s_c, axis=0, keepdims=True)), 0.0)
            den = jnp.sum(e_c, axis=0, keepdims=True)
            p_c = e_c / jnp.where(den > 0.0, den, 1.0)
            o_cmp.append(_dot(vct, p_c.astype(BF16)))
            p_sum = p_sum + p_c
        hi, mid, lo3 = _split3(p_sum)
        blk['o_cmp'] = o_cmp
        blk['imp'] = (_dot(overlap, hi) + _dot(overlap, mid) + _dot(overlap, lo3))[0:n_sel]

    def select(u):
        blk = blocks[u]
        i = blk['i']
        cur = (i * tq + lax.broadcasted_iota(jnp.int32, (n_sel, tq), 1)) >> 6
        forced = (jj32 == 0) | (jj32 == cur) | (jj32 == cur - 1)
        imp = jnp.where(forced, FORCE, blk['imp'])
        imp = jnp.where(jj32 <= cur, imp, -FORCE)
        cnt = jnp.zeros((n_sel, tq), F32)
        for jp in range(n_sel):
            other = imp[jp:jp + 1, :]
            beats = (other > imp) | ((other == imp) & (jj32 > jp))
            cnt = cnt + jnp.where(beats, 1.0, 0.0)
        sel_neg = jnp.where(cnt < float(NSA_TOP_N), 0.0, MASKED)
        sel_neg = jnp.concatenate([sel_neg, jnp.zeros((LANES - n_sel, tq), F32)], axis=0).T
        sel_neg = pltpu.roll(sel_neg, NSA_SEL_LANE0, 1)
        band_block0 = NSA_SEL_LANE0 + ((i * tq - pad) >> 6)
        blk['q_band'] = [jnp.where(in_sel, sel_neg, x).astype(BF16) for x in blk['q_base']]
        blk['q_far'] = [jnp.where(in_sel, jnp.where(lane >= band_block0, MASKED, sel_neg),
                                  x).astype(BF16) for x in blk['q_base']]

    def stage_selected(u, h):
        stage_ref[u, h % 2] = _nt_dot(blocks[u]['ks_band'], blocks[u]['q_band'][h])

    def selected_band(u):
        blk = blocks[u]
        slc = []
        for h in range(N_HEADS):
            slc.append(_first_t(stage_ref[u, h % 2] + bslc_ref[h], blk['vs_band']))
            if h + 2 < N_HEADS:
                stage_selected(u, h + 2)
        blk['slc'] = slc

    def far_tile(kt, q_far, state):
        off = pad + kt * NSA_FAR_TILE
        k_far = ks_ref[pl.ds(off, NSA_FAR_TILE), :]
        v_far = vt_ref[:HEAD_DIM, pl.ds(off, NSA_FAR_TILE)]
        sc = [_nt_dot(k_far, q_far[0]), _nt_dot(k_far, q_far[1])]
        out = []
        for h in range(N_HEADS):
            if h + 2 < N_HEADS:
                sc.append(_nt_dot(k_far, q_far[h + 2]))
            out.append(_update_t(state[h], sc[h], v_far, None))
        return out

    def finish(u, o_slc):
        blk = blocks[u]
        rows = slice(u * tq, (u + 1) * tq)
        g_t = _sigmoid(misc_ref[0, rows, :]).T
        heads = []
        for h in range(N_HEADS):
            row = lambda n: g_t[MISC_G + n * N_HEADS + h:MISC_G + n * N_HEADS + h + 1, :]
            heads.append(row(0) * blk['o_cmp'][h] + row(1) * o_slc[h][:, rows]
                         + row(2) * blk['o_win'][h])
        o_ref[0, rows, :] = jnp.concatenate(heads, axis=0).T.astype(o_ref.dtype)

    both = range(n_blk)
    for u in both:
        prepare(u)
    for u in both:
        compressed(u)
    for u in both:
        blocks[u]['o_win'] = [window(u, 0), window(u, 1)]
    for u in both:
        select(u)
    for u in both:
        stage_selected(u, 0)
        stage_selected(u, 1)
    for u in both:
        blocks[u]['o_win'] += [window(u, 2), window(u, 3)]
    for u in both:
        selected_band(u)

    state = [tuple(jnp.concatenate([blocks[u]['slc'][h][n] for u in both], axis=1)
                   for n in range(3)) for h in range(N_HEADS)]
    q_far = [jnp.concatenate([blocks[u]['q_far'][h] for u in both], axis=0)
             for h in range(N_HEADS)]

    for c in range(s // (n_blk * tq)):
        @pl.when(ip == c)
        def _(c=c):
            last = (c + 1) * n_blk - 1
            n_far = (max(last * tq - pad, 0) + NSA_FAR_TILE - 1) // NSA_FAR_TILE
            st = state
            for kt in range(n_far):
                st = far_tile(kt, q_far, st)
            o_slc = [acc / l for (_, l, acc) in st]
            for u in both:
                finish(u, o_slc)


def nsa_attention(y3, kc, vct, band_slc, band_win, cmpb, gq, gk2):
    b, s, _ = y3.shape
    n_blk = 4
    tq = n_blk * LANES
    band = NSA_WINDOW + LANES
    return pl.pallas_call(
        _nsa_kernel,
        grid=(b, s // tq),
        in_specs=[pl.BlockSpec((1, tq, MIX_W), lambda bi, i: (bi, i, COL_NSAQ // MIX_W)),
                  pl.BlockSpec((1, s, LANES), lambda bi, i: (bi, 0, COL_K2 // LANES)),
                  pl.BlockSpec((1, s, LANES), lambda bi, i: (bi, 0, COL_V2 // LANES)),
                  pl.BlockSpec((1, tq, LANES), lambda bi, i: (bi, i, COL_MISC // LANES)),
                  pl.BlockSpec((1, LANES, LANES), lambda bi, i: (bi, 0, 0)),
                  pl.BlockSpec((1, LANES, LANES), lambda bi, i: (bi, 0, 0)),
                  pl.BlockSpec(band_slc.shape, lambda bi, i: (0, 0, 0)),
                  pl.BlockSpec(band_win.shape, lambda bi, i: (0, 0, 0)),
                  pl.BlockSpec((N_HEADS, LANES, tq), lambda bi, i: (0, 0, i)),
                  pl.BlockSpec((1, MIX_W), lambda bi, i: (0, 0)),
                  pl.BlockSpec((1, LANES), lambda bi, i: (0, 0))],
        out_specs=pl.BlockSpec((1, tq, MIX_W), lambda bi, i: (bi, i, 0)),
        out_shape=jax.ShapeDtypeStruct((b, s, MIX_W), BF16),
        scratch_shapes=[pltpu.VMEM((s + NSA_WINDOW, LANES), BF16),
                        pltpu.VMEM((s + NSA_WINDOW, LANES), BF16),
                        pltpu.VMEM((LANES, s + NSA_WINDOW), BF16),
                        pltpu.VMEM((n_blk, N_HEADS, band, LANES), F32),
                        pltpu.VMEM((n_blk, 2, band, LANES), F32)],
        compiler_params=_cparams("parallel", "arbitrary"),
    )(y3, y3, y3, y3, kc, vct, band_slc, band_win, cmpb, gq, gk2)


def _merge_kernel(ysb_ref, ymla_ref, ynsa_ref, yfox_ref, h_ref, wgate_ref, wb_ref, wo_ref, x_ref,
                  o_ref):
    d = x_ref.shape[1]
    h = h_ref[...]
    u = None
    for n, y_ref in enumerate((ysb_ref, ymla_ref, ynsa_ref, yfox_ref)):
        gate = _sigmoid(_dot(h, wgate_ref[:, n * d:(n + 1) * d]))
        term = gate * _dot(y_ref[...], wb_ref[n])
        u = term if u is None else u + term
    o_ref[...] = x_ref[...] + _dot(u.astype(BF16), wo_ref[...])


def merge_branches(ys, h, wgate_all, wb_all, wo_all, layer, x, *, tm=512):
    t, d = x.shape
    yspec = pl.BlockSpec((tm, MIX_W), lambda i: (i, 0))
    return pl.pallas_call(
        _merge_kernel, grid=(t // tm,),
        in_specs=[yspec] * 4 + [
            pl.BlockSpec((tm, d), lambda i: (i, 0)),
            pl.BlockSpec((None, d, N_BRANCH * d), lambda i: (layer, 0, 0)),
            pl.BlockSpec((None,) + wb_all.shape[1:], lambda i: (layer, 0, 0, 0)),
            pl.BlockSpec((None, d, d), lambda i: (layer, 0, 0)),
            pl.BlockSpec((tm, d), lambda i: (i, 0))],
        out_specs=pl.BlockSpec((tm, d), lambda i: (i, 0)),
        out_shape=jax.ShapeDtypeStruct((t, d), F32),
        compiler_params=_cparams("parallel"),
    )(*ys, h, wgate_all, wb_all, wo_all, x)


def _mlp_ple_kernel(x_ref, g_ref, wu_ref, wd_ref, gp_ref, wg_ref, p_ref, wp_ref, o_ref,
                    h_ref, acc_ref):
    f = pl.program_id(1)

    @pl.when(f == 0)
    def _():
        h_ref[...] = _rms_rows(x_ref[...], g_ref[...]).astype(BF16)
        acc_ref[...] = jnp.zeros_like(acc_ref)

    a = jnp.maximum(_dot(h_ref[...], wu_ref[...]), 0.0)
    acc_ref[...] += _dot((a * a).astype(BF16), wd_ref[...])

    @pl.when(f == pl.num_programs(1) - 1)
    def _():
        x1 = x_ref[...] + acc_ref[...]
        gate = _sigmoid(_dot(_rms_rows(x1, gp_ref[...]).astype(BF16), wg_ref[...]))
        o_ref[...] = x1 + gate * _dot(p_ref[...].astype(BF16), wp_ref[...])


def mlp_ple(x, g, wu, wd, gp, wg, p_all, wp, layer, *, tm=1024, tf=1024):
    t, d = x.shape
    ff = wu.shape[2]
    return pl.pallas_call(
        _mlp_ple_kernel, grid=(t // tm, ff // tf),
        in_specs=[pl.BlockSpec((tm, d), lambda i, f: (i, 0)),
                  pl.BlockSpec((1, d), lambda i, f: (0, 0)),
                  pl.BlockSpec((None, d, tf), lambda i, f: (layer, 0, f)),
                  pl.BlockSpec((None, tf, d), lambda i, f: (layer, f, 0)),
                  pl.BlockSpec((1, d), lambda i, f: (0, 0)),
                  pl.BlockSpec((None, d, d), lambda i, f: (layer, 0, 0)),
                  pl.BlockSpec((None, tm, PLE_DIM), lambda i, f: (layer, i, 0)),
                  pl.BlockSpec((None, PLE_DIM, d), lambda i, f: (layer, 0, 0))],
        out_specs=pl.BlockSpec((tm, d), lambda i, f: (i, 0)),
        out_shape=jax.ShapeDtypeStruct((t, d), F32),
        scratch_shapes=[pltpu.VMEM((tm, d), BF16), pltpu.VMEM((tm, d), F32)],
        compiler_params=_cparams("parallel", "arbitrary"),
    )(x, g.reshape(1, d), wu, wd, gp.reshape(1, d), wg, p_all, wp)


def _pack_w_in(w):
    offs = np.concatenate([[0], np.cumsum(IN_WIDTHS)]).tolist()
    (sb_q, sb_k, sb_v, cq, ckv, kr, nsa_q, kc, vc, ks, vs, kw, vw, ng,
     fox_q, fox_k, fox_v, ff, gate) = [w[..., offs[n]:offs[n + 1]] for n in range(len(IN_WIDTHS))]
    z = lambda n: jnp.zeros(w.shape[:-1] + (n,), w.dtype)
    misc = jnp.concatenate([ff, ng, z(MISC_KR - MISC_G - 3 * N_HEADS), kr,
                            z(LANES - MISC_KR - MLA_ROPE)], axis=-1)
    w_f32_part = jnp.concatenate([cq, ckv, ks, kw, vs, vw, kc, vc, misc, nsa_q], axis=-1)
    w_bf16_part = jnp.concatenate([sb_q, sb_k, sb_v, fox_q, fox_k, fox_v], axis=-1)
    return w_f32_part.astype(BF16), w_bf16_part.astype(BF16), gate.astype(BF16)


def _head_slots(w, width):
    k = w.shape[0]
    w = w.reshape(k, N_HEADS, width)
    return jnp.pad(w, ((0, 0), (0, 0), (0, LANES - width))).reshape(k, N_HEADS * LANES)


def _rope_tables(s):
    half = MLA_ROPE // 2
    inv = jnp.exp(-math.log(ROPE_THETA) * jnp.arange(half, dtype=F32) / half)
    ang = jnp.arange(s, dtype=F32)[:, None] * inv[None, :]
    cos, sin = jnp.cos(ang), jnp.sin(ang)
    ones = jnp.ones((s, MLA_NOPE), F32)
    zeros = lambda n: jnp.zeros((s, n), F32)
    tail = LANES - MLA_QK
    cos_t = jnp.concatenate([ones, cos, cos, jnp.ones((s, tail), F32)], axis=1)
    sa_t = jnp.concatenate([zeros(MLA_NOPE), -sin, zeros(half), zeros(tail)], axis=1)
    sb_t = jnp.concatenate([zeros(MLA_NOPE), zeros(half), sin, zeros(tail)], axis=1)
    return cos_t, sa_t, sb_t


def _pad_lanes(v, left=0):
    v = v.reshape(1, -1)
    return jnp.pad(v, ((0, 0), (left, LANES - left - v.shape[1])))


def _pad_to_lanes(w):
    return jnp.pad(w, ((0, 0), (0, LANES - w.shape[1])))


def kernel(x, p, rel_bias, norm_mix_g, w_in, mla_cq_norm_g, mla_ckv_norm_g, mla_w_uq, mla_w_ukv,
           mla_qn_g, mla_kn_g, nsa_pe_k, nsa_pe_v, nsa_w1_k, nsa_w2_k, nsa_w1_v, nsa_w2_v,
           nsa_qn_g, nsa_kn_g, fox_f_bias, fox_qn_g, fox_kn_g, w_branch, w_o, norm_mlp_g,
           w_mlp_up, w_mlp_down, norm_ple_g, w_ple_gate, w_ple_proj):
    b, s, d = x.shape
    t = b * s
    xf = x.reshape(t, d)
    p_all = p.reshape(DEPTH, t, PLE_DIM)
    band_slc, band_win, cmpb = rel_bias_tables(rel_bias.astype(F32), s)
    cos_t, sa_t, sb_t = _rope_tables(s)
    n_cmp_in = NSA_CMP_STRIDE * HEAD_DIM
    w_a, w_b, w_gate = _pack_w_in(w_in)
    wb_all, wo_all = w_branch.astype(BF16), w_o.astype(BF16)
    wu_all, wd_all = w_mlp_up.astype(BF16), w_mlp_down.astype(BF16)
    wg_all, wp_all = w_ple_gate.astype(BF16), w_ple_proj.astype(BF16)

    for i in range(DEPTH):
        ya, h = norm_proj(xf, norm_mix_g[i], w_a, i)
        yb = matmul_bf16(h, w_b, i)
        y3 = ya.reshape(b, s, N_F32)
        yb3 = yb.reshape(b, s, N_BF16)

        y_sb = sb_attention(yb3)

        wukv = mla_w_ukv[i].reshape(MLA_KV_LORA, N_HEADS, MLA_NOPE + MLA_V)
        q_m, k_m, vt_m = mla_prep(
            y3, mla_cq_norm_g[i].reshape(1, -1), mla_ckv_norm_g[i].reshape(1, -1),
            _head_slots(mla_w_uq[i], MLA_QK).astype(BF16),
            _head_slots(wukv[:, :, :MLA_NOPE].reshape(MLA_KV_LORA, -1), MLA_NOPE).astype(BF16),
            wukv[:, :, MLA_NOPE:].reshape(MLA_KV_LORA, -1).T.astype(BF16),
            _pad_lanes(mla_qn_g[i]), _pad_lanes(mla_kn_g[i]), cos_t, sa_t, sb_t)
        y_mla = mla_attention(q_m, k_m, vt_m)

        pad_sq = lambda w: jnp.pad(_pad_to_lanes(w), ((0, LANES - w.shape[0]), (0, 0)))
        pe_kv = jnp.concatenate([nsa_pe_k[i], nsa_pe_v[i]], axis=1)
        kc, vct = nsa_compress(
            y3, pe_kv[:NSA_CMP_STRIDE], pe_kv[NSA_CMP_STRIDE:],
            *_compress_weights(nsa_w1_k[i], 0), *_compress_weights(nsa_w1_v[i], HEAD_DIM),
            pad_sq(nsa_w2_k[i]).astype(BF16), pad_sq(nsa_w2_v[i].T).astype(BF16),
            _pad_lanes(nsa_kn_g[i, 0]))
        y_nsa = nsa_attention(
            y3, kc, vct, band_slc, band_win, cmpb,
            jnp.tile(nsa_qn_g[i], N_HEADS).reshape(1, -1),
            jnp.concatenate([nsa_kn_g[i, 1], nsa_kn_g[i, 2]]).reshape(1, -1))

        cum = fox_cum(y3, _pad_lanes(fox_f_bias[i], MISC_F))
        y_fox = fox_attention(yb3, cum, jnp.tile(fox_qn_g[i], N_HEADS).reshape(1, -1),
                              jnp.tile(fox_kn_g[i], N_HEADS).reshape(1, -1))

        ys = [a.reshape(t, MIX_W) for a in (y_sb, y_mla, y_nsa, y_fox)]
        xf = merge_branches(ys, h, w_gate, wb_all, wo_all, i, xf)
        xf = mlp_ple(xf, norm_mlp_g[i], wu_all, wd_all, norm_ple_g[i], wg_all, p_all, wp_all, i)
    return xf.reshape(b, s, d)
```

```python
import functools
import math

import numpy as np
import jax
import jax.numpy as jnp
from jax import lax
from jax.experimental import pallas as pl
from jax.experimental.pallas import tpu as pltpu

F32 = jnp.float32
BF16 = jnp.bfloat16

D_MODEL = 1024
DEPTH = 4
HEAD_DIM = 64
N_HEADS = 4
MIX_W = N_HEADS * HEAD_DIM
N_BRANCH = 4
EPS = 1e-6
FORCE = 1e9
MLA_Q_LORA = 384
MLA_KV_LORA = 128
MLA_NOPE = 64
MLA_ROPE = 32
MLA_V = 64
MLA_QK = MLA_NOPE + MLA_ROPE
ROPE_THETA = 10000.0
NSA_CMP_LEN = 32
NSA_CMP_STRIDE = 16
NSA_SEL_LEN = 64
NSA_TOP_N = 16
NSA_WINDOW = 512
REL_BUCKETS = 32
REL_MAX_DIST = 128
D_FF = 4 * D_MODEL
PLE_DIM = 256

LANES = 128
MASKED = -1e30
VMEM_LIMIT = 56 * 1024 * 1024

IN_WIDTHS = ((MIX_W,) * 3
             + (MLA_Q_LORA, MLA_KV_LORA, MLA_ROPE)
             + (MIX_W,) + (HEAD_DIM,) * 6 + (3 * N_HEADS,)
             + (MIX_W,) * 3 + (N_HEADS,)
             + (N_BRANCH * D_MODEL,))

COL_CQ = 0
COL_CKV = 384
COL_K2 = 512
COL_V2 = 640
COL_KCVC = 768
COL_MISC = 896
COL_NSAQ = 1024
N_F32 = 1280
COL_SB = 0
COL_FOX = 768
N_BF16 = 1536
MISC_F = 0
MISC_G = 4
MISC_KR = 32

NSA_PAD_LANE = HEAD_DIM
NSA_SEL_LANE0 = HEAD_DIM + 1
NSA_FAR_TILE = 512


def _cparams(*sem):
    return pltpu.CompilerParams(dimension_semantics=sem, vmem_limit_bytes=VMEM_LIMIT)


def _nt_dot(a, b):
    return lax.dot_general(a, b, (((1,), (1,)), ((), ())), preferred_element_type=F32)


def _dot(a, b):
    return jnp.dot(a, b, preferred_element_type=F32)


def _split3(x):
    hi = x.astype(BF16)
    r1 = x - hi.astype(F32)
    mid = r1.astype(BF16)
    lo = (r1 - mid.astype(F32)).astype(BF16)
    return hi, mid, lo


def _softplus(z):
    return jnp.maximum(z, 0.0) + jnp.log(1.0 + jnp.exp(-jnp.abs(z)))


def _sigmoid(z):
    return 1.0 / (1.0 + jnp.exp(-z))


def _rms_rows(x, g):
    r = lax.rsqrt(jnp.mean(x * x, axis=-1, keepdims=True) + EPS)
    return x * r * g


def _pair_rms(x, g, lo):
    x2 = x * x
    s0 = jnp.sum(jnp.where(lo, x2, 0.0), axis=-1, keepdims=True)
    s1 = jnp.sum(jnp.where(lo, 0.0, x2), axis=-1, keepdims=True)
    r = jnp.where(lo, lax.rsqrt(s0 / HEAD_DIM + EPS), lax.rsqrt(s1 / HEAD_DIM + EPS))
    return x * r * g


LOG2E = math.log2(math.e)


def _first_t(s, vt):
    m = jnp.max(s, axis=0, keepdims=True)
    p = jnp.exp2(s - m)
    return m, jnp.sum(p, axis=0, keepdims=True), _dot(vt, p.astype(BF16))


def _update_t(carry, s, vt, mask):
    m, l, acc = carry
    if mask is not None:
        s = jnp.where(mask, s, MASKED)
    m_new = jnp.maximum(m, jnp.max(s, axis=0, keepdims=True))
    p = jnp.exp2(s - m_new)
    if mask is not None:
        p = jnp.where(mask, p, 0.0)
    alpha = jnp.exp2(m - m_new)
    l = alpha * l + jnp.sum(p, axis=0, keepdims=True)
    acc = alpha * acc + _dot(vt, p.astype(BF16))
    return m_new, l, acc


def _init_t(tq):
    return (jnp.full((1, tq), MASKED, F32), jnp.zeros((1, tq), F32),
            jnp.zeros((HEAD_DIM, tq), F32))


def _causal_flash_static(c, tq, tk, score_fn, vt_fn):
    assert tk % tq == 0
    heads = range(N_HEADS)
    diff = (lax.broadcasted_iota(jnp.int32, (tk, tq), 0)
            - lax.broadcasted_iota(jnp.int32, (tk, tq), 1))
    n_full = (c * tq) // tk
    n_part = 2
    tp = tq // n_part
    parts = [slice(u * tp, (u + 1) * tp) for u in range(n_part)]
    cur = [score_fn(h, 0) for h in heads]
    state = [[_init_t(tp) for _ in parts] for _ in heads]
    for kt in range(n_full + 1):
        nxt = [score_fn(h, (kt + 1) * tk) for h in heads] if kt < n_full else None
        mask = (diff <= c * tq - kt * tk) if kt == n_full else None
        for h in heads:
            vt = vt_fn(h, kt * tk)
            state[h] = [_update_t(state[h][u], cur[h][:, sl], vt,
                                  None if mask is None else mask[:, sl])
                        for u, sl in enumerate(parts)]
        cur = nxt
    return jnp.concatenate(
        [jnp.concatenate([acc / l for (_, l, acc) in state[h]], axis=1) for h in heads], axis=0)


def _augment(x, terms, col, lane, h, key_side):
    live = (lane < HEAD_DIM) if h == 0 else (lane >= HEAD_DIM)
    a0 = HEAD_DIM if h == 0 else 0
    c0, o0 = (a0, a0 + 3) if key_side else (a0 + 3, a0)
    src = lax.broadcasted_iota(jnp.int32, (3 * LANES, LANES), 0)
    dst = lax.broadcasted_iota(jnp.int32, (3 * LANES, LANES), 1)
    place = jnp.where(((src & (LANES - 1)) == col) & (dst == c0 + (src >> 7)),
                      -1.0 if key_side else 1.0, 0.0).astype(BF16)
    ones = jnp.where((lane >= o0) & (lane < o0 + 3), 1.0, 0.0)
    return jnp.where(live, x, _dot(terms, place) + ones)


def _norm_proj_kernel(x_ref, g_ref, w_ref, o_ref, h_ref):
    h = _rms_rows(x_ref[...], g_ref[...]).astype(BF16)
    h_ref[...] = h
    o_ref[...] = _dot(h, w_ref[...])


def norm_proj(x, g, w_all, layer, *, tm=1024):
    t, d = x.shape
    n = w_all.shape[2]
    return pl.pallas_call(
        _norm_proj_kernel,
        grid=(t // tm,),
        in_specs=[pl.BlockSpec((tm, d), lambda i: (i, 0)),
                  pl.BlockSpec((1, d), lambda i: (0, 0)),
                  pl.BlockSpec((None, d, n), lambda i: (layer, 0, 0))],
        out_specs=[pl.BlockSpec((tm, n), lambda i: (i, 0)),
                   pl.BlockSpec((tm, d), lambda i: (i, 0))],
        out_shape=[jax.ShapeDtypeStruct((t, n), F32), jax.ShapeDtypeStruct((t, d), BF16)],
        compiler_params=_cparams("parallel"),
    )(x, g.reshape(1, d), w_all)


def _matmul_kernel(h_ref, w_ref, o_ref):
    o_ref[...] = _dot(h_ref[...], w_ref[...]).astype(o_ref.dtype)


def matmul_bf16(h, w_all, layer, *, tm=2048, tn=512):
    t, d = h.shape
    n = w_all.shape[2]
    return pl.pallas_call(
        _matmul_kernel,
        grid=(t // tm, n // tn),
        in_specs=[pl.BlockSpec((tm, d), lambda i, j: (i, 0)),
                  pl.BlockSpec((None, d, tn), lambda i, j: (layer, 0, j))],
        out_specs=pl.BlockSpec((tm, tn), lambda i, j: (i, j)),
        out_shape=jax.ShapeDtypeStruct((t, n), BF16),
        compiler_params=_cparams("parallel", "parallel"),
    )(h, w_all)


def _sb_kernel(q_ref, k_ref, v_ref, o_ref, vt_ref, *, tq, tk):
    i = pl.program_id(1)

    @pl.when(i == 0)
    def _():
        vt_ref[...] = v_ref[0].astype(F32).T.astype(BF16)

    sub = LANES
    n_sub = tk // sub
    heads = range(N_HEADS)
    lane = lax.broadcasted_iota(jnp.int32, (tq, LANES), 1)
    lo = lane < HEAD_DIM
    qh = []
    for p in range(N_HEADS // 2):
        q = q_ref[0, :, p * LANES:(p + 1) * LANES].astype(F32) * (HEAD_DIM ** -0.5)
        qh += [jnp.where(lo, q, 0.0).astype(BF16), jnp.where(lo, 0.0, q).astype(BF16)]
    rr = lax.broadcasted_iota(jnp.int32, (sub, sub), 0)
    cc = lax.broadcasted_iota(jnp.int32, (sub, sub), 1)
    upper = jnp.where(cc >= rr, 1.0, 0.0).astype(BF16)
    upper2 = jnp.concatenate([upper, upper], axis=1)
    diff = (lax.broadcasted_iota(jnp.int32, (tk, tq), 0)
            - lax.broadcasted_iota(jnp.int32, (tk, tq), 1))

    def logits_at(kt):
        return tuple(_nt_dot(k_ref[0, pl.ds(kt * tk, tk), (h // 2) * LANES:(h // 2 + 1) * LANES],
                             qh[h]) for h in heads)

    def tile(c, kt, zs, carry, masked):
        off = kt * tk
        past = (diff < c * tq - off) if masked else None
        out = []
        for h in heads:
            run, acc = carry[h]
            z = zs[h]
            sp = _softplus(z)
            spm = jnp.where(past, sp, 0.0) if masked else sp
            ws = [None] * n_sub
            for j in reversed(range(n_sub)):
                sl = slice(j * sub, (j + 1) * sub)
                spj = spm[sl]
                hi = spj.astype(BF16)
                mid = (spj - hi.astype(F32)).astype(BF16)
                tail = _dot(upper2, jnp.concatenate([hi, mid], axis=0))
                w = jnp.exp(z[sl] - tail - run)
                if masked:
                    w = jnp.where(past[sl], w, 0.0)
                ws[j] = w.astype(BF16)
                run = run + tail[0:1, :]
            vt = vt_ref[h * HEAD_DIM:(h + 1) * HEAD_DIM, pl.ds(off, tk)]
            out.append((run, acc + _dot(vt, jnp.concatenate(ws, axis=0))))
        return tuple(out)

    zero = (jnp.zeros((1, tq), F32), jnp.zeros((HEAD_DIM, tq), F32))
    for c in range(k_ref.shape[1] // tq):
        @pl.when(i == c)
        def _(c=c):
            n_full = (c * tq) // tk
            zs = logits_at(n_full)
            carry = (zero,) * N_HEADS
            for kt in range(n_full, -1, -1):
                zs_next = logits_at(kt - 1) if kt > 0 else None
                carry = tile(c, kt, zs, carry, kt == n_full)
                zs = zs_next
            o_t = jnp.concatenate([acc for _, acc in carry], axis=0)
            o_ref[0] = o_t.T.astype(o_ref.dtype)


def sb_attention(y3, *, tq=512, tk=512):
    b, s, _ = y3.shape
    qb = COL_SB // MIX_W
    return pl.pallas_call(
        functools.partial(_sb_kernel, tq=tq, tk=tk),
        grid=(b, s // tq),
        in_specs=[pl.BlockSpec((1, tq, MIX_W), lambda bi, i: (bi, i, qb)),
                  pl.BlockSpec((1, s, MIX_W), lambda bi, i: (bi, 0, qb + 1)),
                  pl.BlockSpec((1, s, MIX_W), lambda bi, i: (bi, 0, qb + 2))],
        out_specs=pl.BlockSpec((1, tq, MIX_W), lambda bi, i: (bi, i, 0)),
        out_shape=jax.ShapeDtypeStruct((b, s, MIX_W), BF16),
        scratch_shapes=[pltpu.VMEM((MIX_W, s), BF16)],
        compiler_params=_cparams("parallel", "arbitrary"),
    )(y3, y3, y3)


def _fox_cum_kernel(misc_ref, fb_ref, cum_ref, *, s):
    rr = lax.broadcasted_iota(jnp.int32, (LANES, LANES), 0)
    cc = lax.broadcasted_iota(jnp.int32, (LANES, LANES), 1)
    lower = jnp.where(cc <= rr, 1.0, 0.0).astype(BF16)

    def body(n, carry):
        off = pl.multiple_of(n * LANES, LANES)
        x = misc_ref[0, pl.ds(off, LANES), :] + fb_ref[...]
        log_f = jnp.minimum(x, 0.0) - jnp.log(1.0 + jnp.exp(-jnp.abs(x)))
        hi, mid, lo3 = _split3(log_f)
        c = _dot(lower, hi) + _dot(lower, mid) + _dot(lower, lo3) + carry
        cum_ref[0, pl.ds(off, LANES), :] = c
        return c[LANES - 1:LANES, :]

    lax.fori_loop(0, s // LANES, body, jnp.zeros((1, LANES), F32))


def fox_cum(y3, fbias_row):
    b, s, _ = y3.shape
    return pl.pallas_call(
        functools.partial(_fox_cum_kernel, s=s),
        grid=(b,),
        in_specs=[pl.BlockSpec((1, s, LANES), lambda bi: (bi, 0, COL_MISC // LANES)),
                  pl.BlockSpec((1, LANES), lambda bi: (0, 0))],
        out_specs=pl.BlockSpec((1, s, LANES), lambda bi: (bi, 0, 0)),
        out_shape=jax.ShapeDtypeStruct((b, s, LANES), F32),
        compiler_params=_cparams("parallel"),
    )(y3, fbias_row)


def _fox_kernel(q_ref, k_ref, v_ref, cum_ref, gq_ref, gk_ref, o_ref, ka_ref, vt_ref, *, tq, tk):
    i = pl.program_id(1)
    s = k_ref.shape[1]
    pairs = range(N_HEADS // 2)

    @pl.when(i == 0)
    def _():
        lane_s = lax.broadcasted_iota(jnp.int32, (s, LANES), 1)
        terms = jnp.concatenate(_split3(cum_ref[0] * LOG2E), axis=1)
        for p in pairs:
            cols = slice(p * LANES, (p + 1) * LANES)
            kn = _pair_rms(k_ref[0, :, cols].astype(F32), gk_ref[:, cols], lane_s < HEAD_DIM)
            for h in range(2):
                ka_ref[:, (2 * p + h) * LANES:(2 * p + h + 1) * LANES] = _augment(
                    kn, terms, MISC_F + 2 * p + h, lane_s, h, True).astype(BF16)
        vt_ref[...] = v_ref[0].astype(F32).T.astype(BF16)

    lane = lax.broadcasted_iota(jnp.int32, (tq, LANES), 1)
    terms_q = jnp.concatenate(
        _split3(cum_ref[0, pl.ds(pl.multiple_of(i * tq, tq), tq), :] * LOG2E), axis=1)
    qa = []
    for p in pairs:
        cols = slice(p * LANES, (p + 1) * LANES)
        qn = _pair_rms(q_ref[0, :, cols].astype(F32), gq_ref[:, cols], lane < HEAD_DIM)
        qn = qn * (HEAD_DIM ** -0.5 * LOG2E)
        qa += [_augment(qn, terms_q, MISC_F + 2 * p + h, lane, h, False).astype(BF16)
               for h in range(2)]

    def scores(h, off):
        return _nt_dot(ka_ref[pl.ds(off, tk), h * LANES:(h + 1) * LANES], qa[h])

    def values_t(h, off):
        return vt_ref[h * HEAD_DIM:(h + 1) * HEAD_DIM, pl.ds(off, tk)]

    for c in range(s // tq):
        @pl.when(i == c)
        def _(c=c):
            o_ref[0] = _causal_flash_static(c, tq, tk, scores, values_t).T.astype(o_ref.dtype)


def fox_attention(y3, cum, gq, gk, *, tq=512, tk=512):
    b, s, _ = y3.shape
    qb = COL_FOX // MIX_W
    return pl.pallas_call(
        functools.partial(_fox_kernel, tq=tq, tk=tk),
        grid=(b, s // tq),
        in_specs=[pl.BlockSpec((1, tq, MIX_W), lambda bi, i: (bi, i, qb)),
                  pl.BlockSpec((1, s, MIX_W), lambda bi, i: (bi, 0, qb + 1)),
                  pl.BlockSpec((1, s, MIX_W), lambda bi, i: (bi, 0, qb + 2)),
                  pl.BlockSpec((1, s, LANES), lambda bi, i: (bi, 0, 0)),
                  pl.BlockSpec((1, MIX_W), lambda bi, i: (0, 0)),
                  pl.BlockSpec((1, MIX_W), lambda bi, i: (0, 0))],
        out_specs=pl.BlockSpec((1, tq, MIX_W), lambda bi, i: (bi, i, 0)),
        out_shape=jax.ShapeDtypeStruct((b, s, MIX_W), BF16),
        scratch_shapes=[pltpu.VMEM((s, N_HEADS * LANES), BF16), pltpu.VMEM((MIX_W, s), BF16)],
        compiler_params=_cparams("parallel", "arbitrary"),
    )(y3, y3, y3, cum, gq, gk)


def _mla_prep_kernel(cq_ref, ckv_ref, misc_ref, gcq_ref, gckv_ref, wuq_ref, wuk_ref, wuv_ref,
                     qg_ref, kg_ref, cos_ref, sa_ref, sb_ref, q_out, k_out, v_out, *, ts):
    hq = _rms_rows(cq_ref[0], gcq_ref[...]).astype(BF16)
    hkv = _rms_rows(ckv_ref[0], gckv_ref[...]).astype(BF16)
    q = _dot(hq, wuq_ref[...])
    kn = _dot(hkv, wuk_ref[...])
    v_out[0] = _nt_dot(wuv_ref[...], hkv).astype(v_out.dtype)

    lane = lax.broadcasted_iota(jnp.int32, (ts, LANES), 1)
    misc = misc_ref[0]
    kr = pltpu.roll(jnp.where((lane >= MISC_KR) & (lane < MISC_KR + MLA_ROPE), misc, 0.0),
                    MLA_NOPE - MISC_KR, 1)
    cos, sin = cos_ref[...], sa_ref[...] + sb_ref[...]
    half = MLA_ROPE // 2

    kk = lax.broadcasted_iota(jnp.int32, (2 * LANES, LANES), 0) & (LANES - 1)
    ll = lax.broadcasted_iota(jnp.int32, (2 * LANES, LANES), 1)
    ones2 = jnp.ones((2 * LANES, LANES), BF16)
    first = (ll >= MLA_NOPE) & (ll < MLA_NOPE + half)
    second = (ll >= MLA_NOPE + half) & (ll < MLA_QK)
    swap2 = jnp.where((first & (kk == ll + half)) | (second & (kk == ll - half)),
                      1.0, 0.0).astype(BF16)

    def split2(x):
        hi = x.astype(BF16)
        return jnp.concatenate([hi, (x - hi.astype(F32)).astype(BF16)], axis=1)

    def norm_rope(t, g):
        t = t * lax.rsqrt(_dot(split2(t * t), ones2) / MLA_QK + EPS) * g
        return t * cos + _dot(split2(t), swap2) * sin

    for h in range(N_HEADS):
        sl = slice(h * LANES, (h + 1) * LANES)
        q_out[0, :, sl] = (norm_rope(q[:, sl], qg_ref[...])
                           * (MLA_QK ** -0.5 * LOG2E)).astype(q_out.dtype)
        k_out[0, :, sl] = norm_rope(kn[:, sl] + kr, kg_ref[...]).astype(k_out.dtype)


def mla_prep(y3, gcq, gckv, wuq, wuk, wuv, qg, kg, cos, sa, sb, *, ts=512):
    b, s, _ = y3.shape
    const = lambda shape: pl.BlockSpec(shape, lambda bi, i: (0,) * len(shape))
    return pl.pallas_call(
        functools.partial(_mla_prep_kernel, ts=ts),
        grid=(b, s // ts),
        in_specs=[pl.BlockSpec((1, ts, MLA_Q_LORA), lambda bi, i: (bi, i, COL_CQ // MLA_Q_LORA)),
                  pl.BlockSpec((1, ts, LANES), lambda bi, i: (bi, i, COL_CKV // LANES)),
                  pl.BlockSpec((1, ts, LANES), lambda bi, i: (bi, i, COL_MISC // LANES)),
                  const((1, MLA_Q_LORA)), const((1, MLA_KV_LORA)),
                  const((MLA_Q_LORA, N_HEADS * LANES)), const((MLA_KV_LORA, N_HEADS * LANES)),
                  const((MIX_W, MLA_KV_LORA)), const((1, LANES)), const((1, LANES)),
                  pl.BlockSpec((ts, LANES), lambda bi, i: (i, 0)),
                  pl.BlockSpec((ts, LANES), lambda bi, i: (i, 0)),
                  pl.BlockSpec((ts, LANES), lambda bi, i: (i, 0))],
        out_specs=[pl.BlockSpec((1, ts, N_HEADS * LANES), lambda bi, i: (bi, i, 0)),
                   pl.BlockSpec((1, ts, N_HEADS * LANES), lambda bi, i: (bi, i, 0)),
                   pl.BlockSpec((1, MIX_W, ts), lambda bi, i: (bi, 0, i))],
        out_shape=[jax.ShapeDtypeStruct((b, s, N_HEADS * LANES), BF16),
                   jax.ShapeDtypeStruct((b, s, N_HEADS * LANES), BF16),
                   jax.ShapeDtypeStruct((b, MIX_W, s), BF16)],
        compiler_params=_cparams("parallel", "parallel"),
    )(y3, y3, y3, gcq, gckv, wuq, wuk, wuv, qg, kg, cos, sa, sb)


def _mla_kernel(q_ref, k_ref, vt_ref, o_ref, *, tq, tk):
    i = pl.program_id(1)
    qh = [q_ref[0, :, h * LANES:(h + 1) * LANES] for h in range(N_HEADS)]

    def scores(h, off):
        return _nt_dot(k_ref[0, pl.ds(off, tk), h * LANES:(h + 1) * LANES], qh[h])

    def values_t(h, off):
        return vt_ref[0, h * HEAD_DIM:(h + 1) * HEAD_DIM, pl.ds(off, tk)]

    for c in range(k_ref.shape[1] // tq):
        @pl.when(i == c)
        def _(c=c):
            o_ref[0] = _causal_flash_static(c, tq, tk, scores, values_t).T.astype(o_ref.dtype)


def mla_attention(q, k, vt, *, tq=512, tk=512):
    b, s, _ = q.shape
    return pl.pallas_call(
        functools.partial(_mla_kernel, tq=tq, tk=tk),
        grid=(b, s // tq),
        in_specs=[pl.BlockSpec((1, tq, N_HEADS * LANES), lambda bi, i: (bi, i, 0)),
                  pl.BlockSpec((1, s, N_HEADS * LANES), lambda bi, i: (bi, 0, 0)),
                  pl.BlockSpec((1, MIX_W, s), lambda bi, i: (bi, 0, 0))],
        out_specs=pl.BlockSpec((1, tq, MIX_W), lambda bi, i: (bi, i, 0)),
        out_shape=jax.ShapeDtypeStruct((b, s, MIX_W), BF16),
        compiler_params=_cparams("parallel", "arbitrary"),
    )(q, k, vt)


def _rel_bias_tile(dist, tab_ref, h):
    max_exact = REL_BUCKETS // 2
    d = jnp.maximum(dist, 0)
    large = max_exact + (jnp.log(jnp.maximum(d, 1).astype(F32) / max_exact)
                         / math.log(REL_MAX_DIST / max_exact)
                         * (REL_BUCKETS - max_exact)).astype(jnp.int32)
    large = jnp.minimum(large, REL_BUCKETS - 1)
    bucket = jnp.where(d < max_exact, d, large)
    out = jnp.zeros(dist.shape, F32)
    for bkt in range(REL_BUCKETS):
        out = jnp.where(bucket == bkt, tab_ref[bkt, h], out)
    return out


def _band_bias_kernel(tab_ref, slc_ref, win_ref):
    r = pl.program_id(0)
    kj = r * LANES + lax.broadcasted_iota(jnp.int32, (LANES, LANES), 0)
    qi = lax.broadcasted_iota(jnp.int32, (LANES, LANES), 1)
    dist = qi + NSA_WINDOW - kj
    for h in range(N_HEADS):
        delta = (_rel_bias_tile(dist, tab_ref, h) - tab_ref[REL_BUCKETS - 1, h]) * LOG2E
        slc_ref[h] = jnp.where(dist >= 0, delta, MASKED)
        win_ref[h] = jnp.where((dist >= 0) & (dist < NSA_WINDOW), delta, MASKED)


def _cmp_bias_kernel(tab_ref, o_ref):
    i = pl.program_id(0)
    c = lax.broadcasted_iota(jnp.int32, (LANES, LANES), 0)
    s = i * LANES + lax.broadcasted_iota(jnp.int32, (LANES, LANES), 1)
    dist = s - (c * NSA_CMP_STRIDE + NSA_CMP_LEN - 1)
    for h in range(N_HEADS):
        o_ref[h] = jnp.where(dist >= 0, _rel_bias_tile(dist, tab_ref, h) * LOG2E, MASKED)


def rel_bias_tables(rel_bias, s):
    assert REL_BUCKETS == 32 and REL_MAX_DIST == 128 and NSA_WINDOW >= 113
    band = NSA_WINDOW + LANES
    smem = pl.BlockSpec(memory_space=pltpu.SMEM)
    band_spec = pl.BlockSpec((N_HEADS, LANES, LANES), lambda r: (0, r, 0))
    band_shape = jax.ShapeDtypeStruct((N_HEADS, band, LANES), F32)
    band_slc, band_win = pl.pallas_call(
        _band_bias_kernel, grid=(band // LANES,), in_specs=[smem],
        out_specs=[band_spec, band_spec], out_shape=[band_shape, band_shape],
    )(rel_bias)
    cmpb = pl.pallas_call(
        _cmp_bias_kernel, grid=(s // LANES,), in_specs=[smem],
        out_specs=pl.BlockSpec((N_HEADS, LANES, LANES), lambda i: (0, 0, i)),
        out_shape=jax.ShapeDtypeStruct((N_HEADS, LANES, s), F32),
    )(rel_bias)
    return band_slc, band_win, cmpb


def _nsa_compress_kernel(kv_ref, pea_ref, peb_ref, w1ka_ref, w1kb_ref, w1va_ref, w1vb_ref,
                         w2k_ref, w2v_ref, gk_ref, kc_ref, vct_ref):
    n_blk = kv_ref.shape[1] // NSA_CMP_STRIDE
    views = [kv_ref[0, pl.ds(l, n_blk, stride=NSA_CMP_STRIDE), :] for l in range(NSA_CMP_STRIDE)]
    first = jnp.concatenate([(views[l] + pea_ref[l:l + 1, :]).astype(BF16)
                             for l in range(NSA_CMP_STRIDE)], axis=1)
    second = jnp.concatenate([(views[l] + peb_ref[l:l + 1, :]).astype(BF16)
                              for l in range(NSA_CMP_STRIDE)], axis=1)

    def hidden(wa_ref, wb_ref):
        pre = _dot(first, wa_ref[...]) + pltpu.roll(_dot(second, wb_ref[...]), n_blk - 1, 0)
        return (pre * _sigmoid(pre)).astype(BF16)

    kc = _dot(hidden(w1ka_ref, w1kb_ref), w2k_ref[...])
    kc = kc * lax.rsqrt(jnp.sum(kc * kc, axis=-1, keepdims=True) / HEAD_DIM + EPS) * gk_ref[...]
    kc_ref[0] = kc.astype(kc_ref.dtype)
    vct_ref[0] = _nt_dot(w2v_ref[...], hidden(w1va_ref, w1vb_ref)).astype(vct_ref.dtype)


def nsa_compress(y3, pea, peb, w1ka, w1kb, w1va, w1vb, w2k, w2v, gk):
    b, s, _ = y3.shape
    assert s // NSA_CMP_STRIDE == LANES
    const = lambda a: pl.BlockSpec(a.shape, lambda bi: (0,) * a.ndim)
    out = pl.BlockSpec((1, LANES, LANES), lambda bi: (bi, 0, 0))
    consts = (pea, peb, w1ka, w1kb, w1va, w1vb, w2k, w2v, gk)
    return pl.pallas_call(
        _nsa_compress_kernel, grid=(b,),
        in_specs=[pl.BlockSpec((1, s, LANES), lambda bi: (bi, 0, COL_KCVC // LANES))]
        + [const(a) for a in consts],
        out_specs=[out, out],
        out_shape=[jax.ShapeDtypeStruct((b, LANES, LANES), BF16)] * 2,
        compiler_params=_cparams("parallel"),
    )(y3, *consts)


def _compress_weights(w1, lane0):
    w = w1.reshape(NSA_CMP_LEN, HEAD_DIM, HEAD_DIM)
    w = jnp.pad(w, ((0, 0), (lane0, LANES - HEAD_DIM - lane0), (0, LANES - HEAD_DIM)))
    w = w.reshape(2, NSA_CMP_STRIDE * LANES, LANES).astype(BF16)
    return w[0], w[1]


def _nsa_kernel(q_ref, k2_ref, v2_ref, misc_ref, kc_ref, vct_ref, bslc_ref, bwin_ref, cmpb_ref,
                gq_ref, gk2_ref, o_ref, ks_ref, kw_ref, vt_ref, win_ref, stage_ref):
    tq = LANES
    n_blk = q_ref.shape[1] // tq
    s = k2_ref.shape[1]
    pad = NSA_WINDOW
    band = NSA_WINDOW + tq
    n_sel = s // NSA_SEL_LEN
    assert NSA_SEL_LANE0 + n_sel <= LANES and pad % NSA_FAR_TILE == 0
    ip = pl.program_id(1)

    @pl.when(ip == 0)
    def _():
        lane_s = lax.broadcasted_iota(jnp.int32, (s, LANES), 1)
        row_s = lax.broadcasted_iota(jnp.int32, (s, LANES), 0)
        kn = _pair_rms(k2_ref[0], gk2_ref[...], lane_s < HEAD_DIM)
        sel_lane = NSA_SEL_LANE0 + (row_s >> 6)
        ks_ref[pad:, :] = jnp.where(lane_s < HEAD_DIM, kn,
                                    jnp.where(lane_s == sel_lane, 1.0, 0.0)).astype(BF16)
        kw_ref[pad:, :] = jnp.where(lane_s < HEAD_DIM, pltpu.roll(kn, HEAD_DIM, 1),
                                    0.0).astype(BF16)
        lane_p = lax.broadcasted_iota(jnp.int32, (pad, LANES), 1)
        before = jnp.where(lane_p == NSA_PAD_LANE, MASKED, 0.0).astype(BF16)
        ks_ref[:pad, :] = before
        kw_ref[:pad, :] = before
        vt_ref[:, pad:] = v2_ref[0].T.astype(BF16)
        vt_ref[:, :pad] = jnp.zeros((LANES, pad), BF16)

    lane = lax.broadcasted_iota(jnp.int32, (tq, LANES), 1)
    lo = lane < HEAD_DIM
    scale = HEAD_DIM ** -0.5 * LOG2E
    kc = kc_ref[0]
    vct = vct_ref[0, :HEAD_DIM, :]
    jj = lax.broadcasted_iota(jnp.int32, (LANES, LANES), 0)
    c0 = lax.broadcasted_iota(jnp.int32, (LANES, LANES), 1) * NSA_CMP_STRIDE
    j0 = jj * NSA_SEL_LEN
    overlap = jnp.where((c0 < j0 + NSA_SEL_LEN) & (c0 + NSA_CMP_LEN > j0), 1.0, 0.0).astype(BF16)
    jj32 = lax.broadcasted_iota(jnp.int32, (n_sel, tq), 0)
    in_sel = (lane >= NSA_SEL_LANE0) & (lane < NSA_SEL_LANE0 + n_sel)
    blocks = [dict() for _ in range(n_blk)]

    def prepare(u):
        blk = blocks[u]
        i = ip * n_blk + u
        q = q_ref[0, u * tq:(u + 1) * tq, :]
        q_base = []
        for pair in range(2):
            cols = slice(pair * LANES, (pair + 1) * LANES)
            pn = _pair_rms(q[:, cols], gq_ref[:, cols], lo) * scale
            for head in (jnp.where(lo, pn, 0.0), pltpu.roll(jnp.where(lo, 0.0, pn), HEAD_DIM, 1)):
                q_base.append(jnp.where(lane == NSA_PAD_LANE, 1.0, head))
        q_plain = [x.astype(BF16) for x in q_base]
        b0 = pl.multiple_of(i * tq, tq)
        blk.update(i=i, q_base=q_base, q_plain=q_plain,
                   ks_band=ks_ref[pl.ds(b0, band), :],
                   vs_band=vt_ref[:HEAD_DIM, pl.ds(b0, band)],
                   vw_band=vt_ref[HEAD_DIM:, pl.ds(b0, band)])
        blk['s_cmp'] = [_nt_dot(kc, q_plain[h]) + cmpb_ref[h, :, u * tq:(u + 1) * tq]
                        for h in range(N_HEADS)]
        kw_band = kw_ref[pl.ds(b0, band), :]
        for h in range(N_HEADS):
            win_ref[u, h] = _nt_dot(kw_band, q_plain[h])

    def window(u, h):
        _, l_w, a_w = _first_t(win_ref[u, h] + bwin_ref[h], blocks[u]['vw_band'])
        return a_w / l_w

    def compressed(u):
        blk = blocks[u]
        o_cmp = []
        p_sum = jnp.zeros((LANES, tq), F32)
        for h in range(N_HEADS):
            s_c = blk['s_cmp'][h]
            e_c = jnp.where(s_c > 0.5 * MASKED,
                            jnp.exp2(s_c - jnp.max(s_c, axis=0, keepdims=True)), 0.0)
            den = jnp.sum(e_c, axis=0, keepdims=True)
            p_c = e_c / jnp.where(den > 0.0, den, 1.0)
            o_cmp.append(_dot(vct, p_c.astype(BF16)))
            p_sum = p_sum + p_c
        hi, mid, lo3 = _split3(p_sum)
        blk['o_cmp'] = o_cmp
        blk['imp'] = (_dot(overlap, hi) + _dot(overlap, mid) + _dot(overlap, lo3))[0:n_sel]

    def select(u):
        blk = blocks[u]
        i = blk['i']
        cur = (i * tq + lax.broadcasted_iota(jnp.int32, (n_sel, tq), 1)) >> 6
        forced = (jj32 == 0) | (jj32 == cur) | (jj32 == cur - 1)
        imp = jnp.where(forced, FORCE, blk['imp'])
        imp = jnp.where(jj32 <= cur, imp, -FORCE)
        cnt = jnp.zeros((n_sel, tq), F32)
        for jp in range(n_sel):
            other = imp[jp:jp + 1, :]
            beats = (other > imp) | ((other == imp) & (jj32 > jp))
            cnt = cnt + jnp.where(beats, 1.0, 0.0)
        sel_neg = jnp.where(cnt < float(NSA_TOP_N), 0.0, MASKED)
        sel_neg = jnp.concatenate([sel_neg, jnp.zeros((LANES - n_sel, tq), F32)], axis=0).T
        sel_neg = pltpu.roll(sel_neg, NSA_SEL_LANE0, 1)
        band_block0 = NSA_SEL_LANE0 + ((i * tq - pad) >> 6)
        blk['q_band'] = [jnp.where(in_sel, sel_neg, x).astype(BF16) for x in blk['q_base']]
        blk['q_far'] = [jnp.where(in_sel, jnp.where(lane >= band_block0, MASKED, sel_neg),
                                  x).astype(BF16) for x in blk['q_base']]

    def stage_selected(u, h):
        stage_ref[u, h % 2] = _nt_dot(blocks[u]['ks_band'], blocks[u]['q_band'][h])

    def selected_band(u):
        blk = blocks[u]
        slc = []
        for h in range(N_HEADS):
            slc.append(_first_t(stage_ref[u, h % 2] + bslc_ref[h], blk['vs_band']))
            if h + 2 < N_HEADS:
                stage_selected(u, h + 2)
        blk['slc'] = slc

    def far_tile(kt, q_far, state):
        off = pad + kt * NSA_FAR_TILE
        k_far = ks_ref[pl.ds(off, NSA_FAR_TILE), :]
        v_far = vt_ref[:HEAD_DIM, pl.ds(off, NSA_FAR_TILE)]
        sc = [_nt_dot(k_far, q_far[0]), _nt_dot(k_far, q_far[1])]
        out = []
        for h in range(N_HEADS):
            if h + 2 < N_HEADS:
                sc.append(_nt_dot(k_far, q_far[h + 2]))
            out.append(_update_t(state[h], sc[h], v_far, None))
        return out

    def finish(u, o_slc):
        blk = blocks[u]
        rows = slice(u * tq, (u + 1) * tq)
        g_t = _sigmoid(misc_ref[0, rows, :]).T
        heads = []
        for h in range(N_HEADS):
            row = lambda n: g_t[MISC_G + n * N_HEADS + h:MISC_G + n * N_HEADS + h + 1, :]
            heads.append(row(0) * blk['o_cmp'][h] + row(1) * o_slc[h][:, rows]
                         + row(2) * blk['o_win'][h])
        o_ref[0, rows, :] = jnp.concatenate(heads, axis=0).T.astype(o_ref.dtype)

    both = range(n_blk)
    for u in both:
        prepare(u)
    for u in both:
        compressed(u)
    for u in both:
        blocks[u]['o_win'] = [window(u, 0), window(u, 1)]
    for u in both:
        select(u)
    for u in both:
        stage_selected(u, 0)
        stage_selected(u, 1)
    for u in both:
        blocks[u]['o_win'] += [window(u, 2), window(u, 3)]
    for u in both:
        selected_band(u)

    state = [tuple(jnp.concatenate([blocks[u]['slc'][h][n] for u in both], axis=1)
                   for n in range(3)) for h in range(N_HEADS)]
    q_far = [jnp.concatenate([blocks[u]['q_far'][h] for u in both], axis=0)
             for h in range(N_HEADS)]

    for c in range(s // (n_blk * tq)):
        @pl.when(ip == c)
        def _(c=c):
            last = (c + 1) * n_blk - 1
            n_far = (max(last * tq - pad, 0) + NSA_FAR_TILE - 1) // NSA_FAR_TILE
            st = state
            for kt in range(n_far):
                st = far_tile(kt, q_far, st)
            o_slc = [acc / l for (_, l, acc) in st]
            for u in both:
                finish(u, o_slc)


def nsa_attention(y3, kc, vct, band_slc, band_win, cmpb, gq, gk2):
    b, s, _ = y3.shape
    n_blk = 4
    tq = n_blk * LANES
    band = NSA_WINDOW + LANES
    return pl.pallas_call(
        _nsa_kernel,
        grid=(b, s // tq),
        in_specs=[pl.BlockSpec((1, tq, MIX_W), lambda bi, i: (bi, i, COL_NSAQ // MIX_W)),
                  pl.BlockSpec((1, s, LANES), lambda bi, i: (bi, 0, COL_K2 // LANES)),
                  pl.BlockSpec((1, s, LANES), lambda bi, i: (bi, 0, COL_V2 // LANES)),
                  pl.BlockSpec((1, tq, LANES), lambda bi, i: (bi, i, COL_MISC // LANES)),
                  pl.BlockSpec((1, LANES, LANES), lambda bi, i: (bi, 0, 0)),
                  pl.BlockSpec((1, LANES, LANES), lambda bi, i: (bi, 0, 0)),
                  pl.BlockSpec(band_slc.shape, lambda bi, i: (0, 0, 0)),
                  pl.BlockSpec(band_win.shape, lambda bi, i: (0, 0, 0)),
                  pl.BlockSpec((N_HEADS, LANES, tq), lambda bi, i: (0, 0, i)),
                  pl.BlockSpec((1, MIX_W), lambda bi, i: (0, 0)),
                  pl.BlockSpec((1, LANES), lambda bi, i: (0, 0))],
        out_specs=pl.BlockSpec((1, tq, MIX_W), lambda bi, i: (bi, i, 0)),
        out_shape=jax.ShapeDtypeStruct((b, s, MIX_W), BF16),
        scratch_shapes=[pltpu.VMEM((s + NSA_WINDOW, LANES), BF16),
                        pltpu.VMEM((s + NSA_WINDOW, LANES), BF16),
                        pltpu.VMEM((LANES, s + NSA_WINDOW), BF16),
                        pltpu.VMEM((n_blk, N_HEADS, band, LANES), F32),
                        pltpu.VMEM((n_blk, 2, band, LANES), F32)],
        compiler_params=_cparams("parallel", "arbitrary"),
    )(y3, y3, y3, y3, kc, vct, band_slc, band_win, cmpb, gq, gk2)


def _merge_kernel(ysb_ref, ymla_ref, ynsa_ref, yfox_ref, h_ref, wgate_ref, wb_ref, wo_ref, x_ref,
                  o_ref):
    d = x_ref.shape[1]
    h = h_ref[...]
    u = None
    for n, y_ref in enumerate((ysb_ref, ymla_ref, ynsa_ref, yfox_ref)):
        gate = _sigmoid(_dot(h, wgate_ref[:, n * d:(n + 1) * d]))
        term = gate * _dot(y_ref[...], wb_ref[n])
        u = term if u is None else u + term
    o_ref[...] = x_ref[...] + _dot(u.astype(BF16), wo_ref[...])


def merge_branches(ys, h, wgate_all, wb_all, wo_all, layer, x, *, tm=512):
    t, d = x.shape
    yspec = pl.BlockSpec((tm, MIX_W), lambda i: (i, 0))
    return pl.pallas_call(
        _merge_kernel, grid=(t // tm,),
        in_specs=[yspec] * 4 + [
            pl.BlockSpec((tm, d), lambda i: (i, 0)),
            pl.BlockSpec((None, d, N_BRANCH * d), lambda i: (layer, 0, 0)),
            pl.BlockSpec((None,) + wb_all.shape[1:], lambda i: (layer, 0, 0, 0)),
            pl.BlockSpec((None, d, d), lambda i: (layer, 0, 0)),
            pl.BlockSpec((tm, d), lambda i: (i, 0))],
        out_specs=pl.BlockSpec((tm, d), lambda i: (i, 0)),
        out_shape=jax.ShapeDtypeStruct((t, d), F32),
        compiler_params=_cparams("parallel"),
    )(*ys, h, wgate_all, wb_all, wo_all, x)


def _mlp_ple_kernel(x_ref, g_ref, wu_ref, wd_ref, gp_ref, wg_ref, p_ref, wp_ref, o_ref,
                    h_ref, acc_ref):
    f = pl.program_id(1)

    @pl.when(f == 0)
    def _():
        h_ref[...] = _rms_rows(x_ref[...], g_ref[...]).astype(BF16)
        acc_ref[...] = jnp.zeros_like(acc_ref)

    a = jnp.maximum(_dot(h_ref[...], wu_ref[...]), 0.0)
    acc_ref[...] += _dot((a * a).astype(BF16), wd_ref[...])

    @pl.when(f == pl.num_programs(1) - 1)
    def _():
        x1 = x_ref[...] + acc_ref[...]
        gate = _sigmoid(_dot(_rms_rows(x1, gp_ref[...]).astype(BF16), wg_ref[...]))
        o_ref[...] = x1 + gate * _dot(p_ref[...].astype(BF16), wp_ref[...])


def mlp_ple(x, g, wu, wd, gp, wg, p_all, wp, layer, *, tm=1024, tf=1024):
    t, d = x.shape
    ff = wu.shape[2]
    return pl.pallas_call(
        _mlp_ple_kernel, grid=(t // tm, ff // tf),
        in_specs=[pl.BlockSpec((tm, d), lambda i, f: (i, 0)),
                  pl.BlockSpec((1, d), lambda i, f: (0, 0)),
                  pl.BlockSpec((None, d, tf), lambda i, f: (layer, 0, f)),
                  pl.BlockSpec((None, tf, d), lambda i, f: (layer, f, 0)),
                  pl.BlockSpec((1, d), lambda i, f: (0, 0)),
                  pl.BlockSpec((None, d, d), lambda i, f: (layer, 0, 0)),
                  pl.BlockSpec((None, tm, PLE_DIM), lambda i, f: (layer, i, 0)),
                  pl.BlockSpec((None, PLE_DIM, d), lambda i, f: (layer, 0, 0))],
        out_specs=pl.BlockSpec((tm, d), lambda i, f: (i, 0)),
        out_shape=jax.ShapeDtypeStruct((t, d), F32),
        scratch_shapes=[pltpu.VMEM((tm, d), BF16), pltpu.VMEM((tm, d), F32)],
        compiler_params=_cparams("parallel", "arbitrary"),
    )(x, g.reshape(1, d), wu, wd, gp.reshape(1, d), wg, p_all, wp)


def _pack_w_in(w):
    offs = np.concatenate([[0], np.cumsum(IN_WIDTHS)]).tolist()
    (sb_q, sb_k, sb_v, cq, ckv, kr, nsa_q, kc, vc, ks, vs, kw, vw, ng,
     fox_q, fox_k, fox_v, ff, gate) = [w[..., offs[n]:offs[n + 1]] for n in range(len(IN_WIDTHS))]
    z = lambda n: jnp.zeros(w.shape[:-1] + (n,), w.dtype)
    misc = jnp.concatenate([ff, ng, z(MISC_KR - MISC_G - 3 * N_HEADS), kr,
                            z(LANES - MISC_KR - MLA_ROPE)], axis=-1)
    w_f32_part = jnp.concatenate([cq, ckv, ks, kw, vs, vw, kc, vc, misc, nsa_q], axis=-1)
    w_bf16_part = jnp.concatenate([sb_q, sb_k, sb_v, fox_q, fox_k, fox_v], axis=-1)
    return w_f32_part.astype(BF16), w_bf16_part.astype(BF16), gate.astype(BF16)


def _head_slots(w, width):
    k = w.shape[0]
    w = w.reshape(k, N_HEADS, width)
    return jnp.pad(w, ((0, 0), (0, 0), (0, LANES - width))).reshape(k, N_HEADS * LANES)


def _rope_tables(s):
    half = MLA_ROPE // 2
    inv = jnp.exp(-math.log(ROPE_THETA) * jnp.arange(half, dtype=F32) / half)
    ang = jnp.arange(s, dtype=F32)[:, None] * inv[None, :]
    cos, sin = jnp.cos(ang), jnp.sin(ang)
    ones = jnp.ones((s, MLA_NOPE), F32)
    zeros = lambda n: jnp.zeros((s, n), F32)
    tail = LANES - MLA_QK
    cos_t = jnp.concatenate([ones, cos, cos, jnp.ones((s, tail), F32)], axis=1)
    sa_t = jnp.concatenate([zeros(MLA_NOPE), -sin, zeros(half), zeros(tail)], axis=1)
    sb_t = jnp.concatenate([zeros(MLA_NOPE), zeros(half), sin, zeros(tail)], axis=1)
    return cos_t, sa_t, sb_t


def _pad_lanes(v, left=0):
    v = v.reshape(1, -1)
    return jnp.pad(v, ((0, 0), (left, LANES - left - v.shape[1])))


def _pad_to_lanes(w):
    return jnp.pad(w, ((0, 0), (0, LANES - w.shape[1])))


def kernel(x, p, rel_bias, norm_mix_g, w_in, mla_cq_norm_g, mla_ckv_norm_g, mla_w_uq, mla_w_ukv,
           mla_qn_g, mla_kn_g, nsa_pe_k, nsa_pe_v, nsa_w1_k, nsa_w2_k, nsa_w1_v, nsa_w2_v,
           nsa_qn_g, nsa_kn_g, fox_f_bias, fox_qn_g, fox_kn_g, w_branch, w_o, norm_mlp_g,
           w_mlp_up, w_mlp_down, norm_ple_g, w_ple_gate, w_ple_proj):
    b, s, d = x.shape
    t = b * s
    xf = x.reshape(t, d)
    p_all = p.reshape(DEPTH, t, PLE_DIM)
    band_slc, band_win, cmpb = rel_bias_tables(rel_bias.astype(F32), s)
    cos_t, sa_t, sb_t = _rope_tables(s)
    n_cmp_in = NSA_CMP_STRIDE * HEAD_DIM
    w_a, w_b, w_gate = _pack_w_in(w_in)
    wb_all, wo_all = w_branch.astype(BF16), w_o.astype(BF16)
    wu_all, wd_all = w_mlp_up.astype(BF16), w_mlp_down.astype(BF16)
    wg_all, wp_all = w_ple_gate.astype(BF16), w_ple_proj.astype(BF16)

    for i in range(DEPTH):
        ya, h = norm_proj(xf, norm_mix_g[i], w_a, i)
        yb = matmul_bf16(h, w_b, i)
        y3 = ya.reshape(b, s, N_F32)
        yb3 = yb.reshape(b, s, N_BF16)

        y_sb = sb_attention(yb3)

        wukv = mla_w_ukv[i].reshape(MLA_KV_LORA, N_HEADS, MLA_NOPE + MLA_V)
        q_m, k_m, vt_m = mla_prep(
            y3, mla_cq_norm_g[i].reshape(1, -1), mla_ckv_norm_g[i].reshape(1, -1),
            _head_slots(mla_w_uq[i], MLA_QK).astype(BF16),
            _head_slots(wukv[:, :, :MLA_NOPE].reshape(MLA_KV_LORA, -1), MLA_NOPE).astype(BF16),
            wukv[:, :, MLA_NOPE:].reshape(MLA_KV_LORA, -1).T.astype(BF16),
            _pad_lanes(mla_qn_g[i]), _pad_lanes(mla_kn_g[i]), cos_t, sa_t, sb_t)
        y_mla = mla_attention(q_m, k_m, vt_m)

        pad_sq = lambda w: jnp.pad(_pad_to_lanes(w), ((0, LANES - w.shape[0]), (0, 0)))
        pe_kv = jnp.concatenate([nsa_pe_k[i], nsa_pe_v[i]], axis=1)
        kc, vct = nsa_compress(
            y3, pe_kv[:NSA_CMP_STRIDE], pe_kv[NSA_CMP_STRIDE:],
            *_compress_weights(nsa_w1_k[i], 0), *_compress_weights(nsa_w1_v[i], HEAD_DIM),
            pad_sq(nsa_w2_k[i]).astype(BF16), pad_sq(nsa_w2_v[i].T).astype(BF16),
            _pad_lanes(nsa_kn_g[i, 0]))
        y_nsa = nsa_attention(
            y3, kc, vct, band_slc, band_win, cmpb,
            jnp.tile(nsa_qn_g[i], N_HEADS).reshape(1, -1),
            jnp.concatenate([nsa_kn_g[i, 1], nsa_kn_g[i, 2]]).reshape(1, -1))

        cum = fox_cum(y3, _pad_lanes(fox_f_bias[i], MISC_F))
        y_fox = fox_attention(yb3, cum, jnp.tile(fox_qn_g[i], N_HEADS).reshape(1, -1),
                              jnp.tile(fox_kn_g[i], N_HEADS).reshape(1, -1))

        ys = [a.reshape(t, MIX_W) for a in (y_sb, y_mla, y_nsa, y_fox)]
        xf = merge_branches(ys, h, w_gate, wb_all, wo_all, i, xf)
        xf = mlp_ple(xf, norm_mlp_g[i], wu_all, wd_all, norm_ple_g[i], wg_all, p_all, wp_all, i)
    return xf.reshape(b, s, d)
```
